```python
import math
import jax, jax.numpy as jnp
from jax import lax
import numpy as np

D_MODEL = 2048
BATCH = 16
SEQ = 2048
DEPTH = 1
DEC_BATCH = 2
DEC_SEQ = 4096
PAST_LEN = 128

GRID_W = 64
NA_HEAD_DIM = 64
NA_WIDTH = D_MODEL // 4
NA_HEADS = NA_WIDTH // NA_HEAD_DIM
NA_KR = 8
NA_KC = 16
SW_HEAD_DIM = 64
SW_WIDTH = D_MODEL // 2
SW_HEADS = SW_WIDTH // SW_HEAD_DIM
SW_KV_HEADS = SW_HEADS // 4
SW_KV_WIDTH = SW_KV_HEADS * SW_HEAD_DIM
SW_WINDOW = 128
SW_BLOCK = 128
MEM_LEN = 256
MX_HEADS = 4
MX_WIDTH = D_MODEL // 4
MX_HEAD_DIM = MX_WIDTH // MX_HEADS
T5_BUCKETS = 32
T5_MAX_DIST = 128
N_BRANCHES = 3
IN_SIZES = (NA_WIDTH, NA_WIDTH, NA_WIDTH, SW_WIDTH, SW_KV_WIDTH, SW_KV_WIDTH, MX_WIDTH, N_BRANCHES * D_MODEL)
IN_COLS = sum(IN_SIZES)
N_EXPERTS = 16
EC_CAPACITY = 2
EXPERT_FF = D_MODEL
RMS_EPS = 1e-6
NEG_INF = -1e30

kernel_name = 'hybrid_na_swa_mem_ec_encoder'

F32 = jnp.float32


def rms_norm(x, g):
    x32 = x.astype(F32)
    y = x32 * lax.rsqrt(jnp.mean(x32 * x32, axis=-1, keepdims=True) + RMS_EPS)
    return (y * g.astype(F32)).astype(x.dtype)


def t5_bucket(rel):
    half = T5_BUCKETS // 2
    max_exact = half // 2
    n = jnp.abs(rel)
    nf = jnp.maximum(n, 1).astype(F32)
    large = max_exact + (jnp.log(nf / max_exact) / math.log(T5_MAX_DIST / max_exact) * (half - max_exact)).astype(jnp.int32)
    large = jnp.minimum(large, half - 1)
    return jnp.where(rel > 0, half, 0) + jnp.where(n < max_exact, n, large)


def neighborhood_attention(q, k, v, rpb):
    B, T, H, Dh = q.shape
    rows = T // GRID_W
    kr = min(NA_KR, rows)
    r = np.arange(rows)
    col = np.arange(GRID_W)
    row_start = np.clip(r - kr // 2, 0, rows - kr)
    key_rows = row_start[:, None] + np.arange(kr)[None, :]
    key_tok = (key_rows[:, :, None] * GRID_W + col[None, None, :]).reshape(rows, kr * GRID_W)
    kg = jnp.take(k, key_tok, axis=1)
    vg = jnp.take(v, key_tok, axis=1)
    qg = q.reshape(B, rows, GRID_W, H, Dh)
    s = jnp.einsum('brqhd,brkhd->brhqk', qg, kg, preferred_element_type=F32) * (Dh ** -0.5)
    col_start = np.clip(col - NA_KC // 2, 0, GRID_W - NA_KC)
    in_win = (col[None, :] >= col_start[:, None]) & (col[None, :] < col_start[:, None] + NA_KC)
    dr = key_rows - r[:, None] + NA_KR - 1
    dc = np.clip(col[None, :] - col[:, None] + NA_KC - 1, 0, 2 * NA_KC - 2)
    bias = rpb[:, dr[:, None, :, None], dc[None, :, None, :]].astype(F32)
    bias = jnp.where(jnp.asarray(in_win)[None, None, :, None, :], bias, NEG_INF)
    bias = bias.transpose(1, 0, 2, 3, 4).reshape(rows, H, GRID_W, kr * GRID_W)
    p = jax.nn.softmax(s + bias[None], axis=-1)
    o = jnp.einsum('brhqk,brkhd->brqhd', p.astype(v.dtype), vg)
    return o.reshape(B, T, H * Dh)


def window_gqa_sink(q, k, v, t5_table, sink):
    B, T, H, Dh = q.shape
    KVH = k.shape[2]
    G = H // KVH
    nb = T // SW_BLOCK
    span = SW_BLOCK + 2 * SW_WINDOW
    pad = ((0, 0), (SW_WINDOW, SW_WINDOW), (0, 0), (0, 0))
    kp = jnp.pad(k, pad)
    vp = jnp.pad(v, pad)
    idx = np.arange(nb)[:, None] * SW_BLOCK + np.arange(span)[None, :]
    kg = jnp.take(kp, idx, axis=1)
    vg = jnp.take(vp, idx, axis=1)
    qb = q.reshape(B, nb, SW_BLOCK, KVH, G, Dh)
    s = jnp.einsum('bnqcgd,bnjcd->bncgqj', qb, kg, preferred_element_type=F32) * (Dh ** -0.5)
    rel = np.arange(span)[None, :] - SW_WINDOW - np.arange(SW_BLOCK)[:, None]
    kpos = idx - SW_WINDOW
    valid = (np.abs(rel) <= SW_WINDOW)[None] & ((kpos >= 0) & (kpos < T))[:, None, :]
    bias = t5_table[t5_bucket(jnp.asarray(rel, dtype=jnp.int32))].astype(F32)
    bias = bias.transpose(2, 0, 1).reshape(KVH, G, SW_BLOCK, span)
    s = jnp.where(jnp.asarray(valid)[None, :, None, None], s + bias[None, None], NEG_INF)
    sink_l = sink.astype(F32).reshape(KVH, G)[None, None, :, :, None, None]
    m = jnp.maximum(jnp.max(s, axis=-1, keepdims=True), sink_l)
    p = jnp.exp(s - m)
    p = p / (jnp.sum(p, axis=-1, keepdims=True) + jnp.exp(sink_l - m))
    o = jnp.einsum('bncgqj,bnjcd->bnqcgd', p.astype(v.dtype), vg)
    return o.reshape(B, T, H * Dh)


def memory_cross_attention(q, mk, mv):
    B, T, H, Dh = q.shape
    s = jnp.einsum('bthd,bmhd->bhtm', q, mk, preferred_element_type=F32) * (Dh ** -0.5)
    p = jax.nn.softmax(s, axis=-1)
    o = jnp.einsum('bhtm,bmhd->bthd', p.astype(mv.dtype), mv)
    return o.reshape(B, T, H * Dh)


def expert_choice_ffn(h, w_router, w_gate, w_up, w_down):
    B, T, D = h.shape
    n_tok = B * T
    cap = EC_CAPACITY * n_tok // N_EXPERTS
    hf = h.reshape(n_tok, D)
    logits = jnp.einsum('nd,de->ne', hf, w_router, preferred_element_type=F32)
    aff = jax.nn.softmax(logits, axis=-1)
    top_val, top_idx = lax.top_k(aff.T, cap)
    xs = jnp.take(hf, top_idx, axis=0)
    a = jnp.einsum('ecd,edf->ecf', xs, w_gate)
    b = jnp.einsum('ecd,edf->ecf', xs, w_up)
    out = jnp.einsum('ecf,efd->ecd', jax.nn.silu(a) * b, w_down) * top_val[..., None].astype(h.dtype)
    y = jnp.zeros_like(hf).at[top_idx.reshape(-1)].add(out.reshape(-1, D))
    return y.reshape(B, T, D)


def encoder_layer(x, mem, g_mix, g_mem, w_in, w_mem_kv, na_rpb, t5_table, sw_sink,
                  w_na_o, w_sw_o, w_mx_o, w_out, g_ffn, w_router, w_e_gate, w_e_up, w_e_down):
    B, T, D = x.shape
    h = rms_norm(x, g_mix)
    z = jnp.einsum('btd,dc->btc', h, w_in)
    splits = [int(s) for s in np.cumsum(IN_SIZES)[:-1]]
    na_q, na_k, na_v, sw_q, sw_k, sw_v, mx_q, gate_logits = jnp.split(z, splits, axis=-1)
    na_shape = (B, T, NA_HEADS, NA_HEAD_DIM)
    o_na = neighborhood_attention(na_q.reshape(na_shape), na_k.reshape(na_shape), na_v.reshape(na_shape), na_rpb)
    o_sw = window_gqa_sink(sw_q.reshape(B, T, SW_HEADS, SW_HEAD_DIM),
                           sw_k.reshape(B, T, SW_KV_HEADS, SW_HEAD_DIM),
                           sw_v.reshape(B, T, SW_KV_HEADS, SW_HEAD_DIM), t5_table, sw_sink)
    mkv = jnp.einsum('bmd,dc->bmc', rms_norm(mem, g_mem), w_mem_kv).reshape(B, mem.shape[1], 2, MX_HEADS, MX_HEAD_DIM)
    o_mx = memory_cross_attention(mx_q.reshape(B, T, MX_HEADS, MX_HEAD_DIM), mkv[:, :, 0], mkv[:, :, 1])
    gates = jax.nn.sigmoid(gate_logits.astype(F32)).reshape(B, T, N_BRANCHES, D)
    merged = (gates[:, :, 0] * jnp.einsum('btc,cd->btd', o_na, w_na_o)
              + gates[:, :, 1] * jnp.einsum('btc,cd->btd', o_sw, w_sw_o)
              + gates[:, :, 2] * jnp.einsum('btc,cd->btd', o_mx, w_mx_o)).astype(x.dtype)
    x = x + jnp.einsum('btc,cd->btd', merged, w_out)
    x = x + expert_choice_ffn(rms_norm(x, g_ffn), w_router, w_e_gate, w_e_up, w_e_down)
    return x


def encoder_trunk(x, mem, g_mix, g_mem, w_in, w_mem_kv, na_rpb, t5_table, sw_sink,
                  w_na_o, w_sw_o, w_mx_o, w_out, g_ffn, w_router, w_e_gate, w_e_up, w_e_down, g_final):
    for l in range(DEPTH):
        x = encoder_layer(x, mem, g_mix[l], g_mem[l], w_in[l], w_mem_kv[l], na_rpb[l], t5_table, sw_sink[l],
                          w_na_o[l], w_sw_o[l], w_mx_o[l], w_out[l], g_ffn[l], w_router[l],
                          w_e_gate[l], w_e_up[l], w_e_down[l])
    return rms_norm(x, g_final)


def setup_inputs(seed: int = 0) -> dict:
    key = jax.random.key(seed)
    ks = jax.random.split(key, 22)

    def nrm(k, shape, scale):
        return jax.random.normal(k, shape, F32) * scale

    D = D_MODEL
    return {
        'x_prompt': nrm(ks[0], (BATCH, SEQ, D), 1.0),
        'x_sample': nrm(ks[1], (DEC_BATCH, DEC_SEQ, D), 1.0),
        'mem_prompt': nrm(ks[2], (BATCH, MEM_LEN, D), 1.0),
        'mem_sample': nrm(ks[3], (DEC_BATCH, MEM_LEN, D), 1.0),
        'g_mix': 1.0 + nrm(ks[4], (DEPTH, D), 0.05),
        'g_mem': 1.0 + nrm(ks[5], (DEPTH, D), 0.05),
        'w_in': nrm(ks[6], (DEPTH, D, IN_COLS), D ** -0.5),
        'w_mem_kv': nrm(ks[7], (DEPTH, D, 2 * MX_WIDTH), D ** -0.5),
        'na_rpb': nrm(ks[8], (DEPTH, NA_HEADS, 2 * NA_KR - 1, 2 * NA_KC - 1), 0.1),
        't5_table': nrm(ks[9], (T5_BUCKETS, SW_HEADS), 0.1),
        'sw_sink': nrm(ks[10], (DEPTH, SW_HEADS), 0.5),
        'w_na_o': nrm(ks[11], (DEPTH, NA_WIDTH, D), NA_WIDTH ** -0.5),
        'w_sw_o': nrm(ks[12], (DEPTH, SW_WIDTH, D), SW_WIDTH ** -0.5),
        'w_mx_o': nrm(ks[13], (DEPTH, MX_WIDTH, D), MX_WIDTH ** -0.5),
        'w_out': nrm(ks[14], (DEPTH, D, D), D ** -0.5),
        'g_ffn': 1.0 + nrm(ks[15], (DEPTH, D), 0.05),
        'w_router': nrm(ks[16], (DEPTH, D, N_EXPERTS), D ** -0.5),
        'w_e_gate': nrm(ks[17], (DEPTH, N_EXPERTS, D, EXPERT_FF), D ** -0.5),
        'w_e_up': nrm(ks[18], (DEPTH, N_EXPERTS, D, EXPERT_FF), D ** -0.5),
        'w_e_down': nrm(ks[19], (DEPTH, N_EXPERTS, EXPERT_FF, D), EXPERT_FF ** -0.5),
        'g_final': 1.0 + nrm(ks[20], (D,), 0.05),
    }


def reference(x_prompt, x_sample, mem_prompt, mem_sample, g_mix, g_mem, w_in, w_mem_kv, na_rpb, t5_table,
              sw_sink, w_na_o, w_sw_o, w_mx_o, w_out, g_ffn, w_router, w_e_gate, w_e_up, w_e_down, g_final):
    y_prompt = encoder_trunk(x_prompt, mem_prompt, g_mix, g_mem, w_in, w_mem_kv, na_rpb, t5_table, sw_sink,
                             w_na_o, w_sw_o, w_mx_o, w_out, g_ffn, w_router, w_e_gate, w_e_up, w_e_down, g_final)
    y_sample = encoder_trunk(x_sample, mem_sample, g_mix, g_mem, w_in, w_mem_kv, na_rpb, t5_table, sw_sink,
                             w_na_o, w_sw_o, w_mx_o, w_out, g_ffn, w_router, w_e_gate, w_e_up, w_e_down, g_final)
    return (y_prompt, y_sample)
```

```python
import functools
import math

import numpy as np
import jax
import jax.numpy as jnp
from jax import lax
from jax.experimental import pallas as pl
from jax.experimental.pallas import tpu as pltpu

F32 = jnp.float32
BF16 = jnp.bfloat16
I32 = jnp.int32

RMS_EPS = 1e-6
NEG_INF = -1e30

GRID_W = 64
NA_HEADS = 8
NA_HEAD_DIM = 64
NA_KR = 8
NA_KC = 16
SW_HEADS = 16
SW_KV_HEADS = 4
SW_HEAD_DIM = 64
SW_WINDOW = 128
SW_BLOCK = 128
MX_HEADS = 4
T5_BUCKETS = 32
T5_MAX_DIST = 128
N_BRANCHES = 3
N_EXPERTS = 16
EC_CAPACITY = 2

LANES = 128
V7X_VMEM_BYTES = 64 * 1024 * 1024
VMEM_LIMIT = V7X_VMEM_BYTES * 7 // 8

NT_DIMS = (((1,), (1,)), ((), ()))
TN_DIMS = (((0,), (0,)), ((), ()))


def _params(*sem):
    return pltpu.CompilerParams(dimension_semantics=sem, vmem_limit_bytes=VMEM_LIMIT)


def _rms(x, g):
    return x * lax.rsqrt(jnp.mean(x * x, axis=-1, keepdims=True) + RMS_EPS) * g


def _norm_proj_kernel(x_ref, g_ref, w_ref, *o_refs, scales):
    h = _rms(x_ref[...], g_ref[...]).astype(BF16)
    off = 0
    for o_ref, sc in zip(o_refs, scales):
        width = o_ref.shape[1]
        for c0 in range(0, width, 512):
            cw = min(512, width - c0)
            r = jnp.dot(h, w_ref[:, off + c0:off + c0 + cw], preferred_element_type=F32)
            if sc != 1.0:
                r = r * sc
            o_ref[:, c0:c0 + cw] = r.astype(o_ref.dtype)
        off += width


def norm_proj(x, g, w, widths, scales, tm):
    n, d = x.shape
    tm = min(tm, n)
    assert n % tm == 0
    return pl.pallas_call(
        functools.partial(_norm_proj_kernel, scales=tuple(scales)),
        grid=(n // tm,),
        in_specs=[pl.BlockSpec((tm, d), lambda i: (i, 0)),
                  pl.BlockSpec((1, d), lambda i: (0, 0)),
                  pl.BlockSpec(w.shape, lambda i: (0, 0))],
        out_specs=[pl.BlockSpec((tm, c), lambda i: (i, 0)) for c in widths],
        out_shape=[jax.ShapeDtypeStruct((n, c), BF16) for c in widths],
        compiler_params=_params("parallel"),
        name="norm_proj",
    )(x, g, w)


def _softmax_pv(s, v, extra_logit=None):
    m = jnp.max(s, axis=-1, keepdims=True)
    if extra_logit is not None:
        m = jnp.maximum(m, extra_logit)
    e = jnp.exp(s - m)
    den = jnp.sum(e, axis=-1, keepdims=True)
    if extra_logit is not None:
        den = den + jnp.exp(extra_logit - m)
    o = jnp.dot(e.astype(BF16), v, preferred_element_type=F32)
    return o / den


def _na_kernel(q_ref, k_ref, v_ref, bias_ref, o_ref, *, rows, rb):
    j = pl.program_id(1)
    lo = lax.broadcasted_iota(I32, (GRID_W, LANES), 1) < NA_HEAD_DIM
    nkeys = NA_KR * GRID_W

    def body(i, carry):
        r = j * rb + i
        rs = jnp.clip(r - NA_KR // 2, 0, rows - NA_KR)
        off = r - rs
        q = q_ref[0, pl.ds(pl.multiple_of(i * GRID_W, GRID_W), GRID_W), :]
        kk = k_ref[0, pl.ds(pl.multiple_of(rs * GRID_W, GRID_W), nkeys), :]
        vv = v_ref[0, pl.ds(pl.multiple_of(rs * GRID_W, GRID_W), nkeys), :]
        outs = []
        for p in range(NA_HEADS // 2):
            qp = q[:, p * LANES:(p + 1) * LANES]
            kp = kk[:, p * LANES:(p + 1) * LANES]
            vp = vv[:, p * LANES:(p + 1) * LANES]
            halves = []
            for hh in range(2):
                qm = jnp.where(lo if hh == 0 else jnp.logical_not(lo), qp, jnp.zeros_like(qp))
                s = lax.dot_general(qm, kp, NT_DIMS, preferred_element_type=F32)
                s = s + bias_ref[off, 2 * p + hh]
                halves.append(_softmax_pv(s, vp))
            outs.append(jnp.where(lo, halves[0], halves[1]))
        o_ref[0, pl.ds(pl.multiple_of(i * GRID_W, GRID_W), GRID_W), :] = (
            jnp.concatenate(outs, axis=1).astype(o_ref.dtype))
        return carry

    lax.fori_loop(0, rb, body, 0)


def na_attention(q, k, v, bias, rb=8):
    b, t, c = q.shape
    rows = t // GRID_W
    assert rows >= NA_KR and rows % rb == 0
    return pl.pallas_call(
        functools.partial(_na_kernel, rows=rows, rb=rb),
        grid=(b, rows // rb),
        in_specs=[pl.BlockSpec((1, rb * GRID_W, c), lambda i, j: (i, j, 0)),
                  pl.BlockSpec((1, t, c), lambda i, j: (i, 0, 0)),
                  pl.BlockSpec((1, t, c), lambda i, j: (i, 0, 0)),
                  pl.BlockSpec(bias.shape, lambda i, j: (0, 0, 0, 0))],
        out_specs=pl.BlockSpec((1, rb * GRID_W, c), lambda i, j: (i, j, 0)),
        out_shape=jax.ShapeDtypeStruct((b, t, c), BF16),
        compiler_params=_params("parallel", "arbitrary"),
        name="na_attention",
    )(q, k, v, bias)


def na_bias_table(rpb):
    col = np.arange(GRID_W)
    col_start = np.clip(col - NA_KC // 2, 0, GRID_W - NA_KC)
    in_win = (col[None, :] >= col_start[:, None]) & (col[None, :] < col_start[:, None] + NA_KC)
    dc = np.clip(col[None, :] - col[:, None] + NA_KC - 1, 0, 2 * NA_KC - 2)
    dr = np.arange(NA_KR)[None, :] - np.arange(NA_KR)[:, None] + NA_KR - 1
    bias = rpb[:, dr[:, None, :, None], dc[None, :, None, :]].astype(F32)
    bias = jnp.where(jnp.asarray(in_win)[None, None, :, None, :], bias, NEG_INF)
    return bias.transpose(1, 0, 2, 3, 4).reshape(NA_KR, rpb.shape[0], GRID_W, NA_KR * GRID_W)


def _sw_kernel(sink_ref, q_ref, k_ref, v_ref, bias_ref, o_ref, *, nb):
    n = pl.program_id(1)
    blk = SW_BLOCK
    group = SW_HEADS // SW_KV_HEADS

    def rows_of(ref, c):
        return ref[0, pl.ds(pl.multiple_of(c * blk, blk), blk), :]

    cl = jnp.maximum(n - 1, 0)
    cr = jnp.minimum(n + 1, nb - 1)
    k3 = jnp.concatenate([rows_of(k_ref, cl), rows_of(k_ref, n), rows_of(k_ref, cr)], axis=0)
    v3 = jnp.concatenate([rows_of(v_ref, cl), rows_of(v_ref, n), rows_of(v_ref, cr)], axis=0)
    pen_l = jnp.where(n > 0, 0.0, NEG_INF).astype(F32)
    pen_r = jnp.where(n < nb - 1, 0.0, NEG_INF).astype(F32)
    key = lax.broadcasted_iota(I32, (1, 3 * blk), 1)
    pen = jnp.where(key < blk, pen_l, jnp.where(key >= 2 * blk, pen_r, 0.0))
    lo = lax.broadcasted_iota(I32, (blk, LANES), 1) < SW_HEAD_DIM

    for pair in range(SW_KV_HEADS // 2):
        kp = k3[:, pair * LANES:(pair + 1) * LANES]
        vp = v3[:, pair * LANES:(pair + 1) * LANES]
        per_half = []
        for half in range(2):
            c = 2 * pair + half
            keep = lo if half == 0 else jnp.logical_not(lo)
            qs = []
            for g in range(group):
                t = pair * group + g
                qt = q_ref[0, :, t * LANES:(t + 1) * LANES]
                qs.append(jnp.where(keep, qt, jnp.zeros_like(qt)))
            qq = jnp.concatenate(qs, axis=0)
            s = lax.dot_general(qq, kp, NT_DIMS, preferred_element_type=F32)
            s = s + bias_ref[c * group:(c + 1) * group].reshape(group * blk, 3 * blk) + pen
            sink = jnp.concatenate(
                [jnp.full((blk, 1), sink_ref[c * group + g], F32) for g in range(group)], axis=0)
            per_half.append(_softmax_pv(s, vp, extra_logit=sink))
        for g in range(group):
            t = pair * group + g
            o = jnp.where(lo, per_half[0][g * blk:(g + 1) * blk], per_half[1][g * blk:(g + 1) * blk])
            o_ref[0, :, t * LANES:(t + 1) * LANES] = o.astype(o_ref.dtype)


def sw_attention(q, k, v, bias, sink):
    b, t, c = q.shape
    nb = t // SW_BLOCK
    kvw = k.shape[2]
    grid_spec = pltpu.PrefetchScalarGridSpec(
        num_scalar_prefetch=1,
        grid=(b, nb),
        in_specs=[pl.BlockSpec((1, SW_BLOCK, c), lambda i, j, s: (i, j, 0)),
                  pl.BlockSpec((1, t, kvw), lambda i, j, s: (i, 0, 0)),
                  pl.BlockSpec((1, t, kvw), lambda i, j, s: (i, 0, 0)),
                  pl.BlockSpec(bias.shape, lambda i, j, s: (0, 0, 0))],
        out_specs=pl.BlockSpec((1, SW_BLOCK, c), lambda i, j, s: (i, j, 0)),
    )
    return pl.pallas_call(
        functools.partial(_sw_kernel, nb=nb),
        grid_spec=grid_spec,
        out_shape=jax.ShapeDtypeStruct((b, t, c), BF16),
        compiler_params=_params("parallel", "arbitrary"),
        name="sw_attention",
    )(sink, q, k, v, bias)


def _t5_bucket(rel):
    half = T5_BUCKETS // 2
    max_exact = half // 2
    n = jnp.abs(rel)
    nf = jnp.maximum(n, 1).astype(F32)
    large = max_exact + (jnp.log(nf / max_exact) / math.log(T5_MAX_DIST / max_exact)
                         * (half - max_exact)).astype(jnp.int32)
    large = jnp.minimum(large, half - 1)
    return jnp.where(rel > 0, half, 0) + jnp.where(n < max_exact, n, large)


def sw_bias_table(t5_table):
    span = SW_BLOCK + 2 * SW_WINDOW
    rel = np.arange(span)[None, :] - SW_WINDOW - np.arange(SW_BLOCK)[:, None]
    bias = t5_table[_t5_bucket(jnp.asarray(rel, dtype=jnp.int32))].astype(F32)
    bias = jnp.where(jnp.asarray(np.abs(rel) <= SW_WINDOW)[:, :, None], bias, NEG_INF)
    return bias.transpose(2, 0, 1)


def sw_head_perm():
    group = SW_HEADS // SW_KV_HEADS
    cols = []
    for pair in range(SW_KV_HEADS // 2):
        for g in range(group):
            for c in (2 * pair, 2 * pair + 1):
                hq = c * group + g
                cols.append(np.arange(hq * SW_HEAD_DIM, (hq + 1) * SW_HEAD_DIM))
    return np.concatenate(cols)


def _mx_kernel(q_ref, mk_ref, mv_ref, o_ref, *, scale):
    hd = q_ref.shape[2] // MX_HEADS
    for h in range(MX_HEADS):
        sl = slice(h * hd, (h + 1) * hd)
        s = lax.dot_general(q_ref[0, :, sl], mk_ref[0, :, sl], NT_DIMS, preferred_element_type=F32) * scale
        o_ref[0, :, sl] = _softmax_pv(s, mv_ref[0, :, sl]).astype(o_ref.dtype)


def mx_attention(q, mkv, tq=512):
    b, t, c = q.shape
    m = mkv.shape[1]
    tq = min(tq, t)
    return pl.pallas_call(
        functools.partial(_mx_kernel, scale=float((c // MX_HEADS) ** -0.5)),
        grid=(b, t // tq),
        in_specs=[pl.BlockSpec((1, tq, c), lambda i, j: (i, j, 0)),
                  pl.BlockSpec((1, m, c), lambda i, j: (i, 0, 0)),
                  pl.BlockSpec((1, m, c), lambda i, j: (i, 0, 1))],
        out_specs=pl.BlockSpec((1, tq, c), lambda i, j: (i, j, 0)),
        out_shape=jax.ShapeDtypeStruct((b, t, c), BF16),
        compiler_params=_params("parallel", "arbitrary"),
        name="mx_attention",
    )(q, mkv, mkv)


def _merge_kernel(x_ref, ona_ref, osw_ref, omx_ref, gmix_ref, wg_ref, wna_ref, wsw_ref, wmx_ref, wout_ref,
                  gffn_ref, wrh_ref, wrl_ref, x2_ref, h2_ref, aff_ref, h_scr):
    j = pl.program_id(1)

    @pl.when(j == 0)
    def _():
        x = x_ref[...]
        h_scr[...] = _rms(x, gmix_ref[...]).astype(BF16)
        x2_ref[...] = x

    h = h_scr[...]
    merged = None
    for o_ref, w_ref, b in ((ona_ref, wna_ref, 0), (osw_ref, wsw_ref, 1), (omx_ref, wmx_ref, 2)):
        gate = jax.nn.sigmoid(jnp.dot(h, wg_ref[b], preferred_element_type=F32))
        term = gate * jnp.dot(o_ref[...], w_ref[...], preferred_element_type=F32)
        merged = term if merged is None else merged + term
    x2_ref[...] += jnp.dot(merged.astype(BF16), wout_ref[...], preferred_element_type=F32)

    @pl.when(j == pl.num_programs(1) - 1)
    def _():
        h2 = _rms(x2_ref[...], gffn_ref[...])
        h2_ref[...] = h2
        hi = h2.astype(BF16)
        lo = (h2 - hi.astype(F32)).astype(BF16)
        wh = wrh_ref[...]
        logits = (lax.dot_general(wh, hi, NT_DIMS, preferred_element_type=F32)
                  + lax.dot_general(wh, lo, NT_DIMS, preferred_element_type=F32)
                  + lax.dot_general(wrl_ref[...], hi, NT_DIMS, preferred_element_type=F32))
        m = jnp.max(logits, axis=0, keepdims=True)
        e = jnp.exp(logits - m)
        aff = e / jnp.sum(e, axis=0, keepdims=True)
        for c in range(aff_ref.shape[0]):
            aff_ref[c] = aff[:, c * LANES:(c + 1) * LANES]


def merge(x, o_na, o_sw, o_mx, g_mix, w_gate, w_na_o, w_sw_o, w_mx_o, w_out, g_ffn, wr_hi, wr_lo,
          tm=512, tn=256):
    n, d = x.shape
    tm = min(tm, n)
    ne = wr_hi.shape[0]
    row = lambda i, j: (i, 0)
    return pl.pallas_call(
        _merge_kernel,
        grid=(n // tm, d // tn),
        in_specs=[pl.BlockSpec((tm, d), row),
                  pl.BlockSpec((tm, o_na.shape[1]), row),
                  pl.BlockSpec((tm, o_sw.shape[1]), row),
                  pl.BlockSpec((tm, o_mx.shape[1]), row),
                  pl.BlockSpec((1, d), lambda i, j: (0, 0)),
                  pl.BlockSpec((N_BRANCHES, d, tn), lambda i, j: (0, 0, j)),
                  pl.BlockSpec((w_na_o.shape[0], tn), lambda i, j: (0, j)),
                  pl.BlockSpec((w_sw_o.shape[0], tn), lambda i, j: (0, j)),
                  pl.BlockSpec((w_mx_o.shape[0], tn), lambda i, j: (0, j)),
                  pl.BlockSpec((tn, d), lambda i, j: (j, 0)),
                  pl.BlockSpec((1, d), lambda i, j: (0, 0)),
                  pl.BlockSpec((ne, d), lambda i, j: (0, 0)),
                  pl.BlockSpec((ne, d), lambda i, j: (0, 0))],
        out_specs=[pl.BlockSpec((tm, d), row),
                   pl.BlockSpec((tm, d), row),
                   pl.BlockSpec((tm // LANES, ne, LANES), lambda i, j: (i, 0, 0))],
        out_shape=[jax.ShapeDtypeStruct((n, d), F32),
                   jax.ShapeDtypeStruct((n, d), F32),
                   jax.ShapeDtypeStruct((n // LANES, ne, LANES), F32)],
        scratch_shapes=[pltpu.VMEM((tm, d), BF16)],
        compiler_params=_params("parallel", "arbitrary"),
        name="merge",
    )(x, o_na, o_sw, o_mx, g_mix, w_gate, w_na_o, w_sw_o, w_mx_o, w_out, g_ffn, wr_hi, wr_lo)


NA_WIDTH = NA_HEADS * NA_HEAD_DIM
SW_WIDTH = SW_HEADS * SW_HEAD_DIM
SW_KV_WIDTH = SW_KV_HEADS * SW_HEAD_DIM
QKV_WIDTHS = (NA_WIDTH, NA_WIDTH, NA_WIDTH, SW_WIDTH, SW_KV_WIDTH, SW_KV_WIDTH)


def prep_weights(g_mix, g_mem, w_in, w_mem_kv, na_rpb, t5_table, sw_sink, w_na_o, w_sw_o, w_mx_o, w_out,
                 g_ffn, w_router):
    d = w_in.shape[0]
    mx_width = w_mx_o.shape[0]
    widths = QKV_WIDTHS + (mx_width,)
    offs = np.concatenate([[0], np.cumsum(widths)])
    cols = [w_in[:, offs[i]:offs[i + 1]] for i in range(len(widths))]
    perm = sw_head_perm()
    cols[3] = cols[3][:, perm]
    w_gate = w_in[:, offs[-1]:].reshape(d, N_BRANCHES, d).transpose(1, 0, 2)
    wr = w_router.T.astype(F32)
    wr_hi = wr.astype(BF16)
    return dict(
        g_mix=g_mix.reshape(1, d).astype(F32), g_mem=g_mem.reshape(1, d).astype(F32),
        g_ffn=g_ffn.reshape(1, d).astype(F32),
        w_qkv=jnp.concatenate(cols, axis=1).astype(BF16), qkv_widths=widths,
        qkv_scales=(NA_HEAD_DIM ** -0.5, 1.0, 1.0, SW_HEAD_DIM ** -0.5, 1.0, 1.0, 1.0),
        w_mem_kv=w_mem_kv.astype(BF16),
        na_bias=na_bias_table(na_rpb), sw_bias=sw_bias_table(t5_table), sw_sink=sw_sink.astype(F32),
        w_gate=w_gate.astype(BF16), w_na_o=w_na_o.astype(BF16), w_sw_o=w_sw_o[perm].astype(BF16),
        w_mx_o=w_mx_o.astype(BF16), w_out=w_out.astype(BF16),
        wr_hi=wr_hi, wr_lo=(wr - wr_hi.astype(F32)).astype(BF16))


def layer_front(x, mem, w, debug=False):
    b, t, d = x.shape
    m = mem.shape[1]
    xf = x.reshape(b * t, d)
    na_q, na_k, na_v, sw_q, sw_k, sw_v, mx_q = norm_proj(xf, w["g_mix"], w["w_qkv"], w["qkv_widths"],
                                                         w["qkv_scales"], tm=512)
    (mkv,) = norm_proj(mem.reshape(b * m, d), w["g_mem"], w["w_mem_kv"], (w["w_mem_kv"].shape[1],), (1.0,),
                       tm=512)
    r3 = lambda a: a.reshape(b, t, a.shape[1])
    o_na = na_attention(r3(na_q), r3(na_k), r3(na_v), w["na_bias"])
    o_sw = sw_attention(r3(sw_q), r3(sw_k), r3(sw_v), w["sw_bias"], w["sw_sink"])
    o_mx = mx_attention(r3(mx_q), mkv.reshape(b, m, mkv.shape[1]))
    f2 = lambda a: a.reshape(b * t, a.shape[2])
    x2, h2, aff = merge(xf, f2(o_na), f2(o_sw), f2(o_mx), w["g_mix"], w["w_gate"], w["w_na_o"], w["w_sw_o"],
                        w["w_mx_o"], w["w_out"], w["g_ffn"], w["wr_hi"], w["wr_lo"])
    if debug:
        return o_na, o_sw, o_mx, x2, h2, aff
    return x2, h2, aff


def _select_kernel(aff_ref, sel_ref, pos_ref, off_ref, *, cap):
    nc, ne, _ = aff_ref.shape
    bits = lax.bitcast_convert_type(aff_ref[...], I32)
    tok = (lax.broadcasted_iota(I32, bits.shape, 0) * LANES + lax.broadcasted_iota(I32, bits.shape, 2))

    def count(flags):
        return jnp.sum(flags, axis=(0, 2), keepdims=True)

    def value_step(i, prefix):
        cand = prefix | lax.shift_left(jnp.int32(1), 30 - i)
        cnt = count(jnp.where(bits >= cand, 1.0, 0.0))
        return jnp.where(cnt >= cap, cand, prefix)

    tau = lax.fori_loop(0, 31, value_step, jnp.zeros((1, ne, 1), I32))
    gt = bits > tau
    eq = bits == tau
    need = cap - count(jnp.where(gt, 1.0, 0.0))

    def index_step(i, last):
        cand = last | lax.shift_left(jnp.int32(1), 15 - i)
        cnt = count(jnp.where(eq, jnp.where(tok < cand, 1.0, 0.0), 0.0))
        return jnp.where(cnt < need, cand, last)

    last = lax.fori_loop(0, 16, index_step, jnp.zeros((1, ne, 1), I32))
    sel_ref[...] = jnp.where(gt, 1.0, jnp.where(eq, jnp.where(tok <= last, 1.0, 0.0), 0.0))

    upper = (lax.broadcasted_iota(I32, (LANES, LANES), 0) <= lax.broadcasted_iota(I32, (LANES, LANES), 1))
    upper = jnp.where(upper, 1.0, 0.0).astype(BF16)

    def chunk_step(c, off):
        s = sel_ref[c]
        cum = jnp.dot(s.astype(BF16), upper, preferred_element_type=F32)
        pos_ref[c] = off + cum - s
        off_ref[c] = jnp.broadcast_to(off, s.shape)
        return off + cum[:, LANES - 1:LANES]

    lax.fori_loop(0, nc, chunk_step, jnp.zeros((ne, 1), F32))


def select(aff, cap):
    nc, ne, _ = aff.shape
    assert nc * LANES <= 65536
    shape = jax.ShapeDtypeStruct(aff.shape, F32)
    return pl.pallas_call(
        functools.partial(_select_kernel, cap=float(cap)),
        out_shape=[shape, shape, shape],
        compiler_params=pltpu.CompilerParams(vmem_limit_bytes=VMEM_LIMIT),
        name="select",
    )(aff)


TOK_COL, CHUNK_COL, HI_COL = 0, 1, 2


def _compact_kernel(choff_ref, sel_ref, pos_ref, aff_ref, o_ref, tv_scr):
    e = pl.program_id(0)
    nc, ne, _ = aff_ref.shape
    win = 2 * LANES
    col = lax.broadcasted_iota(I32, (LANES, LANES), 1)
    row = lax.broadcasted_iota(I32, (LANES, LANES), 0)

    @pl.when(e == 0)
    def _():
        ecol = lax.broadcasted_iota(I32, (ne, LANES), 1) - lax.broadcasted_iota(I32, (ne, LANES), 0)
        place = [jnp.where(ecol == HI_COL + k * ne, 1.0, 0.0).astype(BF16) for k in range(3)]

        def build(c, carry):
            a = aff_ref[c]
            hi = a.astype(BF16)
            r1 = a - hi.astype(F32)
            mid = r1.astype(BF16)
            lo = (r1 - mid.astype(F32)).astype(BF16)
            rec = (lax.dot_general(hi, place[0], TN_DIMS, preferred_element_type=F32)
                   + lax.dot_general(mid, place[1], TN_DIMS, preferred_element_type=F32)
                   + lax.dot_general(lo, place[2], TN_DIMS, preferred_element_type=F32))
            rec = rec + jnp.where(col == TOK_COL, row, jnp.where(col == CHUNK_COL, c, 0)).astype(F32)
            tv_scr[c] = rec.astype(BF16)
            return carry

        lax.fori_loop(0, nc, build, 0)

    o_ref[...] = jnp.zeros_like(o_ref)
    slot0 = lax.broadcasted_iota(I32, (win, LANES), 0).astype(F32)

    def body(c, carry):
        off = choff_ref[c * ne + e]
        base = pl.multiple_of((off // LANES) * LANES, LANES)
        s = sel_ref[c, pl.ds(e, 1), :]
        p = pos_ref[c, pl.ds(e, 1), :] - base.astype(F32)
        onehot = jnp.where(s > 0.0, jnp.where(slot0 == p, 1.0, 0.0), 0.0).astype(BF16)
        o_ref[0, pl.ds(base, win), :] += jnp.dot(onehot, tv_scr[c], preferred_element_type=F32)
        return carry

    lax.fori_loop(0, nc, body, 0)


def compact(sel, pos, choff, aff, cap):
    nc, ne, _ = aff.shape
    assert nc <= 256 and HI_COL + 3 * ne <= LANES and cap % LANES == 0
    rows = cap + 2 * LANES
    full = pl.BlockSpec(aff.shape, lambda e, s: (0, 0, 0))
    rec = pl.pallas_call(
        _compact_kernel,
        grid_spec=pltpu.PrefetchScalarGridSpec(
            num_scalar_prefetch=1, grid=(ne,),
            in_specs=[full, full, full],
            out_specs=pl.BlockSpec((1, rows, LANES), lambda e, s: (e, 0, 0)),
            scratch_shapes=[pltpu.VMEM((nc, LANES, LANES), BF16)]),
        out_shape=jax.ShapeDtypeStruct((ne, rows, LANES), F32),
        compiler_params=_params("arbitrary"),
        name="compact",
    )(choff, sel, pos, aff)
    rec = rec[:, :cap]
    idx = (rec[:, :, CHUNK_COL] * LANES + rec[:, :, TOK_COL]).astype(I32)
    digits = rec[:, :, HI_COL:HI_COL + 3 * ne].reshape(ne, cap, 3, ne)
    own = jnp.eye(ne, dtype=F32)[:, None, None, :]
    weight = jnp.sum(jnp.sum(digits * own, axis=-1), axis=-1)
    return idx, weight


def _expert_kernel(idx_ref, wt_ref, h2_ref, x2_ref, wg_ref, wu_ref, wd_ref, y_ref,
                   xs32, xs, yb, acc, sems, *, ts):
    del x2_ref
    f = pl.program_id(2)

    def h_copy(r):
        return pltpu.make_async_copy(h2_ref.at[pl.ds(idx_ref[0, 0, r], 1)], xs32.at[pl.ds(r, 1)], sems.at[0])

    def y_in_copy(r):
        return pltpu.make_async_copy(y_ref.at[pl.ds(idx_ref[0, 0, r], 1)], yb.at[pl.ds(r, 1)], sems.at[1])

    def y_out_copy(r):
        return pltpu.make_async_copy(yb.at[pl.ds(r, 1)], y_ref.at[pl.ds(idx_ref[0, 0, r], 1)], sems.at[2])

    def for_rows(fn):
        def body(r, carry):
            fn(r)
            return carry
        lax.fori_loop(0, ts, body, 0)

    @pl.when(f == 0)
    def _():
        def start(r):
            h_copy(r).start()
            y_in_copy(r).start()
        for_rows(start)

        def wait(r):
            h_copy(r).wait()
            y_in_copy(r).wait()
        for_rows(wait)
        xs[...] = xs32[...].astype(BF16)

    x = xs[...]
    a = jnp.dot(x, wg_ref[0], preferred_element_type=F32)
    b = jnp.dot(x, wu_ref[0], preferred_element_type=F32)
    hm = (jax.nn.silu(a) * b).astype(BF16)
    part = jnp.dot(hm, wd_ref[0], preferred_element_type=F32)

    @pl.when(f == 0)
    def _():
        acc[...] = part

    @pl.when(f > 0)
    def _():
        acc[...] += part

    @pl.when(f == pl.num_programs(2) - 1)
    def _():
        yb[...] += acc[...] * wt_ref[0]
        for_rows(lambda r: y_out_copy(r).start())
        for_rows(lambda r: y_out_copy(r).wait())


def expert_ffn(idx, weight, h2, x2, w_gate, w_up, w_down, ts=512, tf=512):
    ne, cap = idx.shape
    n, d = h2.shape
    ff = w_gate.shape[2]
    ts = min(ts, cap)
    tf = min(tf, ff)
    s_tiles = cap // ts
    idx3 = idx.reshape(ne * s_tiles, 1, ts)
    wt3 = weight.reshape(ne * s_tiles, ts, 1)
    tile = lambda e, s, f: (e * s_tiles + s, 0, 0)
    return pl.pallas_call(
        functools.partial(_expert_kernel, ts=ts),
        grid=(ne, s_tiles, ff // tf),
        in_specs=[pl.BlockSpec((1, 1, ts), tile, memory_space=pltpu.SMEM),
                  pl.BlockSpec((1, ts, 1), tile),
                  pl.BlockSpec(memory_space=pl.ANY),
                  pl.BlockSpec(memory_space=pl.ANY),
                  pl.BlockSpec((1, d, tf), lambda e, s, f: (e, 0, f)),
                  pl.BlockSpec((1, d, tf), lambda e, s, f: (e, 0, f)),
                  pl.BlockSpec((1, tf, d), lambda e, s, f: (e, f, 0))],
        out_specs=pl.BlockSpec(memory_space=pl.ANY),
        out_shape=jax.ShapeDtypeStruct((n, d), F32),
        scratch_shapes=[pltpu.VMEM((ts, d), F32), pltpu.VMEM((ts, d), BF16), pltpu.VMEM((ts, d), F32),
                        pltpu.VMEM((ts, d), F32), pltpu.SemaphoreType.DMA((3,))],
        input_output_aliases={3: 0},
        compiler_params=_params("arbitrary", "arbitrary", "arbitrary"),
        name="expert_ffn",
    )(idx3, wt3, h2, x2, w_gate, w_up, w_down)


def _final_norm_kernel(x_ref, g_ref, o_ref):
    o_ref[...] = _rms(x_ref[...], g_ref[...])


def final_norm(x, g, tm=1024):
    n, d = x.shape
    tm = min(tm, n)
    return pl.pallas_call(
        _final_norm_kernel,
        grid=(n // tm,),
        in_specs=[pl.BlockSpec((tm, d), lambda i: (i, 0)), pl.BlockSpec((1, d), lambda i: (0, 0))],
        out_specs=pl.BlockSpec((tm, d), lambda i: (i, 0)),
        out_shape=jax.ShapeDtypeStruct((n, d), F32),
        compiler_params=_params("parallel"),
        name="final_norm",
    )(x, g)


def moe_and_norm(x2, h2, aff, w_e_gate, w_e_up, w_e_down, g_final):
    n, d = x2.shape
    cap = EC_CAPACITY * n // N_EXPERTS
    sel, pos, off = select(aff, cap)
    choff = off[:, :, 0].astype(I32).reshape(-1)
    idx, weight = compact(sel, pos, choff, aff, cap)
    y = expert_ffn(idx, weight, h2, x2, w_e_gate, w_e_up, w_e_down)
    return final_norm(y, g_final)


def encoder_group(x, mem, w, experts, g_final):
    b, t, d = x.shape
    x2, h2, aff = layer_front(x, mem, w)
    return moe_and_norm(x2, h2, aff, *experts, g_final).reshape(b, t, d)


def kernel(x_prompt, x_sample, mem_prompt, mem_sample, g_mix, g_mem, w_in, w_mem_kv, na_rpb, t5_table, sw_sink,
           w_na_o, w_sw_o, w_mx_o, w_out, g_ffn, w_router, w_e_gate, w_e_up, w_e_down, g_final):
    assert g_mix.shape[0] == 1, "single-layer trunk"
    w = prep_weights(g_mix[0], g_mem[0], w_in[0], w_mem_kv[0], na_rpb[0], t5_table, sw_sink[0], w_na_o[0],
                     w_sw_o[0], w_mx_o[0], w_out[0], g_ffn[0], w_router[0])
    experts = (w_e_gate[0].astype(BF16), w_e_up[0].astype(BF16), w_e_down[0].astype(BF16))
    gf = g_final.reshape(1, -1).astype(F32)
    y_prompt = encoder_group(x_prompt, mem_prompt, w, experts, gf)
    y_sample = encoder_group(x_sample, mem_sample, w, experts, gf)
    return (y_prompt, y_sample)
```

```python
import functools
import math

import numpy as np
import jax
import jax.numpy as jnp
from jax import lax
from jax.experimental import pallas as pl
from jax.experimental.pallas import tpu as pltpu

F32 = jnp.float32
BF16 = jnp.bfloat16
I32 = jnp.int32

RMS_EPS = 1e-6
NEG_INF = -1e30

GRID_W = 64
NA_HEADS = 8
NA_HEAD_DIM = 64
NA_KR = 8
NA_KC = 16
SW_HEADS = 16
SW_KV_HEADS = 4
SW_HEAD_DIM = 64
SW_WINDOW = 128
SW_BLOCK = 128
MX_HEADS = 4
T5_BUCKETS = 32
T5_MAX_DIST = 128
N_BRANCHES = 3
N_EXPERTS = 16
EC_CAPACITY = 2

LANES = 128
SUBLANES = 8
V7X_VMEM_BYTES = 64 * 1024 * 1024
VMEM_LIMIT = V7X_VMEM_BYTES * 7 // 8

CHUNK_UNROLL = 8
EXPERT_TS = 512

NT_DIMS = (((1,), (1,)), ((), ()))
TN_DIMS = (((0,), (0,)), ((), ()))


def _params(*sem):
    return pltpu.CompilerParams(dimension_semantics=sem, vmem_limit_bytes=VMEM_LIMIT)


def _rms(x, g):
    return x * lax.rsqrt(jnp.mean(x * x, axis=-1, keepdims=True) + RMS_EPS) * g


def _to_row_tiles(strip):
    return strip.reshape(strip.shape[0] // SUBLANES, SUBLANES, LANES)


def _from_row_tiles(tiles):
    return tiles.reshape(tiles.shape[0] * SUBLANES, LANES)


def _row_tile_shape(rows, d):
    return (rows // SUBLANES, d // LANES, SUBLANES, LANES)


def _norm_proj_kernel(x_ref, g_ref, w_ref, *o_refs, scales):
    h = _rms(x_ref[...], g_ref[...]).astype(BF16)
    off = 0
    for o_ref, sc in zip(o_refs, scales):
        width = o_ref.shape[1]
        for c0 in range(0, width, 512):
            cw = min(512, width - c0)
            r = jnp.dot(h, w_ref[:, off + c0:off + c0 + cw], preferred_element_type=F32)
            if sc != 1.0:
                r = r * sc
            o_ref[:, c0:c0 + cw] = r.astype(o_ref.dtype)
        off += width


def norm_proj(x, g, w, widths, scales, tm):
    n, d = x.shape
    tm = min(tm, n)
    assert n % tm == 0
    return pl.pallas_call(
        functools.partial(_norm_proj_kernel, scales=tuple(scales)),
        grid=(n // tm,),
        in_specs=[pl.BlockSpec((tm, d), lambda i: (i, 0)),
                  pl.BlockSpec((1, d), lambda i: (0, 0)),
                  pl.BlockSpec(w.shape, lambda i: (0, 0))],
        out_specs=[pl.BlockSpec((tm, c), lambda i: (i, 0)) for c in widths],
        out_shape=[jax.ShapeDtypeStruct((n, c), BF16) for c in widths],
        compiler_params=_params("parallel"),
        name="norm_proj",
    )(x, g, w)


def _softmax_pv(s, v, extra_logit=None):
    m = jnp.max(s, axis=-1, keepdims=True)
    if extra_logit is not None:
        m = jnp.maximum(m, extra_logit)
    e = jnp.exp(s - m)
    den = jnp.sum(e, axis=-1, keepdims=True)
    if extra_logit is not None:
        den = den + jnp.exp(extra_logit - m)
    o = jnp.dot(e.astype(BF16), v, preferred_element_type=F32)
    return o / den


def _na_kernel(q_ref, k_ref, v_ref, bias_ref, o_ref, *, rows, rb):
    j = pl.program_id(1)
    lo = lax.broadcasted_iota(I32, (GRID_W, LANES), 1) < NA_HEAD_DIM
    nkeys = NA_KR * GRID_W

    def body(i, carry):
        r = j * rb + i
        rs = jnp.clip(r - NA_KR // 2, 0, rows - NA_KR)
        off = r - rs
        q = q_ref[0, pl.ds(pl.multiple_of(i * GRID_W, GRID_W), GRID_W), :]
        kk = k_ref[0, pl.ds(pl.multiple_of(rs * GRID_W, GRID_W), nkeys), :]
        vv = v_ref[0, pl.ds(pl.multiple_of(rs * GRID_W, GRID_W), nkeys), :]
        outs = []
        for p in range(NA_HEADS // 2):
            qp = q[:, p * LANES:(p + 1) * LANES]
            kp = kk[:, p * LANES:(p + 1) * LANES]
            vp = vv[:, p * LANES:(p + 1) * LANES]
            halves = []
            for hh in range(2):
                qm = jnp.where(lo if hh == 0 else jnp.logical_not(lo), qp, jnp.zeros_like(qp))
                s = lax.dot_general(qm, kp, NT_DIMS, preferred_element_type=F32)
                s = s + bias_ref[off, 2 * p + hh]
                halves.append(_softmax_pv(s, vp))
            outs.append(jnp.where(lo, halves[0], halves[1]))
        o_ref[0, pl.ds(pl.multiple_of(i * GRID_W, GRID_W), GRID_W), :] = (
            jnp.concatenate(outs, axis=1).astype(o_ref.dtype))
        return carry

    lax.fori_loop(0, rb, body, 0)


def na_attention(q, k, v, bias, rb=8):
    b, t, c = q.shape
    rows = t // GRID_W
    assert rows >= NA_KR and rows % rb == 0
    return pl.pallas_call(
        functools.partial(_na_kernel, rows=rows, rb=rb),
        grid=(b, rows // rb),
        in_specs=[pl.BlockSpec((1, rb * GRID_W, c), lambda i, j: (i, j, 0)),
                  pl.BlockSpec((1, t, c), lambda i, j: (i, 0, 0)),
                  pl.BlockSpec((1, t, c), lambda i, j: (i, 0, 0)),
                  pl.BlockSpec(bias.shape, lambda i, j: (0, 0, 0, 0))],
        out_specs=pl.BlockSpec((1, rb * GRID_W, c), lambda i, j: (i, j, 0)),
        out_shape=jax.ShapeDtypeStruct((b, t, c), BF16),
        compiler_params=_params("parallel", "arbitrary"),
        name="na_attention",
    )(q, k, v, bias)


def na_bias_table(rpb):
    col = np.arange(GRID_W)
    col_start = np.clip(col - NA_KC // 2, 0, GRID_W - NA_KC)
    in_win = (col[None, :] >= col_start[:, None]) & (col[None, :] < col_start[:, None] + NA_KC)
    dc = np.clip(col[None, :] - col[:, None] + NA_KC - 1, 0, 2 * NA_KC - 2)
    heads = rpb.shape[0]
    by_col = jnp.take(rpb.astype(F32), jnp.asarray(dc.reshape(-1)), axis=2)
    by_col = by_col.reshape(heads, 2 * NA_KR - 1, GRID_W, GRID_W)
    by_col = jnp.where(jnp.asarray(in_win)[None, None], by_col, NEG_INF)
    per_off = [by_col[:, NA_KR - 1 - off:2 * NA_KR - 1 - off].transpose(0, 2, 1, 3) for off in range(NA_KR)]
    return jnp.stack(per_off).reshape(NA_KR, heads, GRID_W, NA_KR * GRID_W)


def _sw_kernel(sink_ref, q_ref, k_ref, v_ref, bias_ref, o_ref, *, nb):
    n = pl.program_id(1)
    blk = SW_BLOCK
    group = SW_HEADS // SW_KV_HEADS

    def rows_of(ref, c):
        return ref[0, pl.ds(pl.multiple_of(c * blk, blk), blk), :]

    cl = jnp.maximum(n - 1, 0)
    cr = jnp.minimum(n + 1, nb - 1)
    k3 = jnp.concatenate([rows_of(k_ref, cl), rows_of(k_ref, n), rows_of(k_ref, cr)], axis=0)
    v3 = jnp.concatenate([rows_of(v_ref, cl), rows_of(v_ref, n), rows_of(v_ref, cr)], axis=0)
    pen_l = jnp.where(n > 0, 0.0, NEG_INF).astype(F32)
    pen_r = jnp.where(n < nb - 1, 0.0, NEG_INF).astype(F32)
    key = lax.broadcasted_iota(I32, (1, 3 * blk), 1)
    pen = jnp.where(key < blk, pen_l, jnp.where(key >= 2 * blk, pen_r, 0.0))
    lo = lax.broadcasted_iota(I32, (blk, LANES), 1) < SW_HEAD_DIM

    for pair in range(SW_KV_HEADS // 2):
        kp = k3[:, pair * LANES:(pair + 1) * LANES]
        vp = v3[:, pair * LANES:(pair + 1) * LANES]
        per_half = []
        for half in range(2):
            c = 2 * pair + half
            keep = lo if half == 0 else jnp.logical_not(lo)
            qs = []
            for g in range(group):
                t = pair * group + g
                qt = q_ref[0, :, t * LANES:(t + 1) * LANES]
                qs.append(jnp.where(keep, qt, jnp.zeros_like(qt)))
            qq = jnp.concatenate(qs, axis=0)
            s = lax.dot_general(qq, kp, NT_DIMS, preferred_element_type=F32)
            s = s + bias_ref[c * group:(c + 1) * group].reshape(group * blk, 3 * blk) + pen
            sink = jnp.concatenate(
                [jnp.full((blk, 1), sink_ref[c * group + g], F32) for g in range(group)], axis=0)
            per_half.append(_softmax_pv(s, vp, extra_logit=sink))
        for g in range(group):
            t = pair * group + g
            o = jnp.where(lo, per_half[0][g * blk:(g + 1) * blk], per_half[1][g * blk:(g + 1) * blk])
            o_ref[0, :, t * LANES:(t + 1) * LANES] = o.astype(o_ref.dtype)


def sw_attention(q, k, v, bias, sink):
    b, t, c = q.shape
    nb = t // SW_BLOCK
    kvw = k.shape[2]
    grid_spec = pltpu.PrefetchScalarGridSpec(
        num_scalar_prefetch=1,
        grid=(b, nb),
        in_specs=[pl.BlockSpec((1, SW_BLOCK, c), lambda i, j, s: (i, j, 0)),
                  pl.BlockSpec((1, t, kvw), lambda i, j, s: (i, 0, 0)),
                  pl.BlockSpec((1, t, kvw), lambda i, j, s: (i, 0, 0)),
                  pl.BlockSpec(bias.shape, lambda i, j, s: (0, 0, 0))],
        out_specs=pl.BlockSpec((1, SW_BLOCK, c), lambda i, j, s: (i, j, 0)),
    )
    return pl.pallas_call(
        functools.partial(_sw_kernel, nb=nb),
        grid_spec=grid_spec,
        out_shape=jax.ShapeDtypeStruct((b, t, c), BF16),
        compiler_params=_params("parallel", "arbitrary"),
        name="sw_attention",
    )(sink, q, k, v, bias)


def _t5_bucket(rel):
    half = T5_BUCKETS // 2
    max_exact = half // 2
    n = jnp.abs(rel)
    nf = jnp.maximum(n, 1).astype(F32)
    large = max_exact + (jnp.log(nf / max_exact) / math.log(T5_MAX_DIST / max_exact)
                         * (half - max_exact)).astype(jnp.int32)
    large = jnp.minimum(large, half - 1)
    return jnp.where(rel > 0, half, 0) + jnp.where(n < max_exact, n, large)


def sw_bias_table(t5_table):
    span = SW_BLOCK + 2 * SW_WINDOW
    rel = np.arange(span)[None, :] - SW_WINDOW - np.arange(SW_BLOCK)[:, None]
    bucket = _t5_bucket(jnp.asarray(rel, dtype=jnp.int32))[None]
    table = t5_table.astype(F32).T[:, :, None, None]
    bias = jnp.zeros((t5_table.shape[1], SW_BLOCK, span), F32)
    for b in range(T5_BUCKETS):
        bias = jnp.where(bucket == b, table[:, b], bias)
    return jnp.where(jnp.asarray(np.abs(rel) <= SW_WINDOW)[None], bias, NEG_INF)


def sw_pair_heads(w, axis):
    group = SW_HEADS // SW_KV_HEADS
    shape = w.shape
    split = shape[:axis] + (SW_KV_HEADS // 2, 2, group, SW_HEAD_DIM) + shape[axis + 1:]
    order = list(range(len(split)))
    order[axis + 1], order[axis + 2] = axis + 2, axis + 1
    return w.reshape(split).transpose(order).reshape(shape)


def _mx_kernel(q_ref, mk_ref, mv_ref, o_ref, *, scale):
    hd = q_ref.shape[2] // MX_HEADS
    for h in range(MX_HEADS):
        sl = slice(h * hd, (h + 1) * hd)
        s = lax.dot_general(q_ref[0, :, sl], mk_ref[0, :, sl], NT_DIMS, preferred_element_type=F32) * scale
        o_ref[0, :, sl] = _softmax_pv(s, mv_ref[0, :, sl]).astype(o_ref.dtype)


def mx_attention(q, mkv, tq=512):
    b, t, c = q.shape
    m = mkv.shape[1]
    tq = min(tq, t)
    return pl.pallas_call(
        functools.partial(_mx_kernel, scale=float((c // MX_HEADS) ** -0.5)),
        grid=(b, t // tq),
        in_specs=[pl.BlockSpec((1, tq, c), lambda i, j: (i, j, 0)),
                  pl.BlockSpec((1, m, c), lambda i, j: (i, 0, 0)),
                  pl.BlockSpec((1, m, c), lambda i, j: (i, 0, 1))],
        out_specs=pl.BlockSpec((1, tq, c), lambda i, j: (i, j, 0)),
        out_shape=jax.ShapeDtypeStruct((b, t, c), BF16),
        compiler_params=_params("parallel", "arbitrary"),
        name="mx_attention",
    )(q, mkv, mkv)


def _merge_kernel(x_ref, ona_ref, osw_ref, omx_ref, gmix_ref, wg_ref, wna_ref, wsw_ref, wmx_ref, wout_ref,
                  gffn_ref, wrh_ref, wrl_ref, y2_ref, h2_ref, aff_ref, h_scr):
    j = pl.program_id(1)
    strips = range(h2_ref.shape[1])

    @pl.when(j == 0)
    def _():
        x = x_ref[...]
        h_scr[...] = _rms(x, gmix_ref[...]).astype(BF16)
        for c in strips:
            h2_ref[:, c] = _to_row_tiles(x[:, c * LANES:(c + 1) * LANES])

    h = h_scr[...]
    merged = None
    for o_ref, w_ref, b in ((ona_ref, wna_ref, 0), (osw_ref, wsw_ref, 1), (omx_ref, wmx_ref, 2)):
        gate = jax.nn.sigmoid(jnp.dot(h, wg_ref[b], preferred_element_type=F32))
        term = gate * jnp.dot(o_ref[...], w_ref[...], preferred_element_type=F32)
        merged = term if merged is None else merged + term
    part = jnp.dot(merged.astype(BF16), wout_ref[...], preferred_element_type=F32)
    for c in strips:
        h2_ref[:, c] += _to_row_tiles(part[:, c * LANES:(c + 1) * LANES])

    @pl.when(j == pl.num_programs(1) - 1)
    def _():
        for c in strips:
            y2_ref[0, :, c] = h2_ref[:, c]
        y2_ref[1] = jnp.zeros(y2_ref.shape[1:], F32)
        x2 = jnp.concatenate([_from_row_tiles(h2_ref[:, c]) for c in strips], axis=1)
        h2 = _rms(x2, gffn_ref[...])
        for c in strips:
            h2_ref[:, c] = _to_row_tiles(h2[:, c * LANES:(c + 1) * LANES])
        hi = h2.astype(BF16)
        lo = (h2 - hi.astype(F32)).astype(BF16)
        wh = wrh_ref[...]
        logits = (lax.dot_general(wh, hi, NT_DIMS, preferred_element_type=F32)
                  + lax.dot_general(wh, lo, NT_DIMS, preferred_element_type=F32)
                  + lax.dot_general(wrl_ref[...], hi, NT_DIMS, preferred_element_type=F32))
        m = jnp.max(logits, axis=0, keepdims=True)
        e = jnp.exp(logits - m)
        aff = e / jnp.sum(e, axis=0, keepdims=True)
        for c in range(aff_ref.shape[0]):
            aff_ref[c] = aff[:, c * LANES:(c + 1) * LANES]


def merge(x, o_na, o_sw, o_mx, g_mix, w_gate, w_na_o, w_sw_o, w_mx_o, w_out, g_ffn, wr_hi, wr_lo,
          tm=512, tn=256):
    n, d = x.shape
    tm = min(tm, n)
    assert n % tm == 0 and tm % LANES == 0 and d % LANES == 0
    ne = wr_hi.shape[0]
    row = lambda i, j: (i, 0)
    return pl.pallas_call(
        _merge_kernel,
        grid=(n // tm, d // tn),
        in_specs=[pl.BlockSpec((tm, d), row),
                  pl.BlockSpec((tm, o_na.shape[1]), row),
                  pl.BlockSpec((tm, o_sw.shape[1]), row),
                  pl.BlockSpec((tm, o_mx.shape[1]), row),
                  pl.BlockSpec((1, d), lambda i, j: (0, 0)),
                  pl.BlockSpec((N_BRANCHES, d, tn), lambda i, j: (0, 0, j)),
                  pl.BlockSpec((w_na_o.shape[0], tn), lambda i, j: (0, j)),
                  pl.BlockSpec((w_sw_o.shape[0], tn), lambda i, j: (0, j)),
                  pl.BlockSpec((w_mx_o.shape[0], tn), lambda i, j: (0, j)),
                  pl.BlockSpec((tn, d), lambda i, j: (j, 0)),
                  pl.BlockSpec((1, d), lambda i, j: (0, 0)),
                  pl.BlockSpec((ne, d), lambda i, j: (0, 0)),
                  pl.BlockSpec((ne, d), lambda i, j: (0, 0))],
        out_specs=[pl.BlockSpec((2,) + _row_tile_shape(tm, d), lambda i, j: (0, i, 0, 0, 0)),
                   pl.BlockSpec(_row_tile_shape(tm, d), lambda i, j: (i, 0, 0, 0)),
                   pl.BlockSpec((tm // LANES, ne, LANES), lambda i, j: (i, 0, 0))],
        out_shape=[jax.ShapeDtypeStruct((2,) + _row_tile_shape(n, d), F32),
                   jax.ShapeDtypeStruct(_row_tile_shape(n, d), F32),
                   jax.ShapeDtypeStruct((n // LANES, ne, LANES), F32)],
        scratch_shapes=[pltpu.VMEM((tm, d), BF16)],
        compiler_params=_params("parallel", "arbitrary"),
        name="merge",
    )(x, o_na, o_sw, o_mx, g_mix, w_gate, w_na_o, w_sw_o, w_mx_o, w_out, g_ffn, wr_hi, wr_lo)


NA_WIDTH = NA_HEADS * NA_HEAD_DIM
SW_WIDTH = SW_HEADS * SW_HEAD_DIM
SW_KV_WIDTH = SW_KV_HEADS * SW_HEAD_DIM
QKV_WIDTHS = (NA_WIDTH, NA_WIDTH, NA_WIDTH, SW_WIDTH, SW_KV_WIDTH, SW_KV_WIDTH)


def prep_weights(g_mix, g_mem, w_in, w_mem_kv, na_rpb, t5_table, sw_sink, w_na_o, w_sw_o, w_mx_o, w_out,
                 g_ffn, w_router):
    d = w_in.shape[0]
    mx_width = w_mx_o.shape[0]
    widths = QKV_WIDTHS + (mx_width,)
    offs = np.concatenate([[0], np.cumsum(widths)])
    cols = [w_in[:, offs[i]:offs[i + 1]] for i in range(len(widths))]
    cols[3] = sw_pair_heads(cols[3], axis=1)
    w_gate = w_in[:, offs[-1]:].reshape(d, N_BRANCHES, d).transpose(1, 0, 2)
    wr = w_router.T.astype(F32)
    wr_hi = wr.astype(BF16)
    return dict(
        g_mix=g_mix.reshape(1, d).astype(F32), g_mem=g_mem.reshape(1, d).astype(F32),
        g_ffn=g_ffn.reshape(1, d).astype(F32),
        w_qkv=jnp.concatenate(cols, axis=1).astype(BF16), qkv_widths=widths,
        qkv_scales=(NA_HEAD_DIM ** -0.5, 1.0, 1.0, SW_HEAD_DIM ** -0.5, 1.0, 1.0, 1.0),
        w_mem_kv=w_mem_kv.astype(BF16),
        na_bias=na_bias_table(na_rpb), sw_bias=sw_bias_table(t5_table), sw_sink=sw_sink.astype(F32),
        w_gate=w_gate.astype(BF16), w_na_o=w_na_o.astype(BF16), w_sw_o=sw_pair_heads(w_sw_o, axis=0).astype(BF16),
        w_mx_o=w_mx_o.astype(BF16), w_out=w_out.astype(BF16),
        wr_hi=wr_hi, wr_lo=(wr - wr_hi.astype(F32)).astype(BF16))


def layer_front(x, mem, w, debug=False):
    b, t, d = x.shape
    m = mem.shape[1]
    xf = x.reshape(b * t, d)
    na_q, na_k, na_v, sw_q, sw_k, sw_v, mx_q = norm_proj(xf, w["g_mix"], w["w_qkv"], w["qkv_widths"],
                                                         w["qkv_scales"], tm=512)
    (mkv,) = norm_proj(mem.reshape(b * m, d), w["g_mem"], w["w_mem_kv"], (w["w_mem_kv"].shape[1],), (1.0,),
                       tm=512)
    r3 = lambda a: a.reshape(b, t, a.shape[1])
    o_na = na_attention(r3(na_q), r3(na_k), r3(na_v), w["na_bias"])
    o_sw = sw_attention(r3(sw_q), r3(sw_k), r3(sw_v), w["sw_bias"], w["sw_sink"])
    o_mx = mx_attention(r3(mx_q), mkv.reshape(b, m, mkv.shape[1]))
    f2 = lambda a: a.reshape(b * t, a.shape[2])
    x2, h2, aff = merge(xf, f2(o_na), f2(o_sw), f2(o_mx), w["g_mix"], w["w_gate"], w["w_na_o"], w["w_sw_o"],
                        w["w_mx_o"], w["w_out"], w["g_ffn"], w["wr_hi"], w["wr_lo"])
    if debug:
        return o_na, o_sw, o_mx, x2, h2, aff
    return x2, h2, aff


def _select_kernel(aff_ref, sel_ref, pos_ref, off_ref, *, cap):
    nc, ne, _ = aff_ref.shape
    bits = lax.bitcast_convert_type(aff_ref[...], I32)
    tok = (lax.broadcasted_iota(I32, bits.shape, 0) * LANES + lax.broadcasted_iota(I32, bits.shape, 2))

    def count(flags):
        return jnp.sum(flags, axis=(0, 2), keepdims=True)

    def value_step(i, prefix):
        cand = prefix | lax.shift_left(jnp.int32(1), 30 - i)
        cnt = count(jnp.where(bits >= cand, 1.0, 0.0))
        return jnp.where(cnt >= cap, cand, prefix)

    tau = lax.fori_loop(0, 31, value_step, jnp.zeros((1, ne, 1), I32))
    gt = bits > tau
    eq = bits == tau
    need = cap - count(jnp.where(gt, 1.0, 0.0))

    def index_step(i, last):
        cand = last | lax.shift_left(jnp.int32(1), 15 - i)
        cnt = count(jnp.where(eq, jnp.where(tok < cand, 1.0, 0.0), 0.0))
        return jnp.where(cnt < need, cand, last)

    last = lax.fori_loop(0, 16, index_step, jnp.zeros((1, ne, 1), I32))
    sel_ref[...] = jnp.where(gt, 1.0, jnp.where(eq, jnp.where(tok <= last, 1.0, 0.0), 0.0))

    upper = (lax.broadcasted_iota(I32, (LANES, LANES), 0) <= lax.broadcasted_iota(I32, (LANES, LANES), 1))
    upper = jnp.where(upper, 1.0, 0.0).astype(BF16)

    def chunk_step(c, off):
        s = sel_ref[c]
        cum = jnp.dot(s.astype(BF16), upper, preferred_element_type=F32)
        pos_ref[c] = off + cum - s
        off_ref[c] = jnp.broadcast_to(off, s.shape)
        return off + cum[:, LANES - 1:LANES]

    lax.fori_loop(0, nc, chunk_step, jnp.zeros((ne, 1), F32), unroll=CHUNK_UNROLL)


def select(aff, cap):
    nc, ne, _ = aff.shape
    assert nc * LANES <= 65536
    shape = jax.ShapeDtypeStruct(aff.shape, F32)
    return pl.pallas_call(
        functools.partial(_select_kernel, cap=float(cap)),
        out_shape=[shape, shape, shape],
        compiler_params=pltpu.CompilerParams(vmem_limit_bytes=VMEM_LIMIT),
        name="select",
    )(aff)


TOK_COL, CHUNK_COL, HI_COL = 0, 1, 2


def _compact_kernel(choff_ref, sel_ref, pos_ref, aff_ref, o_ref, tv_scr):
    e = pl.program_id(0)
    nc, ne, _ = aff_ref.shape
    win = 2 * LANES
    col = lax.broadcasted_iota(I32, (LANES, LANES), 1)
    row = lax.broadcasted_iota(I32, (LANES, LANES), 0)

    @pl.when(e == 0)
    def _():
        ecol = lax.broadcasted_iota(I32, (ne, LANES), 1) - lax.broadcasted_iota(I32, (ne, LANES), 0)
        place = [jnp.where(ecol == HI_COL + k * ne, 1.0, 0.0).astype(BF16) for k in range(3)]

        def build(c, carry):
            a = aff_ref[c]
            hi = a.astype(BF16)
            r1 = a - hi.astype(F32)
            mid = r1.astype(BF16)
            lo = (r1 - mid.astype(F32)).astype(BF16)
            rec = (lax.dot_general(hi, place[0], TN_DIMS, preferred_element_type=F32)
                   + lax.dot_general(mid, place[1], TN_DIMS, preferred_element_type=F32)
                   + lax.dot_general(lo, place[2], TN_DIMS, preferred_element_type=F32))
            rec = rec + jnp.where(col == TOK_COL, row, jnp.where(col == CHUNK_COL, c, 0)).astype(F32)
            tv_scr[c] = rec.astype(BF16)
            return carry

        lax.fori_loop(0, nc, build, 0, unroll=CHUNK_UNROLL)

    o_ref[...] = jnp.zeros_like(o_ref)
    slot0 = lax.broadcasted_iota(I32, (win, LANES), 0).astype(F32)

    def body(c, carry):
        off = choff_ref[c * ne + e]
        base = pl.multiple_of((off // LANES) * LANES, LANES)
        s = sel_ref[c, pl.ds(e, 1), :]
        p = pos_ref[c, pl.ds(e, 1), :] - base.astype(F32)
        onehot = jnp.where(s > 0.0, jnp.where(slot0 == p, 1.0, 0.0), 0.0).astype(BF16)
        o_ref[0, pl.ds(base, win), :] += jnp.dot(onehot, tv_scr[c], preferred_element_type=F32)
        return carry

    lax.fori_loop(0, nc, body, 0, unroll=CHUNK_UNROLL)


def compact(sel, pos, choff, aff, cap):
    nc, ne, _ = aff.shape
    assert nc <= 256 and HI_COL + 3 * ne <= LANES and cap % LANES == 0
    rows = cap + 2 * LANES
    full = pl.BlockSpec(aff.shape, lambda e, s: (0, 0, 0))
    rec = pl.pallas_call(
        _compact_kernel,
        grid_spec=pltpu.PrefetchScalarGridSpec(
            num_scalar_prefetch=1, grid=(ne,),
            in_specs=[full, full, full],
            out_specs=pl.BlockSpec((1, rows, LANES), lambda e, s: (e, 0, 0)),
            scratch_shapes=[pltpu.VMEM((nc, LANES, LANES), BF16)]),
        out_shape=jax.ShapeDtypeStruct((ne, rows, LANES), F32),
        compiler_params=_params("arbitrary"),
        name="compact",
    )(choff, sel, pos, aff)
    rec = rec[:, :cap]
    idx = (rec[:, :, CHUNK_COL] * LANES + rec[:, :, TOK_COL]).astype(I32)
    digits = rec[:, :, HI_COL:HI_COL + 3 * ne].reshape(ne, cap, 3, ne)
    own = jnp.eye(ne, dtype=F32)[:, None, None, :]
    weight = jnp.sum(jnp.sum(digits * own, axis=-1), axis=-1)
    return idx, weight


def _expert_kernel(idx_prev, idx_cur, idx_next, wt_ref, h2_ref, yin_ref, wg_ref, wu_ref, wd_ref, y_ref,
                   xs32, xs, yb, acc, gx, gy, sc, *, ts, nf, s_tiles):
    del yin_ref
    sub = h2_ref.shape[1]
    t = pl.program_id(0)
    f = pl.program_id(1)
    last_t = pl.num_programs(0) - 1
    rps = ts // nf
    xslot, yslot = t % 2, t % 3
    nxslot, nyslot = (t + 1) % 2, (t + 1) % 3
    pyslot = (t + 2) % 3
    plane = (t // s_tiles) % 2
    nplane = (jnp.minimum(t + 1, last_t) // s_tiles) % 2
    pplane = (jnp.maximum(t - 1, 0) // s_tiles) % 2

    def row_of(ref, row):
        return ref.at[row >> 3, :, pl.ds(row & (SUBLANES - 1), 1), :]

    def slot_row(ref, ra, rs):
        return ref.at[ra, :, pl.ds(rs, 1), :]

    def x_copy(token, ra, rs, slot):
        return pltpu.make_async_copy(row_of(h2_ref, token), slot_row(xs32.at[slot], ra, rs), gx.at[slot])

    def y_in_copy(token, ra, rs, slot, pln):
        return pltpu.make_async_copy(row_of(y_ref.at[pln], token), slot_row(yb.at[slot], ra, rs), gy.at[slot])

    def y_out_copy(token, ra, rs, slot, pln):
        return pltpu.make_async_copy(slot_row(yb.at[slot], ra, rs), row_of(y_ref.at[pln], token), sc.at[slot])

    def wait_x(slot):
        pltpu.make_async_copy(h2_ref.at[pl.ds(0, ts // SUBLANES)], xs32.at[slot], gx.at[slot]).wait()

    def wait_y_in(slot):
        pltpu.make_async_copy(y_ref.at[0, pl.ds(0, ts // SUBLANES)], yb.at[slot], gy.at[slot]).wait()

    def wait_y_out(slot):
        pltpu.make_async_copy(yb.at[slot], y_ref.at[0, pl.ds(0, ts // SUBLANES)], sc.at[slot]).wait()

    def for_rows(fn):
        def body(r, carry):
            fn(idx_cur[0, 0, r], r >> 3, r & (SUBLANES - 1))
            return carry
        lax.fori_loop(0, ts, body, 0)

    @pl.when(f == 0)
    def _():
        @pl.when(t == 0)
        def _():
            def start(token, ra, rs):
                x_copy(token, ra, rs, 0).start()
                y_in_copy(token, ra, rs, 0, 0).start()
                y_in_copy(token, ra, rs, 2, 0).start()
            for_rows(start)
            wait_y_in(2)

        wait_x(xslot)
        wait_y_in(yslot)
        for c in range(sub):
            xs[:, c * LANES:(c + 1) * LANES] = _from_row_tiles(xs32[xslot, :, c]).astype(BF16)

    for k in range(rps):
        r = f * rps + k
        ra, rs = f * (rps // SUBLANES) + k // SUBLANES, k % SUBLANES
        x_copy(idx_next[0, 0, r], ra, rs, nxslot).start()
        y_in_copy(idx_next[0, 0, r], ra, rs, nyslot, nplane).start()
        y_out_copy(idx_prev[0, 0, r], ra, rs, pyslot, pplane).start()

    x = xs[...]
    a = jnp.dot(x, wg_ref[0], preferred_element_type=F32)
    b = jnp.dot(x, wu_ref[0], preferred_element_type=F32)
    hm = (jax.nn.silu(a) * b).astype(BF16)
    part = jnp.dot(hm, wd_ref[0], preferred_element_type=F32)

    @pl.when(f == 0)
    def _():
        acc[...] = part

    @pl.when(f > 0)
    def _():
        acc[...] += part

    @pl.when(f == nf - 1)
    def _():
        contrib = acc[...] * wt_ref[0]
        for c in range(sub):
            yb[yslot, :, c] += _to_row_tiles(contrib[:, c * LANES:(c + 1) * LANES])
        wait_y_out(pyslot)

        @pl.when(t == last_t)
        def _():
            wait_x(nxslot)
            wait_y_in(nyslot)
            for_rows(lambda token, ra, rs: y_out_copy(token, ra, rs, yslot, plane).start())
            wait_y_out(yslot)


def expert_ffn(idx, weight, h2, y2, w_gate, w_up, w_down, tf=512):
    ne, cap = idx.shape
    d = h2.shape[1] * LANES
    ff = w_gate.shape[2]
    ts = EXPERT_TS
    tf = min(tf, ff)
    nf = ff // tf
    s_tiles = cap // ts
    assert cap % ts == 0 and s_tiles >= 2 and ts % nf == 0 and y2.shape == (2,) + h2.shape
    nt = ne * s_tiles
    idx3 = idx.reshape(nt, 1, ts)
    wt3 = weight.reshape(nt, ts, 1)
    smem_tile = lambda shift: pl.BlockSpec(
        (1, 1, ts), lambda t, f: (jnp.clip(t + shift, 0, nt - 1), 0, 0), memory_space=pltpu.SMEM)
    return pl.pallas_call(
        functools.partial(_expert_kernel, ts=ts, nf=nf, s_tiles=s_tiles),
        grid=(nt, nf),
        in_specs=[smem_tile(-1), smem_tile(0), smem_tile(1),
                  pl.BlockSpec((1, ts, 1), lambda t, f: (t, 0, 0)),
                  pl.BlockSpec(memory_space=pl.ANY),
                  pl.BlockSpec(memory_space=pl.ANY),
                  pl.BlockSpec((1, d, tf), lambda t, f: (t // s_tiles, 0, f)),
                  pl.BlockSpec((1, d, tf), lambda t, f: (t // s_tiles, 0, f)),
                  pl.BlockSpec((1, tf, d), lambda t, f: (t // s_tiles, f, 0))],
        out_specs=pl.BlockSpec(memory_space=pl.ANY),
        out_shape=jax.ShapeDtypeStruct(y2.shape, F32),
        scratch_shapes=[pltpu.VMEM((2,) + _row_tile_shape(ts, d), F32), pltpu.VMEM((ts, d), BF16),
                        pltpu.VMEM((3,) + _row_tile_shape(ts, d), F32),
                        pltpu.VMEM((ts, d), F32), pltpu.SemaphoreType.DMA((2,)),
                        pltpu.SemaphoreType.DMA((3,)), pltpu.SemaphoreType.DMA((3,))],
        input_output_aliases={5: 0},
        compiler_params=_params("arbitrary", "arbitrary"),
        name="expert_ffn",
    )(idx3, idx3, idx3, wt3, h2, y2, w_gate, w_up, w_down)


def _final_norm_kernel(y_ref, g_ref, o_ref):
    x = jnp.concatenate([_from_row_tiles(y_ref[0, :, c] + y_ref[1, :, c]) for c in range(y_ref.shape[2])], axis=1)
    o_ref[...] = _rms(x, g_ref[...])


def final_norm(y2, g, tm=512):
    n, d = y2.shape[1] * SUBLANES, y2.shape[2] * LANES
    tm = min(tm, n)
    assert n % tm == 0
    return pl.pallas_call(
        _final_norm_kernel,
        grid=(n // tm,),
        in_specs=[pl.BlockSpec((2,) + _row_tile_shape(tm, d), lambda i: (0, i, 0, 0, 0)),
                  pl.BlockSpec((1, d), lambda i: (0, 0))],
        out_specs=pl.BlockSpec((tm, d), lambda i: (i, 0)),
        out_shape=jax.ShapeDtypeStruct((n, d), F32),
        compiler_params=_params("parallel"),
        name="final_norm",
    )(y2, g)


def moe_and_norm(y2, h2, aff, w_e_gate, w_e_up, w_e_down, g_final):
    n = h2.shape[0] * SUBLANES
    cap = EC_CAPACITY * n // N_EXPERTS
    sel, pos, off = select(aff, cap)
    choff = off[:, :, 0].astype(I32).reshape(-1)
    idx, weight = compact(sel, pos, choff, aff, cap)
    y2 = expert_ffn(idx, weight, h2, y2, w_e_gate, w_e_up, w_e_down)
    return final_norm(y2, g_final)


def encoder_group(x, mem, w, experts, g_final):
    b, t, d = x.shape
    y2, h2, aff = layer_front(x, mem, w)
    return moe_and_norm(y2, h2, aff, *experts, g_final).reshape(b, t, d)


def kernel(x_prompt, x_sample, mem_prompt, mem_sample, g_mix, g_mem, w_in, w_mem_kv, na_rpb, t5_table, sw_sink,
           w_na_o, w_sw_o, w_mx_o, w_out, g_ffn, w_router, w_e_gate, w_e_up, w_e_down, g_final):
    assert g_mix.shape[0] == 1, "single-layer trunk"
    w = prep_weights(g_mix[0], g_mem[0], w_in[0], w_mem_kv[0], na_rpb[0], t5_table, sw_sink[0], w_na_o[0],
                     w_sw_o[0], w_mx_o[0], w_out[0], g_ffn[0], w_router[0])
    experts = (w_e_gate[0].astype(BF16), w_e_up[0].astype(BF16), w_e_down[0].astype(BF16))
    gf = g_final.reshape(1, -1).astype(F32)
    y_prompt = encoder_group(x_prompt, mem_prompt, w, experts, gf)
    y_sample = encoder_group(x_sample, mem_sample, w, experts, gf)
    return (y_prompt, y_sample)
```

```python
import functools
import math

import numpy as np
import jax
import jax.numpy as jnp
from jax import lax
from jax.experimental import pallas as pl
from jax.experimental.pallas import tpu as pltpu

F32 = jnp.float32
BF16 = jnp.bfloat16
I32 = jnp.int32

RMS_EPS = 1e-6
NEG_INF = -1e30

GRID_W = 64
NA_HEADS = 8
NA_HEAD_DIM = 64
NA_KR = 8
NA_KC = 16
SW_HEADS = 16
SW_KV_HEADS = 4
SW_HEAD_DIM = 64
SW_WINDOW = 128
SW_BLOCK = 128
MX_HEADS = 4
T5_BUCKETS = 32
T5_MAX_DIST = 128
N_BRANCHES = 3
N_EXPERTS = 16
EC_CAPACITY = 2

LANES = 128
SUBLANES = 8
V7X_VMEM_BYTES = 64 * 1024 * 1024
VMEM_LIMIT = V7X_VMEM_BYTES * 7 // 8

CHUNK_UNROLL = 8
EXPERT_TS = 512

NT_DIMS = (((1,), (1,)), ((), ()))
TN_DIMS = (((0,), (0,)), ((), ()))


def _params(*sem):
    return pltpu.CompilerParams(dimension_semantics=sem, vmem_limit_bytes=VMEM_LIMIT)


def _rms(x, g):
    return x * lax.rsqrt(jnp.mean(x * x, axis=-1, keepdims=True) + RMS_EPS) * g


def _to_row_tiles(strip):
    return strip.reshape(strip.shape[0] // SUBLANES, SUBLANES, LANES)


def _from_row_tiles(tiles):
    return tiles.reshape(tiles.shape[0] * SUBLANES, LANES)


def _row_tile_shape(rows, d):
    return (rows // SUBLANES, d // LANES, SUBLANES, LANES)


def _norm_proj_kernel(x_ref, g_ref, w_ref, *o_refs, scales):
    h = _rms(x_ref[...], g_ref[...]).astype(BF16)
    off = 0
    for o_ref, sc in zip(o_refs, scales):
        width = o_ref.shape[1]
        for c0 in range(0, width, 512):
            cw = min(512, width - c0)
            r = jnp.dot(h, w_ref[:, off + c0:off + c0 + cw], preferred_element_type=F32)
            if sc != 1.0:
                r = r * sc
            o_ref[:, c0:c0 + cw] = r.astype(o_ref.dtype)
        off += width


def norm_proj(x, g, w, widths, scales, tm):
    n, d = x.shape
    tm = min(tm, n)
    assert n % tm == 0
    return pl.pallas_call(
        functools.partial(_norm_proj_kernel, scales=tuple(scales)),
        grid=(n // tm,),
        in_specs=[pl.BlockSpec((tm, d), lambda i: (i, 0)),
                  pl.BlockSpec((1, d), lambda i: (0, 0)),
                  pl.BlockSpec(w.shape, lambda i: (0, 0))],
        out_specs=[pl.BlockSpec((tm, c), lambda i: (i, 0)) for c in widths],
        out_shape=[jax.ShapeDtypeStruct((n, c), BF16) for c in widths],
        compiler_params=_params("parallel"),
        name="norm_proj",
    )(x, g, w)


def _softmax_pv(s, v, extra_logit=None):
    m = jnp.max(s, axis=-1, keepdims=True)
    if extra_logit is not None:
        m = jnp.maximum(m, extra_logit)
    e = jnp.exp(s - m)
    den = jnp.sum(e, axis=-1, keepdims=True)
    if extra_logit is not None:
        den = den + jnp.exp(extra_logit - m)
    o = jnp.dot(e.astype(BF16), v, preferred_element_type=F32)
    return o / den


def _softmax_pv_mxu_sum(s, v, extra_logit):
    rows, keys = s.shape
    mb = jnp.maximum(jnp.broadcast_to(jnp.max(s, axis=-1, keepdims=True), (rows, LANES)), extra_logit)
    e = jnp.concatenate([jnp.exp(s[:, t * LANES:(t + 1) * LANES] - mb) for t in range(keys // LANES)],
                        axis=1).astype(BF16)
    den = jnp.dot(e, jnp.ones((keys, v.shape[1]), BF16), preferred_element_type=F32) + jnp.exp(extra_logit - mb)
    return jnp.dot(e, v, preferred_element_type=F32) / den


def _na_kernel(q_ref, k_ref, v_ref, bias_ref, o_ref, *, rows, rb):
    j = pl.program_id(1)
    lo = lax.broadcasted_iota(I32, (GRID_W, LANES), 1) < NA_HEAD_DIM
    nkeys = NA_KR * GRID_W

    def body(i, carry):
        r = j * rb + i
        rs = jnp.clip(r - NA_KR // 2, 0, rows - NA_KR)
        off = r - rs
        q = q_ref[0, pl.ds(pl.multiple_of(i * GRID_W, GRID_W), GRID_W), :]
        kk = k_ref[0, pl.ds(pl.multiple_of(rs * GRID_W, GRID_W), nkeys), :]
        vv = v_ref[0, pl.ds(pl.multiple_of(rs * GRID_W, GRID_W), nkeys), :]
        scores = []
        for p in range(NA_HEADS // 2):
            qp = q[:, p * LANES:(p + 1) * LANES]
            kp = kk[:, p * LANES:(p + 1) * LANES]
            zero = jnp.zeros_like(qp)
            q2 = jnp.concatenate([jnp.where(lo, qp, zero), jnp.where(lo, zero, qp)], axis=0)
            s = lax.dot_general(q2, kp, NT_DIMS, preferred_element_type=F32)
            scores.append(s + bias_ref[off, 2 * p:2 * p + 2].reshape(2 * GRID_W, nkeys))
        outs = []
        for p in range(NA_HEADS // 2):
            o2 = _softmax_pv(scores[p], vv[:, p * LANES:(p + 1) * LANES])
            outs.append(jnp.where(lo, o2[:GRID_W], o2[GRID_W:]))
        o_ref[0, pl.ds(pl.multiple_of(i * GRID_W, GRID_W), GRID_W), :] = (
            jnp.concatenate(outs, axis=1).astype(o_ref.dtype))
        return carry

    lax.fori_loop(0, rb, body, 0)


def na_attention(q, k, v, bias, rb=8):
    b, t, c = q.shape
    rows = t // GRID_W
    assert rows >= NA_KR and rows % rb == 0
    return pl.pallas_call(
        functools.partial(_na_kernel, rows=rows, rb=rb),
        grid=(b, rows // rb),
        in_specs=[pl.BlockSpec((1, rb * GRID_W, c), lambda i, j: (i, j, 0)),
                  pl.BlockSpec((1, t, c), lambda i, j: (i, 0, 0)),
                  pl.BlockSpec((1, t, c), lambda i, j: (i, 0, 0)),
                  pl.BlockSpec(bias.shape, lambda i, j: (0, 0, 0, 0))],
        out_specs=pl.BlockSpec((1, rb * GRID_W, c), lambda i, j: (i, j, 0)),
        out_shape=jax.ShapeDtypeStruct((b, t, c), BF16),
        compiler_params=_params("parallel", "arbitrary"),
        name="na_attention",
    )(q, k, v, bias)


def na_bias_table(rpb):
    col = np.arange(GRID_W)
    col_start = np.clip(col - NA_KC // 2, 0, GRID_W - NA_KC)
    in_win = (col[None, :] >= col_start[:, None]) & (col[None, :] < col_start[:, None] + NA_KC)
    dc = np.clip(col[None, :] - col[:, None] + NA_KC - 1, 0, 2 * NA_KC - 2)
    heads = rpb.shape[0]
    by_col = jnp.take(rpb.astype(F32), jnp.asarray(dc.reshape(-1)), axis=2)
    by_col = by_col.reshape(heads, 2 * NA_KR - 1, GRID_W, GRID_W)
    by_col = jnp.where(jnp.asarray(in_win)[None, None], by_col, NEG_INF)
    per_off = [by_col[:, NA_KR - 1 - off:2 * NA_KR - 1 - off].transpose(0, 2, 1, 3) for off in range(NA_KR)]
    return jnp.stack(per_off).reshape(NA_KR, heads, GRID_W, NA_KR * GRID_W)


def _sw_kernel(sink_ref, q_ref, k_ref, v_ref, bias_ref, o_ref, s_scr, *, nb):
    n = pl.program_id(1)
    blk = SW_BLOCK
    group = SW_HEADS // SW_KV_HEADS

    def rows_of(ref, c):
        return ref[0, pl.ds(pl.multiple_of(c * blk, blk), blk), :]

    cl = jnp.maximum(n - 1, 0)
    cr = jnp.minimum(n + 1, nb - 1)
    k3 = jnp.concatenate([rows_of(k_ref, cl), rows_of(k_ref, n), rows_of(k_ref, cr)], axis=0)
    v3 = jnp.concatenate([rows_of(v_ref, cl), rows_of(v_ref, n), rows_of(v_ref, cr)], axis=0)
    pen_l = jnp.where(n > 0, 0.0, NEG_INF).astype(F32)
    pen_r = jnp.where(n < nb - 1, 0.0, NEG_INF).astype(F32)
    key = lax.broadcasted_iota(I32, (1, 3 * blk), 1)
    pen = jnp.where(key < blk, pen_l, jnp.where(key >= 2 * blk, pen_r, 0.0))
    lo = lax.broadcasted_iota(I32, (blk, LANES), 1) < SW_HEAD_DIM

    for c in range(SW_KV_HEADS):
        pair, half = divmod(c, 2)
        kp = k3[:, pair * LANES:(pair + 1) * LANES]
        keep = lo if half == 0 else jnp.logical_not(lo)
        qs = []
        for g in range(group):
            t = pair * group + g
            qt = q_ref[0, :, t * LANES:(t + 1) * LANES]
            qs.append(jnp.where(keep, qt, jnp.zeros_like(qt)))
        qq = jnp.concatenate(qs, axis=0)
        s = lax.dot_general(qq, kp, NT_DIMS, preferred_element_type=F32)
        s_scr[c] = s + bias_ref[c * group:(c + 1) * group].reshape(group * blk, 3 * blk) + pen

    for pair in range(SW_KV_HEADS // 2):
        vp = v3[:, pair * LANES:(pair + 1) * LANES]
        per_half = []
        for half in range(2):
            c = 2 * pair + half
            sink = jnp.concatenate(
                [jnp.full((blk, LANES), sink_ref[c * group + g], F32) for g in range(group)], axis=0)
            per_half.append(_softmax_pv_mxu_sum(s_scr[c], vp, sink))
        for g in range(group):
            t = pair * group + g
            o = jnp.where(lo, per_half[0][g * blk:(g + 1) * blk], per_half[1][g * blk:(g + 1) * blk])
            o_ref[0, :, t * LANES:(t + 1) * LANES] = o.astype(o_ref.dtype)


def sw_attention(q, k, v, bias, sink):
    b, t, c = q.shape
    nb = t // SW_BLOCK
    kvw = k.shape[2]
    grid_spec = pltpu.PrefetchScalarGridSpec(
        num_scalar_prefetch=1,
        grid=(b, nb),
        in_specs=[pl.BlockSpec((1, SW_BLOCK, c), lambda i, j, s: (i, j, 0)),
                  pl.BlockSpec((1, t, kvw), lambda i, j, s: (i, 0, 0)),
                  pl.BlockSpec((1, t, kvw), lambda i, j, s: (i, 0, 0)),
                  pl.BlockSpec(bias.shape, lambda i, j, s: (0, 0, 0))],
        out_specs=pl.BlockSpec((1, SW_BLOCK, c), lambda i, j, s: (i, j, 0)),
        scratch_shapes=[pltpu.VMEM((SW_KV_HEADS, (SW_HEADS // SW_KV_HEADS) * SW_BLOCK, 3 * SW_BLOCK), F32)],
    )
    return pl.pallas_call(
        functools.partial(_sw_kernel, nb=nb),
        grid_spec=grid_spec,
        out_shape=jax.ShapeDtypeStruct((b, t, c), BF16),
        compiler_params=_params("parallel", "arbitrary"),
        name="sw_attention",
    )(sink, q, k, v, bias)


def _t5_bucket(rel):
    half = T5_BUCKETS // 2
    max_exact = half // 2
    n = jnp.abs(rel)
    nf = jnp.maximum(n, 1).astype(F32)
    large = max_exact + (jnp.log(nf / max_exact) / math.log(T5_MAX_DIST / max_exact)
                         * (half - max_exact)).astype(jnp.int32)
    large = jnp.minimum(large, half - 1)
    return jnp.where(rel > 0, half, 0) + jnp.where(n < max_exact, n, large)


def sw_bias_table(t5_table):
    span = SW_BLOCK + 2 * SW_WINDOW
    rel = np.arange(span)[None, :] - SW_WINDOW - np.arange(SW_BLOCK)[:, None]
    bucket = _t5_bucket(jnp.asarray(rel, dtype=jnp.int32))[None]
    table = t5_table.astype(F32).T[:, :, None, None]
    bias = jnp.zeros((t5_table.shape[1], SW_BLOCK, span), F32)
    for b in range(T5_BUCKETS):
        bias = jnp.where(bucket == b, table[:, b], bias)
    return jnp.where(jnp.asarray(np.abs(rel) <= SW_WINDOW)[None], bias, NEG_INF)


def sw_pair_heads(w, axis):
    group = SW_HEADS // SW_KV_HEADS
    shape = w.shape
    split = shape[:axis] + (SW_KV_HEADS // 2, 2, group, SW_HEAD_DIM) + shape[axis + 1:]
    order = list(range(len(split)))
    order[axis + 1], order[axis + 2] = axis + 2, axis + 1
    return w.reshape(split).transpose(order).reshape(shape)


def _mx_kernel(q_ref, mk_ref, mv_ref, o_ref, *, scale):
    hd = q_ref.shape[2] // MX_HEADS
    for h in range(MX_HEADS):
        sl = slice(h * hd, (h + 1) * hd)
        s = lax.dot_general(q_ref[0, :, sl], mk_ref[0, :, sl], NT_DIMS, preferred_element_type=F32) * scale
        o_ref[0, :, sl] = _softmax_pv(s, mv_ref[0, :, sl]).astype(o_ref.dtype)


def mx_attention(q, mkv, tq=512):
    b, t, c = q.shape
    m = mkv.shape[1]
    tq = min(tq, t)
    return pl.pallas_call(
        functools.partial(_mx_kernel, scale=float((c // MX_HEADS) ** -0.5)),
        grid=(b, t // tq),
        in_specs=[pl.BlockSpec((1, tq, c), lambda i, j: (i, j, 0)),
                  pl.BlockSpec((1, m, c), lambda i, j: (i, 0, 0)),
                  pl.BlockSpec((1, m, c), lambda i, j: (i, 0, 1))],
        out_specs=pl.BlockSpec((1, tq, c), lambda i, j: (i, j, 0)),
        out_shape=jax.ShapeDtypeStruct((b, t, c), BF16),
        compiler_params=_params("parallel", "arbitrary"),
        name="mx_attention",
    )(q, mkv, mkv)


def _merge_kernel(x_ref, ona_ref, osw_ref, omx_ref, gmix_ref, wg_ref, wna_ref, wsw_ref, wmx_ref, wout_ref,
                  gffn_ref, wrh_ref, wrl_ref, y2_ref, h2_ref, aff_ref, h_scr):
    j = pl.program_id(1)
    strips = range(h2_ref.shape[1])

    @pl.when(j == 0)
    def _():
        x = x_ref[...]
        h_scr[...] = _rms(x, gmix_ref[...]).astype(BF16)
        for c in strips:
            h2_ref[:, c] = _to_row_tiles(x[:, c * LANES:(c + 1) * LANES])

    h = h_scr[...]
    merged = None
    for o_ref, w_ref, b in ((ona_ref, wna_ref, 0), (osw_ref, wsw_ref, 1), (omx_ref, wmx_ref, 2)):
        gate = jax.nn.sigmoid(jnp.dot(h, wg_ref[b], preferred_element_type=F32))
        term = gate * jnp.dot(o_ref[...], w_ref[...], preferred_element_type=F32)
        merged = term if merged is None else merged + term
    part = jnp.dot(merged.astype(BF16), wout_ref[...], preferred_element_type=F32)
    for c in strips:
        h2_ref[:, c] += _to_row_tiles(part[:, c * LANES:(c + 1) * LANES])

    @pl.when(j == pl.num_programs(1) - 1)
    def _():
        for c in strips:
            y2_ref[0, :, c] = h2_ref[:, c]
        y2_ref[1] = jnp.zeros(y2_ref.shape[1:], F32)
        x2 = jnp.concatenate([_from_row_tiles(h2_ref[:, c]) for c in strips], axis=1)
        h2 = _rms(x2, gffn_ref[...])
        for c in strips:
            h2_ref[:, c] = _to_row_tiles(h2[:, c * LANES:(c + 1) * LANES])
        hi = h2.astype(BF16)
        lo = (h2 - hi.astype(F32)).astype(BF16)
        wh = wrh_ref[...]
        logits = (lax.dot_general(wh, hi, NT_DIMS, preferred_element_type=F32)
                  + lax.dot_general(wh, lo, NT_DIMS, preferred_element_type=F32)
                  + lax.dot_general(wrl_ref[...], hi, NT_DIMS, preferred_element_type=F32))
        m = jnp.max(logits, axis=0, keepdims=True)
        e = jnp.exp(logits - m)
        aff = e / jnp.sum(e, axis=0, keepdims=True)
        for c in range(aff_ref.shape[0]):
            aff_ref[c] = aff[:, c * LANES:(c + 1) * LANES]


def merge(x, o_na, o_sw, o_mx, g_mix, w_gate, w_na_o, w_sw_o, w_mx_o, w_out, g_ffn, wr_hi, wr_lo,
          tm=512, tn=256):
    n, d = x.shape
    tm = min(tm, n)
    assert n % tm == 0 and tm % LANES == 0 and d % LANES == 0
    ne = wr_hi.shape[0]
    row = lambda i, j: (i, 0)
    return pl.pallas_call(
        _merge_kernel,
        grid=(n // tm, d // tn),
        in_specs=[pl.BlockSpec((tm, d), row),
                  pl.BlockSpec((tm, o_na.shape[1]), row),
                  pl.BlockSpec((tm, o_sw.shape[1]), row),
                  pl.BlockSpec((tm, o_mx.shape[1]), row),
                  pl.BlockSpec((1, d), lambda i, j: (0, 0)),
                  pl.BlockSpec((N_BRANCHES, d, tn), lambda i, j: (0, 0, j)),
                  pl.BlockSpec((w_na_o.shape[0], tn), lambda i, j: (0, j)),
                  pl.BlockSpec((w_sw_o.shape[0], tn), lambda i, j: (0, j)),
                  pl.BlockSpec((w_mx_o.shape[0], tn), lambda i, j: (0, j)),
                  pl.BlockSpec((tn, d), lambda i, j: (j, 0)),
                  pl.BlockSpec((1, d), lambda i, j: (0, 0)),
                  pl.BlockSpec((ne, d), lambda i, j: (0, 0)),
                  pl.BlockSpec((ne, d), lambda i, j: (0, 0))],
        out_specs=[pl.BlockSpec((2,) + _row_tile_shape(tm, d), lambda i, j: (0, i, 0, 0, 0)),
                   pl.BlockSpec(_row_tile_shape(tm, d), lambda i, j: (i, 0, 0, 0)),
                   pl.BlockSpec((tm // LANES, ne, LANES), lambda i, j: (i, 0, 0))],
        out_shape=[jax.ShapeDtypeStruct((2,) + _row_tile_shape(n, d), F32),
                   jax.ShapeDtypeStruct(_row_tile_shape(n, d), F32),
                   jax.ShapeDtypeStruct((n // LANES, ne, LANES), F32)],
        scratch_shapes=[pltpu.VMEM((tm, d), BF16)],
        compiler_params=_params("parallel", "arbitrary"),
        name="merge",
    )(x, o_na, o_sw, o_mx, g_mix, w_gate, w_na_o, w_sw_o, w_mx_o, w_out, g_ffn, wr_hi, wr_lo)


NA_WIDTH = NA_HEADS * NA_HEAD_DIM
SW_WIDTH = SW_HEADS * SW_HEAD_DIM
SW_KV_WIDTH = SW_KV_HEADS * SW_HEAD_DIM
QKV_WIDTHS = (NA_WIDTH, NA_WIDTH, NA_WIDTH, SW_WIDTH, SW_KV_WIDTH, SW_KV_WIDTH)


def prep_weights(g_mix, g_mem, w_in, w_mem_kv, na_rpb, t5_table, sw_sink, w_na_o, w_sw_o, w_mx_o, w_out,
                 g_ffn, w_router):
    d = w_in.shape[0]
    mx_width = w_mx_o.shape[0]
    widths = QKV_WIDTHS + (mx_width,)
    offs = np.concatenate([[0], np.cumsum(widths)])
    cols = [w_in[:, offs[i]:offs[i + 1]] for i in range(len(widths))]
    cols[3] = sw_pair_heads(cols[3], axis=1)
    w_gate = w_in[:, offs[-1]:].reshape(d, N_BRANCHES, d).transpose(1, 0, 2)
    wr = w_router.T.astype(F32)
    wr_hi = wr.astype(BF16)
    return dict(
        g_mix=g_mix.reshape(1, d).astype(F32), g_mem=g_mem.reshape(1, d).astype(F32),
        g_ffn=g_ffn.reshape(1, d).astype(F32),
        w_qkv=jnp.concatenate(cols, axis=1).astype(BF16), qkv_widths=widths,
        qkv_scales=(NA_HEAD_DIM ** -0.5, 1.0, 1.0, SW_HEAD_DIM ** -0.5, 1.0, 1.0, 1.0),
        w_mem_kv=w_mem_kv.astype(BF16),
        na_bias=na_bias_table(na_rpb), sw_bias=sw_bias_table(t5_table), sw_sink=sw_sink.astype(F32),
        w_gate=w_gate.astype(BF16), w_na_o=w_na_o.astype(BF16), w_sw_o=sw_pair_heads(w_sw_o, axis=0).astype(BF16),
        w_mx_o=w_mx_o.astype(BF16), w_out=w_out.astype(BF16),
        wr_hi=wr_hi, wr_lo=(wr - wr_hi.astype(F32)).astype(BF16))


def layer_front(x, mem, w, debug=False):
    b, t, d = x.shape
    m = mem.shape[1]
    xf = x.reshape(b * t, d)
    na_q, na_k, na_v, sw_q, sw_k, sw_v, mx_q = norm_proj(xf, w["g_mix"], w["w_qkv"], w["qkv_widths"],
                                                         w["qkv_scales"], tm=512)
    (mkv,) = norm_proj(mem.reshape(b * m, d), w["g_mem"], w["w_mem_kv"], (w["w_mem_kv"].shape[1],), (1.0,),
                       tm=512)
    r3 = lambda a: a.reshape(b, t, a.shape[1])
    o_na = na_attention(r3(na_q), r3(na_k), r3(na_v), w["na_bias"])
    o_sw = sw_attention(r3(sw_q), r3(sw_k), r3(sw_v), w["sw_bias"], w["sw_sink"])
    o_mx = mx_attention(r3(mx_q), mkv.reshape(b, m, mkv.shape[1]))
    f2 = lambda a: a.reshape(b * t, a.shape[2])
    x2, h2, aff = merge(xf, f2(o_na), f2(o_sw), f2(o_mx), w["g_mix"], w["w_gate"], w["w_na_o"], w["w_sw_o"],
                        w["w_mx_o"], w["w_out"], w["g_ffn"], w["wr_hi"], w["wr_lo"])
    if debug:
        return o_na, o_sw, o_mx, x2, h2, aff
    return x2, h2, aff


def _select_kernel(aff_ref, sel_ref, pos_ref, off_ref, *, cap):
    nc, ne, _ = aff_ref.shape
    bits = lax.bitcast_convert_type(aff_ref[...], I32)
    tok = (lax.broadcasted_iota(I32, bits.shape, 0) * LANES + lax.broadcasted_iota(I32, bits.shape, 2))

    def count(flags):
        return jnp.sum(flags, axis=(0, 2), keepdims=True)

    def value_step(i, prefix):
        cand = prefix | lax.shift_left(jnp.int32(1), 30 - i)
        cnt = count(jnp.where(bits >= cand, 1.0, 0.0))
        return jnp.where(cnt >= cap, cand, prefix)

    tau = lax.fori_loop(0, 31, value_step, jnp.zeros((1, ne, 1), I32))
    gt = bits > tau
    eq = bits == tau
    need = cap - count(jnp.where(gt, 1.0, 0.0))

    def index_step(i, last):
        cand = last | lax.shift_left(jnp.int32(1), 15 - i)
        cnt = count(jnp.where(eq, jnp.where(tok < cand, 1.0, 0.0), 0.0))
        return jnp.where(cnt < need, cand, last)

    last = lax.fori_loop(0, 16, index_step, jnp.zeros((1, ne, 1), I32))
    sel_ref[...] = jnp.where(gt, 1.0, jnp.where(eq, jnp.where(tok <= last, 1.0, 0.0), 0.0))

    upper = (lax.broadcasted_iota(I32, (LANES, LANES), 0) <= lax.broadcasted_iota(I32, (LANES, LANES), 1))
    upper = jnp.where(upper, 1.0, 0.0).astype(BF16)

    def chunk_step(c, off):
        s = sel_ref[c]
        cum = jnp.dot(s.astype(BF16), upper, preferred_element_type=F32)
        pos_ref[c] = off + cum - s
        off_ref[c] = jnp.broadcast_to(off, s.shape)
        return off + cum[:, LANES - 1:LANES]

    lax.fori_loop(0, nc, chunk_step, jnp.zeros((ne, 1), F32), unroll=CHUNK_UNROLL)


def select(aff, cap):
    nc, ne, _ = aff.shape
    assert nc * LANES <= 65536
    shape = jax.ShapeDtypeStruct(aff.shape, F32)
    return pl.pallas_call(
        functools.partial(_select_kernel, cap=float(cap)),
        out_shape=[shape, shape, shape],
        compiler_params=pltpu.CompilerParams(vmem_limit_bytes=VMEM_LIMIT),
        name="select",
    )(aff)


TOK_COL, CHUNK_COL, HI_COL = 0, 1, 2


def _compact_kernel(choff_ref, sel_ref, pos_ref, aff_ref, o_ref, tv_scr):
    e = pl.program_id(0)
    nc, ne, _ = aff_ref.shape
    win = 2 * LANES
    col = lax.broadcasted_iota(I32, (LANES, LANES), 1)
    row = lax.broadcasted_iota(I32, (LANES, LANES), 0)

    @pl.when(e == 0)
    def _():
        ecol = lax.broadcasted_iota(I32, (ne, LANES), 1) - lax.broadcasted_iota(I32, (ne, LANES), 0)
        place = [jnp.where(ecol == HI_COL + k * ne, 1.0, 0.0).astype(BF16) for k in range(3)]

        def build(c, carry):
            a = aff_ref[c]
            hi = a.astype(BF16)
            r1 = a - hi.astype(F32)
            mid = r1.astype(BF16)
            lo = (r1 - mid.astype(F32)).astype(BF16)
            rec = (lax.dot_general(hi, place[0], TN_DIMS, preferred_element_type=F32)
                   + lax.dot_general(mid, place[1], TN_DIMS, preferred_element_type=F32)
                   + lax.dot_general(lo, place[2], TN_DIMS, preferred_element_type=F32))
            rec = rec + jnp.where(col == TOK_COL, row, jnp.where(col == CHUNK_COL, c, 0)).astype(F32)
            tv_scr[c] = rec.astype(BF16)
            return carry

        lax.fori_loop(0, nc, build, 0, unroll=CHUNK_UNROLL)

    o_ref[...] = jnp.zeros_like(o_ref)
    slot0 = lax.broadcasted_iota(I32, (win, LANES), 0).astype(F32)

    def body(c, carry):
        off = choff_ref[c * ne + e]
        base = pl.multiple_of((off // LANES) * LANES, LANES)
        s = sel_ref[c, pl.ds(e, 1), :]
        p = pos_ref[c, pl.ds(e, 1), :] - base.astype(F32)
        onehot = jnp.where(s > 0.0, jnp.where(slot0 == p, 1.0, 0.0), 0.0).astype(BF16)
        o_ref[0, pl.ds(base, win), :] += jnp.dot(onehot, tv_scr[c], preferred_element_type=F32)
        return carry

    lax.fori_loop(0, nc, body, 0, unroll=CHUNK_UNROLL)


def compact(sel, pos, choff, aff, cap):
    nc, ne, _ = aff.shape
    assert nc <= 256 and HI_COL + 3 * ne <= LANES and cap % LANES == 0
    rows = cap + 2 * LANES
    full = pl.BlockSpec(aff.shape, lambda e, s: (0, 0, 0))
    rec = pl.pallas_call(
        _compact_kernel,
        grid_spec=pltpu.PrefetchScalarGridSpec(
            num_scalar_prefetch=1, grid=(ne,),
            in_specs=[full, full, full],
            out_specs=pl.BlockSpec((1, rows, LANES), lambda e, s: (e, 0, 0)),
            scratch_shapes=[pltpu.VMEM((nc, LANES, LANES), BF16)]),
        out_shape=jax.ShapeDtypeStruct((ne, rows, LANES), F32),
        compiler_params=_params("arbitrary"),
        name="compact",
    )(choff, sel, pos, aff)
    rec = rec[:, :cap]
    idx = (rec[:, :, CHUNK_COL] * LANES + rec[:, :, TOK_COL]).astype(I32)
    digits = rec[:, :, HI_COL:HI_COL + 3 * ne].reshape(ne, cap, 3, ne)
    own = jnp.eye(ne, dtype=F32)[:, None, None, :]
    weight = jnp.sum(jnp.sum(digits * own, axis=-1), axis=-1)
    return idx, weight


def _expert_kernel(idx_prev, idx_cur, idx_next, wt_ref, h2_ref, yin_ref, wg_ref, wu_ref, wd_ref, y_ref,
                   xs32, xs, yb, acc, gx, gy, sc, *, ts, nf, s_tiles):
    del yin_ref
    sub = h2_ref.shape[1]
    t = pl.program_id(0)
    f = pl.program_id(1)
    last_t = pl.num_programs(0) - 1
    rps = ts // nf
    xslot, yslot = t % 2, t % 3
    nxslot, nyslot = (t + 1) % 2, (t + 1) % 3
    pyslot = (t + 2) % 3
    plane = (t // s_tiles) % 2
    nplane = (jnp.minimum(t + 1, last_t) // s_tiles) % 2
    pplane = (jnp.maximum(t - 1, 0) // s_tiles) % 2

    def row_of(ref, row):
        return ref.at[row >> 3, :, pl.ds(row & (SUBLANES - 1), 1), :]

    def slot_row(ref, ra, rs):
        return ref.at[ra, :, pl.ds(rs, 1), :]

    def x_copy(token, ra, rs, slot):
        return pltpu.make_async_copy(row_of(h2_ref, token), slot_row(xs32.at[slot], ra, rs), gx.at[slot])

    def y_in_copy(token, ra, rs, slot, pln):
        return pltpu.make_async_copy(row_of(y_ref.at[pln], token), slot_row(yb.at[slot], ra, rs), gy.at[slot])

    def y_out_copy(token, ra, rs, slot, pln):
        return pltpu.make_async_copy(slot_row(yb.at[slot], ra, rs), row_of(y_ref.at[pln], token), sc.at[slot])

    def wait_x(slot):
        pltpu.make_async_copy(h2_ref.at[pl.ds(0, ts // SUBLANES)], xs32.at[slot], gx.at[slot]).wait()

    def wait_y_in(slot):
        pltpu.make_async_copy(y_ref.at[0, pl.ds(0, ts // SUBLANES)], yb.at[slot], gy.at[slot]).wait()

    def wait_y_out(slot):
        pltpu.make_async_copy(yb.at[slot], y_ref.at[0, pl.ds(0, ts // SUBLANES)], sc.at[slot]).wait()

    def for_rows(fn):
        def body(r, carry):
            fn(idx_cur[0, 0, r], r >> 3, r & (SUBLANES - 1))
            return carry
        lax.fori_loop(0, ts, body, 0)

    @pl.when(f == 0)
    def _():
        @pl.when(t == 0)
        def _():
            def start(token, ra, rs):
                x_copy(token, ra, rs, 0).start()
                y_in_copy(token, ra, rs, 0, 0).start()
                y_in_copy(token, ra, rs, 2, 0).start()
            for_rows(start)
            wait_y_in(2)

        wait_x(xslot)
        wait_y_in(yslot)
        for c in range(sub):
            xs[:, c * LANES:(c + 1) * LANES] = _from_row_tiles(xs32[xslot, :, c]).astype(BF16)

    for k in range(rps):
        r = f * rps + k
        ra, rs = f * (rps // SUBLANES) + k // SUBLANES, k % SUBLANES
        x_copy(idx_next[0, 0, r], ra, rs, nxslot).start()
        y_in_copy(idx_next[0, 0, r], ra, rs, nyslot, nplane).start()
        y_out_copy(idx_prev[0, 0, r], ra, rs, pyslot, pplane).start()

    x = xs[...]
    a = jnp.dot(x, wg_ref[0], preferred_element_type=F32)
    b = jnp.dot(x, wu_ref[0], preferred_element_type=F32)
    hm = (jax.nn.silu(a) * b).astype(BF16)
    part = jnp.dot(hm, wd_ref[0], preferred_element_type=F32)

    @pl.when(f == 0)
    def _():
        acc[...] = part

    @pl.when(f > 0)
    def _():
        acc[...] += part

    @pl.when(f == nf - 1)
    def _():
        contrib = acc[...] * wt_ref[0]
        for c in range(sub):
            yb[yslot, :, c] += _to_row_tiles(contrib[:, c * LANES:(c + 1) * LANES])
        wait_y_out(pyslot)

        @pl.when(t == last_t)
        def _():
            wait_x(nxslot)
            wait_y_in(nyslot)
            for_rows(lambda token, ra, rs: y_out_copy(token, ra, rs, yslot, plane).start())
            wait_y_out(yslot)


def expert_ffn(idx, weight, h2, y2, w_gate, w_up, w_down, tf=512):
    ne, cap = idx.shape
    d = h2.shape[1] * LANES
    ff = w_gate.shape[2]
    ts = EXPERT_TS
    tf = min(tf, ff)
    nf = ff // tf
    s_tiles = cap // ts
    assert cap % ts == 0 and s_tiles >= 2 and ts % nf == 0 and y2.shape == (2,) + h2.shape
    nt = ne * s_tiles
    idx3 = idx.reshape(nt, 1, ts)
    wt3 = weight.reshape(nt, ts, 1)
    smem_tile = lambda shift: pl.BlockSpec(
        (1, 1, ts), lambda t, f: (jnp.clip(t + shift, 0, nt - 1), 0, 0), memory_space=pltpu.SMEM)
    return pl.pallas_call(
        functools.partial(_expert_kernel, ts=ts, nf=nf, s_tiles=s_tiles),
        grid=(nt, nf),
        in_specs=[smem_tile(-1), smem_tile(0), smem_tile(1),
                  pl.BlockSpec((1, ts, 1), lambda t, f: (t, 0, 0)),
                  pl.BlockSpec(memory_space=pl.ANY),
                  pl.BlockSpec(memory_space=pl.ANY),
                  pl.BlockSpec((1, d, tf), lambda t, f: (t // s_tiles, 0, f)),
                  pl.BlockSpec((1, d, tf), lambda t, f: (t // s_tiles, 0, f)),
                  pl.BlockSpec((1, tf, d), lambda t, f: (t // s_tiles, f, 0))],
        out_specs=pl.BlockSpec(memory_space=pl.ANY),
        out_shape=jax.ShapeDtypeStruct(y2.shape, F32),
        scratch_shapes=[pltpu.VMEM((2,) + _row_tile_shape(ts, d), F32), pltpu.VMEM((ts, d), BF16),
                        pltpu.VMEM((3,) + _row_tile_shape(ts, d), F32),
                        pltpu.VMEM((ts, d), F32), pltpu.SemaphoreType.DMA((2,)),
                        pltpu.SemaphoreType.DMA((3,)), pltpu.SemaphoreType.DMA((3,))],
        input_output_aliases={5: 0},
        compiler_params=_params("arbitrary", "arbitrary"),
        name="expert_ffn",
    )(idx3, idx3, idx3, wt3, h2, y2, w_gate, w_up, w_down)


def _final_norm_kernel(y_ref, g_ref, o_ref):
    x = jnp.concatenate([_from_row_tiles(y_ref[0, :, c] + y_ref[1, :, c]) for c in range(y_ref.shape[2])], axis=1)
    o_ref[...] = _rms(x, g_ref[...])


def final_norm(y2, g, tm=512):
    n, d = y2.shape[1] * SUBLANES, y2.shape[2] * LANES
    tm = min(tm, n)
    assert n % tm == 0
    return pl.pallas_call(
        _final_norm_kernel,
        grid=(n // tm,),
        in_specs=[pl.BlockSpec((2,) + _row_tile_shape(tm, d), lambda i: (0, i, 0, 0, 0)),
                  pl.BlockSpec((1, d), lambda i: (0, 0))],
        out_specs=pl.BlockSpec((tm, d), lambda i: (i, 0)),
        out_shape=jax.ShapeDtypeStruct((n, d), F32),
        compiler_params=_params("parallel"),
        name="final_norm",
    )(y2, g)


def moe_and_norm(y2, h2, aff, w_e_gate, w_e_up, w_e_down, g_final):
    n = h2.shape[0] * SUBLANES
    cap = EC_CAPACITY * n // N_EXPERTS
    sel, pos, off = select(aff, cap)
    choff = off[:, :, 0].astype(I32).reshape(-1)
    idx, weight = compact(sel, pos, choff, aff, cap)
    y2 = expert_ffn(idx, weight, h2, y2, w_e_gate, w_e_up, w_e_down)
    return final_norm(y2, g_final)


def encoder_group(x, mem, w, experts, g_final):
    b, t, d = x.shape
    y2, h2, aff = layer_front(x, mem, w)
    return moe_and_norm(y2, h2, aff, *experts, g_final).reshape(b, t, d)


def kernel(x_prompt, x_sample, mem_prompt, mem_sample, g_mix, g_mem, w_in, w_mem_kv, na_rpb, t5_table, sw_sink,
           w_na_o, w_sw_o, w_mx_o, w_out, g_ffn, w_router, w_e_gate, w_e_up, w_e_down, g_final):
    assert g_mix.shape[0] == 1, "single-layer trunk"
    w = prep_weights(g_mix[0], g_mem[0], w_in[0], w_mem_kv[0], na_rpb[0], t5_table, sw_sink[0], w_na_o[0],
                     w_sw_o[0], w_mx_o[0], w_out[0], g_ffn[0], w_router[0])
    experts = (w_e_gate[0].astype(BF16), w_e_up[0].astype(BF16), w_e_down[0].astype(BF16))
    gf = g_final.reshape(1, -1).astype(F32)
    y_prompt = encoder_group(x_prompt, mem_prompt, w, experts, gf)
    y_sample = encoder_group(x_sample, mem_sample, w, experts, gf)
    return (y_prompt, y_sample)
```

```python
import functools
import math

import numpy as np
import jax
import jax.numpy as jnp
from jax import lax
from jax.experimental import pallas as pl
from jax.experimental.pallas import tpu as pltpu

F32 = jnp.float32
BF16 = jnp.bfloat16
I32 = jnp.int32

RMS_EPS = 1e-6
NEG_INF = -1e30

GRID_W = 64
NA_HEADS = 8
NA_HEAD_DIM = 64
NA_KR = 8
NA_KC = 16
SW_HEADS = 16
SW_KV_HEADS = 4
SW_HEAD_DIM = 64
SW_WINDOW = 128
SW_BLOCK = 128
MX_HEADS = 4
T5_BUCKETS = 32
T5_MAX_DIST = 128
N_BRANCHES = 3
N_EXPERTS = 16
EC_CAPACITY = 2

LANES = 128
SUBLANES = 8
V7X_VMEM_BYTES = 64 * 1024 * 1024
VMEM_LIMIT = V7X_VMEM_BYTES * 7 // 8

CHUNK_UNROLL = 8
EXPERT_TS = 512

NT_DIMS = (((1,), (1,)), ((), ()))
TN_DIMS = (((0,), (0,)), ((), ()))


def _params(*sem):
    return pltpu.CompilerParams(dimension_semantics=sem, vmem_limit_bytes=VMEM_LIMIT)


def _rms(x, g):
    return x * lax.rsqrt(jnp.mean(x * x, axis=-1, keepdims=True) + RMS_EPS) * g


def _to_row_tiles(strip):
    return strip.reshape(strip.shape[0] // SUBLANES, SUBLANES, LANES)


def _from_row_tiles(tiles):
    return tiles.reshape(tiles.shape[0] * SUBLANES, LANES)


def _row_tile_shape(rows, d):
    return (rows // SUBLANES, d // LANES, SUBLANES, LANES)


def _norm_proj_kernel(x_ref, g_ref, w_ref, *o_refs, scales):
    h = _rms(x_ref[...], g_ref[...]).astype(BF16)
    off = 0
    for o_ref, sc in zip(o_refs, scales):
        width = o_ref.shape[1]
        for c0 in range(0, width, 512):
            cw = min(512, width - c0)
            r = jnp.dot(h, w_ref[:, off + c0:off + c0 + cw], preferred_element_type=F32)
            if sc != 1.0:
                r = r * sc
            o_ref[:, c0:c0 + cw] = r.astype(o_ref.dtype)
        off += width


def norm_proj(x, g, w, widths, scales, tm):
    n, d = x.shape
    tm = min(tm, n)
    assert n % tm == 0
    return pl.pallas_call(
        functools.partial(_norm_proj_kernel, scales=tuple(scales)),
        grid=(n // tm,),
        in_specs=[pl.BlockSpec((tm, d), lambda i: (i, 0)),
                  pl.BlockSpec((1, d), lambda i: (0, 0)),
                  pl.BlockSpec(w.shape, lambda i: (0, 0))],
        out_specs=[pl.BlockSpec((tm, c), lambda i: (i, 0)) for c in widths],
        out_shape=[jax.ShapeDtypeStruct((n, c), BF16) for c in widths],
        compiler_params=_params("parallel"),
        name="norm_proj",
    )(x, g, w)


def _softmax_pv(s, v, extra_logit=None):
    m = jnp.max(s, axis=-1, keepdims=True)
    if extra_logit is not None:
        m = jnp.maximum(m, extra_logit)
    e = jnp.exp(s - m)
    den = jnp.sum(e, axis=-1, keepdims=True)
    if extra_logit is not None:
        den = den + jnp.exp(extra_logit - m)
    o = jnp.dot(e.astype(BF16), v, preferred_element_type=F32)
    return o / den


def _softmax_pv_mxu_sum(s, v, extra_logit):
    rows, keys = s.shape
    mb = jnp.maximum(jnp.broadcast_to(jnp.max(s, axis=-1, keepdims=True), (rows, LANES)), extra_logit)
    e = jnp.concatenate([jnp.exp(s[:, t * LANES:(t + 1) * LANES] - mb) for t in range(keys // LANES)],
                        axis=1).astype(BF16)
    den = jnp.dot(e, jnp.ones((keys, v.shape[1]), BF16), preferred_element_type=F32) + jnp.exp(extra_logit - mb)
    return jnp.dot(e, v, preferred_element_type=F32) / den


def _na_kernel(q_ref, k_ref, v_ref, bias_ref, o_ref, *, rows, rb):
    j = pl.program_id(1)
    lo = lax.broadcasted_iota(I32, (GRID_W, LANES), 1) < NA_HEAD_DIM
    nkeys = NA_KR * GRID_W

    def body(i, carry):
        r = j * rb + i
        rs = jnp.clip(r - NA_KR // 2, 0, rows - NA_KR)
        off = r - rs
        q = q_ref[0, pl.ds(pl.multiple_of(i * GRID_W, GRID_W), GRID_W), :]
        kk = k_ref[0, pl.ds(pl.multiple_of(rs * GRID_W, GRID_W), nkeys), :]
        vv = v_ref[0, pl.ds(pl.multiple_of(rs * GRID_W, GRID_W), nkeys), :]
        scores = []
        for p in range(NA_HEADS // 2):
            qp = q[:, p * LANES:(p + 1) * LANES]
            kp = kk[:, p * LANES:(p + 1) * LANES]
            zero = jnp.zeros_like(qp)
            q2 = jnp.concatenate([jnp.where(lo, qp, zero), jnp.where(lo, zero, qp)], axis=0)
            s = lax.dot_general(q2, kp, NT_DIMS, preferred_element_type=F32)
            scores.append(s + bias_ref[off, 2 * p:2 * p + 2].reshape(2 * GRID_W, nkeys))
        outs = []
        for p in range(NA_HEADS // 2):
            o2 = _softmax_pv(scores[p], vv[:, p * LANES:(p + 1) * LANES])
            outs.append(jnp.where(lo, o2[:GRID_W], o2[GRID_W:]))
        o_ref[0, pl.ds(pl.multiple_of(i * GRID_W, GRID_W), GRID_W), :] = (
            jnp.concatenate(outs, axis=1).astype(o_ref.dtype))
        return carry

    lax.fori_loop(0, rb, body, 0)


def na_attention(q, k, v, bias, rb=8):
    b, t, c = q.shape
    rows = t // GRID_W
    assert rows >= NA_KR and rows % rb == 0
    return pl.pallas_call(
        functools.partial(_na_kernel, rows=rows, rb=rb),
        grid=(b, rows // rb),
        in_specs=[pl.BlockSpec((1, rb * GRID_W, c), lambda i, j: (i, j, 0)),
                  pl.BlockSpec((1, t, c), lambda i, j: (i, 0, 0)),
                  pl.BlockSpec((1, t, c), lambda i, j: (i, 0, 0)),
                  pl.BlockSpec(bias.shape, lambda i, j: (0, 0, 0, 0))],
        out_specs=pl.BlockSpec((1, rb * GRID_W, c), lambda i, j: (i, j, 0)),
        out_shape=jax.ShapeDtypeStruct((b, t, c), BF16),
        compiler_params=_params("parallel", "arbitrary"),
        name="na_attention",
    )(q, k, v, bias)


def na_bias_table(rpb):
    col = np.arange(GRID_W)
    col_start = np.clip(col - NA_KC // 2, 0, GRID_W - NA_KC)
    in_win = (col[None, :] >= col_start[:, None]) & (col[None, :] < col_start[:, None] + NA_KC)
    dc = np.clip(col[None, :] - col[:, None] + NA_KC - 1, 0, 2 * NA_KC - 2)
    heads = rpb.shape[0]
    by_col = jnp.take(rpb.astype(F32), jnp.asarray(dc.reshape(-1)), axis=2)
    by_col = by_col.reshape(heads, 2 * NA_KR - 1, GRID_W, GRID_W)
    by_col = jnp.where(jnp.asarray(in_win)[None, None], by_col, NEG_INF)
    per_off = [by_col[:, NA_KR - 1 - off:2 * NA_KR - 1 - off].transpose(0, 2, 1, 3) for off in range(NA_KR)]
    return jnp.stack(per_off).reshape(NA_KR, heads, GRID_W, NA_KR * GRID_W)


def _sw_kernel(sink_ref, q_ref, k_ref, v_ref, bias_ref, o_ref, s_scr, *, nb):
    n = pl.program_id(1)
    blk = SW_BLOCK
    group = SW_HEADS // SW_KV_HEADS

    def rows_of(ref, c):
        return ref[0, pl.ds(pl.multiple_of(c * blk, blk), blk), :]

    cl = jnp.maximum(n - 1, 0)
    cr = jnp.minimum(n + 1, nb - 1)
    k3 = jnp.concatenate([rows_of(k_ref, cl), rows_of(k_ref, n), rows_of(k_ref, cr)], axis=0)
    v3 = jnp.concatenate([rows_of(v_ref, cl), rows_of(v_ref, n), rows_of(v_ref, cr)], axis=0)
    pen_l = jnp.where(n > 0, 0.0, NEG_INF).astype(F32)
    pen_r = jnp.where(n < nb - 1, 0.0, NEG_INF).astype(F32)
    key = lax.broadcasted_iota(I32, (1, 3 * blk), 1)
    pen = jnp.where(key < blk, pen_l, jnp.where(key >= 2 * blk, pen_r, 0.0))
    lo = lax.broadcasted_iota(I32, (blk, LANES), 1) < SW_HEAD_DIM

    for c in range(SW_KV_HEADS):
        pair, half = divmod(c, 2)
        kp = k3[:, pair * LANES:(pair + 1) * LANES]
        keep = lo if half == 0 else jnp.logical_not(lo)
        qs = []
        for g in range(group):
            t = pair * group + g
            qt = q_ref[0, :, t * LANES:(t + 1) * LANES]
            qs.append(jnp.where(keep, qt, jnp.zeros_like(qt)))
        qq = jnp.concatenate(qs, axis=0)
        s = lax.dot_general(qq, kp, NT_DIMS, preferred_element_type=F32)
        s_scr[c] = s + bias_ref[c * group:(c + 1) * group].reshape(group * blk, 3 * blk) + pen

    for pair in range(SW_KV_HEADS // 2):
        vp = v3[:, pair * LANES:(pair + 1) * LANES]
        per_half = []
        for half in range(2):
            c = 2 * pair + half
            sink = jnp.concatenate(
                [jnp.full((blk, LANES), sink_ref[c * group + g], F32) for g in range(group)], axis=0)
            per_half.append(_softmax_pv_mxu_sum(s_scr[c], vp, sink))
        for g in range(group):
            t = pair * group + g
            o = jnp.where(lo, per_half[0][g * blk:(g + 1) * blk], per_half[1][g * blk:(g + 1) * blk])
            o_ref[0, :, t * LANES:(t + 1) * LANES] = o.astype(o_ref.dtype)


def sw_attention(q, k, v, bias, sink):
    b, t, c = q.shape
    nb = t // SW_BLOCK
    kvw = k.shape[2]
    grid_spec = pltpu.PrefetchScalarGridSpec(
        num_scalar_prefetch=1,
        grid=(b, nb),
        in_specs=[pl.BlockSpec((1, SW_BLOCK, c), lambda i, j, s: (i, j, 0)),
                  pl.BlockSpec((1, t, kvw), lambda i, j, s: (i, 0, 0)),
                  pl.BlockSpec((1, t, kvw), lambda i, j, s: (i, 0, 0)),
                  pl.BlockSpec(bias.shape, lambda i, j, s: (0, 0, 0))],
        out_specs=pl.BlockSpec((1, SW_BLOCK, c), lambda i, j, s: (i, j, 0)),
        scratch_shapes=[pltpu.VMEM((SW_KV_HEADS, (SW_HEADS // SW_KV_HEADS) * SW_BLOCK, 3 * SW_BLOCK), F32)],
    )
    return pl.pallas_call(
        functools.partial(_sw_kernel, nb=nb),
        grid_spec=grid_spec,
        out_shape=jax.ShapeDtypeStruct((b, t, c), BF16),
        compiler_params=_params("parallel", "arbitrary"),
        name="sw_attention",
    )(sink, q, k, v, bias)


def _t5_bucket(rel):
    half = T5_BUCKETS // 2
    max_exact = half // 2
    n = jnp.abs(rel)
    nf = jnp.maximum(n, 1).astype(F32)
    large = max_exact + (jnp.log(nf / max_exact) / math.log(T5_MAX_DIST / max_exact)
                         * (half - max_exact)).astype(jnp.int32)
    large = jnp.minimum(large, half - 1)
    return jnp.where(rel > 0, half, 0) + jnp.where(n < max_exact, n, large)


def sw_bias_table(t5_table):
    span = SW_BLOCK + 2 * SW_WINDOW
    rel = np.arange(span)[None, :] - SW_WINDOW - np.arange(SW_BLOCK)[:, None]
    bucket = _t5_bucket(jnp.asarray(rel, dtype=jnp.int32))[None]
    table = t5_table.astype(F32).T[:, :, None, None]
    bias = jnp.zeros((t5_table.shape[1], SW_BLOCK, span), F32)
    for b in range(T5_BUCKETS):
        bias = jnp.where(bucket == b, table[:, b], bias)
    return jnp.where(jnp.asarray(np.abs(rel) <= SW_WINDOW)[None], bias, NEG_INF)


def sw_pair_heads(w, axis):
    group = SW_HEADS // SW_KV_HEADS
    shape = w.shape
    split = shape[:axis] + (SW_KV_HEADS // 2, 2, group, SW_HEAD_DIM) + shape[axis + 1:]
    order = list(range(len(split)))
    order[axis + 1], order[axis + 2] = axis + 2, axis + 1
    return w.reshape(split).transpose(order).reshape(shape)


def _mx_kernel(q_ref, mk_ref, mv_ref, o_ref, *, scale):
    hd = q_ref.shape[2] // MX_HEADS
    for h in range(MX_HEADS):
        sl = slice(h * hd, (h + 1) * hd)
        s = lax.dot_general(q_ref[0, :, sl], mk_ref[0, :, sl], NT_DIMS, preferred_element_type=F32) * scale
        o_ref[0, :, sl] = _softmax_pv(s, mv_ref[0, :, sl]).astype(o_ref.dtype)


def mx_attention(q, mkv, tq=512):
    b, t, c = q.shape
    m = mkv.shape[1]
    tq = min(tq, t)
    return pl.pallas_call(
        functools.partial(_mx_kernel, scale=float((c // MX_HEADS) ** -0.5)),
        grid=(b, t // tq),
        in_specs=[pl.BlockSpec((1, tq, c), lambda i, j: (i, j, 0)),
                  pl.BlockSpec((1, m, c), lambda i, j: (i, 0, 0)),
                  pl.BlockSpec((1, m, c), lambda i, j: (i, 0, 1))],
        out_specs=pl.BlockSpec((1, tq, c), lambda i, j: (i, j, 0)),
        out_shape=jax.ShapeDtypeStruct((b, t, c), BF16),
        compiler_params=_params("parallel", "arbitrary"),
        name="mx_attention",
    )(q, mkv, mkv)


def _merge_kernel(x_ref, ona_ref, osw_ref, omx_ref, gmix_ref, wg_ref, wna_ref, wsw_ref, wmx_ref, wout_ref,
                  gffn_ref, wrh_ref, wrl_ref, z_ref, aff_ref, h_scr):
    j = pl.program_id(1)
    nstrip = z_ref.shape[1] // 3
    strips = range(nstrip)

    @pl.when(j == 0)
    def _():
        x = x_ref[...]
        h_scr[...] = _rms(x, gmix_ref[...]).astype(BF16)
        for c in strips:
            z_ref[:, c] = _to_row_tiles(x[:, c * LANES:(c + 1) * LANES])

    h = h_scr[...]
    merged = None
    for o_ref, w_ref, b in ((ona_ref, wna_ref, 0), (osw_ref, wsw_ref, 1), (omx_ref, wmx_ref, 2)):
        gate = jax.nn.sigmoid(jnp.dot(h, wg_ref[b], preferred_element_type=F32))
        term = gate * jnp.dot(o_ref[...], w_ref[...], preferred_element_type=F32)
        merged = term if merged is None else merged + term
    part = jnp.dot(merged.astype(BF16), wout_ref[...], preferred_element_type=F32)
    for c in strips:
        z_ref[:, c] += _to_row_tiles(part[:, c * LANES:(c + 1) * LANES])

    @pl.when(j == pl.num_programs(1) - 1)
    def _():
        x2 = jnp.concatenate([_from_row_tiles(z_ref[:, c]) for c in strips], axis=1)
        h2 = _rms(x2, gffn_ref[...])
        for c in strips:
            z_ref[:, nstrip + c] = _to_row_tiles(h2[:, c * LANES:(c + 1) * LANES])
            z_ref[:, 2 * nstrip + c] = jnp.zeros((z_ref.shape[0], SUBLANES, LANES), F32)
        hi = h2.astype(BF16)
        lo = (h2 - hi.astype(F32)).astype(BF16)
        wh = wrh_ref[...]
        logits = (lax.dot_general(wh, hi, NT_DIMS, preferred_element_type=F32)
                  + lax.dot_general(wh, lo, NT_DIMS, preferred_element_type=F32)
                  + lax.dot_general(wrl_ref[...], hi, NT_DIMS, preferred_element_type=F32))
        m = jnp.max(logits, axis=0, keepdims=True)
        e = jnp.exp(logits - m)
        aff = e / jnp.sum(e, axis=0, keepdims=True)
        for c in range(aff_ref.shape[0]):
            aff_ref[c] = aff[:, c * LANES:(c + 1) * LANES]


def merge(x, o_na, o_sw, o_mx, g_mix, w_gate, w_na_o, w_sw_o, w_mx_o, w_out, g_ffn, wr_hi, wr_lo,
          tm=512, tn=256):
    n, d = x.shape
    tm = min(tm, n)
    assert n % tm == 0 and tm % LANES == 0 and d % LANES == 0
    ne = wr_hi.shape[0]
    row = lambda i, j: (i, 0)
    return pl.pallas_call(
        _merge_kernel,
        grid=(n // tm, d // tn),
        in_specs=[pl.BlockSpec((tm, d), row),
                  pl.BlockSpec((tm, o_na.shape[1]), row),
                  pl.BlockSpec((tm, o_sw.shape[1]), row),
                  pl.BlockSpec((tm, o_mx.shape[1]), row),
                  pl.BlockSpec((1, d), lambda i, j: (0, 0)),
                  pl.BlockSpec((N_BRANCHES, d, tn), lambda i, j: (0, 0, j)),
                  pl.BlockSpec((w_na_o.shape[0], tn), lambda i, j: (0, j)),
                  pl.BlockSpec((w_sw_o.shape[0], tn), lambda i, j: (0, j)),
                  pl.BlockSpec((w_mx_o.shape[0], tn), lambda i, j: (0, j)),
                  pl.BlockSpec((tn, d), lambda i, j: (j, 0)),
                  pl.BlockSpec((1, d), lambda i, j: (0, 0)),
                  pl.BlockSpec((ne, d), lambda i, j: (0, 0)),
                  pl.BlockSpec((ne, d), lambda i, j: (0, 0))],
        out_specs=[pl.BlockSpec(_row_tile_shape(tm, 3 * d), lambda i, j: (i, 0, 0, 0)),
                   pl.BlockSpec((tm // LANES, ne, LANES), lambda i, j: (i, 0, 0))],
        out_shape=[jax.ShapeDtypeStruct(_row_tile_shape(n, 3 * d), F32),
                   jax.ShapeDtypeStruct((n // LANES, ne, LANES), F32)],
        scratch_shapes=[pltpu.VMEM((tm, d), BF16)],
        compiler_params=_params("parallel", "arbitrary"),
        name="merge",
    )(x, o_na, o_sw, o_mx, g_mix, w_gate, w_na_o, w_sw_o, w_mx_o, w_out, g_ffn, wr_hi, wr_lo)


NA_WIDTH = NA_HEADS * NA_HEAD_DIM
SW_WIDTH = SW_HEADS * SW_HEAD_DIM
SW_KV_WIDTH = SW_KV_HEADS * SW_HEAD_DIM
QKV_WIDTHS = (NA_WIDTH, NA_WIDTH, NA_WIDTH, SW_WIDTH, SW_KV_WIDTH, SW_KV_WIDTH)


def prep_weights(g_mix, g_mem, w_in, w_mem_kv, na_rpb, t5_table, sw_sink, w_na_o, w_sw_o, w_mx_o, w_out,
                 g_ffn, w_router):
    d = w_in.shape[0]
    mx_width = w_mx_o.shape[0]
    widths = QKV_WIDTHS + (mx_width,)
    offs = np.concatenate([[0], np.cumsum(widths)])
    cols = [w_in[:, offs[i]:offs[i + 1]] for i in range(len(widths))]
    cols[3] = sw_pair_heads(cols[3], axis=1)
    w_gate = w_in[:, offs[-1]:].reshape(d, N_BRANCHES, d).transpose(1, 0, 2)
    wr = w_router.T.astype(F32)
    wr_hi = wr.astype(BF16)
    return dict(
        g_mix=g_mix.reshape(1, d).astype(F32), g_mem=g_mem.reshape(1, d).astype(F32),
        g_ffn=g_ffn.reshape(1, d).astype(F32),
        w_qkv=jnp.concatenate(cols, axis=1).astype(BF16), qkv_widths=widths,
        qkv_scales=(NA_HEAD_DIM ** -0.5, 1.0, 1.0, SW_HEAD_DIM ** -0.5, 1.0, 1.0, 1.0),
        w_mem_kv=w_mem_kv.astype(BF16),
        na_bias=na_bias_table(na_rpb), sw_bias=sw_bias_table(t5_table), sw_sink=sw_sink.astype(F32),
        w_gate=w_gate.astype(BF16), w_na_o=w_na_o.astype(BF16), w_sw_o=sw_pair_heads(w_sw_o, axis=0).astype(BF16),
        w_mx_o=w_mx_o.astype(BF16), w_out=w_out.astype(BF16),
        wr_hi=wr_hi, wr_lo=(wr - wr_hi.astype(F32)).astype(BF16))


def layer_front(x, mem, w, debug=False):
    b, t, d = x.shape
    m = mem.shape[1]
    xf = x.reshape(b * t, d)
    na_q, na_k, na_v, sw_q, sw_k, sw_v, mx_q = norm_proj(xf, w["g_mix"], w["w_qkv"], w["qkv_widths"],
                                                         w["qkv_scales"], tm=512)
    (mkv,) = norm_proj(mem.reshape(b * m, d), w["g_mem"], w["w_mem_kv"], (w["w_mem_kv"].shape[1],), (1.0,),
                       tm=512)
    r3 = lambda a: a.reshape(b, t, a.shape[1])
    o_na = na_attention(r3(na_q), r3(na_k), r3(na_v), w["na_bias"])
    o_sw = sw_attention(r3(sw_q), r3(sw_k), r3(sw_v), w["sw_bias"], w["sw_sink"])
    o_mx = mx_attention(r3(mx_q), mkv.reshape(b, m, mkv.shape[1]))
    f2 = lambda a: a.reshape(b * t, a.shape[2])
    z, aff = merge(xf, f2(o_na), f2(o_sw), f2(o_mx), w["g_mix"], w["w_gate"], w["w_na_o"], w["w_sw_o"],
                   w["w_mx_o"], w["w_out"], w["g_ffn"], w["wr_hi"], w["wr_lo"])
    if debug:
        return o_na, o_sw, o_mx, z, aff
    return z, aff


def _select_kernel(aff_ref, sel_ref, pos_ref, off_ref, *, cap):
    nc, ne, _ = aff_ref.shape
    bits = lax.bitcast_convert_type(aff_ref[...], I32)
    tok = (lax.broadcasted_iota(I32, bits.shape, 0) * LANES + lax.broadcasted_iota(I32, bits.shape, 2))

    def count(flags):
        return jnp.sum(flags, axis=(0, 2), keepdims=True)

    def value_step(i, prefix):
        cand = prefix | lax.shift_left(jnp.int32(1), 30 - i)
        cnt = count(jnp.where(bits >= cand, 1.0, 0.0))
        return jnp.where(cnt >= cap, cand, prefix)

    tau = lax.fori_loop(0, 31, value_step, jnp.zeros((1, ne, 1), I32))
    gt = bits > tau
    eq = bits == tau
    need = cap - count(jnp.where(gt, 1.0, 0.0))

    def index_step(i, last):
        cand = last | lax.shift_left(jnp.int32(1), 15 - i)
        cnt = count(jnp.where(eq, jnp.where(tok < cand, 1.0, 0.0), 0.0))
        return jnp.where(cnt < need, cand, last)

    last = lax.fori_loop(0, 16, index_step, jnp.zeros((1, ne, 1), I32))
    sel_ref[...] = jnp.where(gt, 1.0, jnp.where(eq, jnp.where(tok <= last, 1.0, 0.0), 0.0))

    upper = (lax.broadcasted_iota(I32, (LANES, LANES), 0) <= lax.broadcasted_iota(I32, (LANES, LANES), 1))
    upper = jnp.where(upper, 1.0, 0.0).astype(BF16)

    def chunk_step(c, off):
        s = sel_ref[c]
        cum = jnp.dot(s.astype(BF16), upper, preferred_element_type=F32)
        pos_ref[c] = off + cum - s
        off_ref[c] = jnp.broadcast_to(off, s.shape)
        return off + cum[:, LANES - 1:LANES]

    lax.fori_loop(0, nc, chunk_step, jnp.zeros((ne, 1), F32), unroll=CHUNK_UNROLL)


def select(aff, cap):
    nc, ne, _ = aff.shape
    assert nc * LANES <= 65536
    shape = jax.ShapeDtypeStruct(aff.shape, F32)
    return pl.pallas_call(
        functools.partial(_select_kernel, cap=float(cap)),
        out_shape=[shape, shape, shape],
        compiler_params=pltpu.CompilerParams(vmem_limit_bytes=VMEM_LIMIT),
        name="select",
    )(aff)


TOK_COL, CHUNK_COL, HI_COL = 0, 1, 2


def _compact_kernel(choff_ref, sel_ref, pos_ref, aff_ref, o_ref, tv_scr):
    e = pl.program_id(0)
    nc, ne, _ = aff_ref.shape
    win = 2 * LANES
    col = lax.broadcasted_iota(I32, (LANES, LANES), 1)
    row = lax.broadcasted_iota(I32, (LANES, LANES), 0)

    @pl.when(e == 0)
    def _():
        ecol = lax.broadcasted_iota(I32, (ne, LANES), 1) - lax.broadcasted_iota(I32, (ne, LANES), 0)
        place = [jnp.where(ecol == HI_COL + k * ne, 1.0, 0.0).astype(BF16) for k in range(3)]

        def build(c, carry):
            a = aff_ref[c]
            hi = a.astype(BF16)
            r1 = a - hi.astype(F32)
            mid = r1.astype(BF16)
            lo = (r1 - mid.astype(F32)).astype(BF16)
            rec = (lax.dot_general(hi, place[0], TN_DIMS, preferred_element_type=F32)
                   + lax.dot_general(mid, place[1], TN_DIMS, preferred_element_type=F32)
                   + lax.dot_general(lo, place[2], TN_DIMS, preferred_element_type=F32))
            rec = rec + jnp.where(col == TOK_COL, row, jnp.where(col == CHUNK_COL, c, 0)).astype(F32)
            tv_scr[c] = rec.astype(BF16)
            return carry

        lax.fori_loop(0, nc, build, 0, unroll=CHUNK_UNROLL)

    o_ref[...] = jnp.zeros_like(o_ref)
    slot0 = lax.broadcasted_iota(I32, (win, LANES), 0).astype(F32)

    def body(c, carry):
        off = choff_ref[c * ne + e]
        base = pl.multiple_of((off // LANES) * LANES, LANES)
        s = sel_ref[c, pl.ds(e, 1), :]
        p = pos_ref[c, pl.ds(e, 1), :] - base.astype(F32)
        onehot = jnp.where(s > 0.0, jnp.where(slot0 == p, 1.0, 0.0), 0.0).astype(BF16)
        o_ref[0, pl.ds(base, win), :] += jnp.dot(onehot, tv_scr[c], preferred_element_type=F32)
        return carry

    lax.fori_loop(0, nc, body, 0, unroll=CHUNK_UNROLL)


def compact(sel, pos, choff, aff, cap):
    nc, ne, _ = aff.shape
    assert nc <= 256 and HI_COL + 3 * ne <= LANES and cap % LANES == 0
    rows = cap + 2 * LANES
    full = pl.BlockSpec(aff.shape, lambda e, s: (0, 0, 0))
    rec = pl.pallas_call(
        _compact_kernel,
        grid_spec=pltpu.PrefetchScalarGridSpec(
            num_scalar_prefetch=1, grid=(ne,),
            in_specs=[full, full, full],
            out_specs=pl.BlockSpec((1, rows, LANES), lambda e, s: (e, 0, 0)),
            scratch_shapes=[pltpu.VMEM((nc, LANES, LANES), BF16)]),
        out_shape=jax.ShapeDtypeStruct((ne, rows, LANES), F32),
        compiler_params=_params("arbitrary"),
        name="compact",
    )(choff, sel, pos, aff)
    idx = (rec[:, :cap, CHUNK_COL] * LANES + rec[:, :cap, TOK_COL]).astype(I32)
    return idx, rec


def _expert_kernel(idx_prev, idx_cur, idx_next, rec_ref, zin_ref, wg_ref, wu_ref, wd_ref, z_ref,
                   buf, xs, acc, gsem, ssem, *, ts, nf, s_tiles, ne):
    del zin_ref
    sub = z_ref.shape[1] // 3
    t = pl.program_id(0)
    f = pl.program_id(1)
    last_t = pl.num_programs(0) - 1
    rps = ts // nf
    slot, nslot, pslot = t % 3, (t + 1) % 3, (t + 2) % 3
    odd = (t // s_tiles) % 2
    nodd = (jnp.minimum(t + 1, last_t) // s_tiles) % 2
    podd = (jnp.maximum(t - 1, 0) // s_tiles) % 2

    def rows(ref, row, first, count):
        return ref.at[row >> 3, pl.ds(first, count), pl.ds(row & (SUBLANES - 1), 1), :]

    def gather(token, ra, rs, to_slot, is_odd):
        return pltpu.make_async_copy(rows(z_ref, token, is_odd * sub, 2 * sub),
                                     buf.at[to_slot, ra, :, pl.ds(rs, 1), :], gsem.at[to_slot])

    def scatter(token, ra, rs, from_slot, is_odd):
        return pltpu.make_async_copy(buf.at[from_slot, ra, pl.ds(is_odd * sub, sub), pl.ds(rs, 1), :],
                                     rows(z_ref, token, is_odd * 2 * sub, sub), ssem.at[from_slot])

    def wait_gather(s):
        pltpu.make_async_copy(z_ref.at[pl.ds(0, ts // SUBLANES), pl.ds(0, 2 * sub)], buf.at[s], gsem.at[s]).wait()

    def wait_scatter(s):
        pltpu.make_async_copy(buf.at[s, :, pl.ds(0, sub)], z_ref.at[pl.ds(0, ts // SUBLANES), pl.ds(0, sub)],
                              ssem.at[s]).wait()

    def for_rows(fn):
        def body(r, carry):
            fn(idx_cur[0, 0, r], r >> 3, r & (SUBLANES - 1))
            return carry
        lax.fori_loop(0, ts, body, 0)

    @pl.when(f == 0)
    def _():
        @pl.when(t == 0)
        def _():
            def start(token, ra, rs):
                gather(token, ra, rs, 0, 0).start()
                gather(token, ra, rs, 2, 0).start()
            for_rows(start)
            wait_gather(2)

        wait_gather(slot)
        h2_first = (1 - odd) * sub
        for c in range(sub):
            xs[:, c * LANES:(c + 1) * LANES] = _from_row_tiles(buf[slot, :, h2_first + c]).astype(BF16)

    for k in range(rps):
        r = f * rps + k
        ra, rs = f * (rps // SUBLANES) + k // SUBLANES, k % SUBLANES
        gather(idx_next[0, 0, r], ra, rs, nslot, nodd).start()
        scatter(idx_prev[0, 0, r], ra, rs, pslot, podd).start()

    x = xs[...]
    a = jnp.dot(x, wg_ref[0], preferred_element_type=F32)
    b = jnp.dot(x, wu_ref[0], preferred_element_type=F32)
    hm = (jax.nn.silu(a) * b).astype(BF16)
    part = jnp.dot(hm, wd_ref[0], preferred_element_type=F32)

    @pl.when(f == 0)
    def _():
        acc[...] = part

    @pl.when(f > 0)
    def _():
        acc[...] += part

    @pl.when(f == nf - 1)
    def _():
        lane = lax.broadcasted_iota(I32, (1, LANES), 1) - (HI_COL + t // s_tiles)
        own = (lane == 0) | (lane == ne) | (lane == 2 * ne)
        contrib = acc[...] * jnp.sum(jnp.where(own, rec_ref[0], 0.0), axis=1, keepdims=True)
        y_first = odd * sub
        for c in range(sub):
            buf[slot, :, y_first + c] += _to_row_tiles(contrib[:, c * LANES:(c + 1) * LANES])
        wait_scatter(pslot)

        @pl.when(t == last_t)
        def _():
            wait_gather(nslot)
            for_rows(lambda token, ra, rs: scatter(token, ra, rs, slot, odd).start())
            wait_scatter(slot)


def expert_ffn(idx, rec, z, w_gate, w_up, w_down, tf=512):
    ne, cap = idx.shape
    d = z.shape[1] // 3 * LANES
    ff = w_gate.shape[2]
    ts = EXPERT_TS
    tf = min(tf, ff)
    nf = ff // tf
    s_tiles = cap // ts
    assert cap % ts == 0 and s_tiles >= 2 and ts % nf == 0
    nt = ne * s_tiles
    idx3 = idx.reshape(nt, 1, ts)
    smem_tile = lambda shift: pl.BlockSpec(
        (1, 1, ts), lambda t, f: (jnp.clip(t + shift, 0, nt - 1), 0, 0), memory_space=pltpu.SMEM)
    return pl.pallas_call(
        functools.partial(_expert_kernel, ts=ts, nf=nf, s_tiles=s_tiles, ne=ne),
        grid=(nt, nf),
        in_specs=[smem_tile(-1), smem_tile(0), smem_tile(1),
                  pl.BlockSpec((1, ts, LANES), lambda t, f: (t // s_tiles, t % s_tiles, 0)),
                  pl.BlockSpec(memory_space=pl.ANY),
                  pl.BlockSpec((1, d, tf), lambda t, f: (t // s_tiles, 0, f)),
                  pl.BlockSpec((1, d, tf), lambda t, f: (t // s_tiles, 0, f)),
                  pl.BlockSpec((1, tf, d), lambda t, f: (t // s_tiles, f, 0))],
        out_specs=pl.BlockSpec(memory_space=pl.ANY),
        out_shape=jax.ShapeDtypeStruct(z.shape, F32),
        scratch_shapes=[pltpu.VMEM((3,) + _row_tile_shape(ts, 2 * d), F32), pltpu.VMEM((ts, d), BF16),
                        pltpu.VMEM((ts, d), F32), pltpu.SemaphoreType.DMA((3,)), pltpu.SemaphoreType.DMA((3,))],
        input_output_aliases={4: 0},
        compiler_params=_params("arbitrary", "arbitrary"),
        name="expert_ffn",
    )(idx3, idx3, idx3, rec, z, w_gate, w_up, w_down)


def _final_norm_kernel(y0_ref, y1_ref, g_ref, o_ref):
    x = jnp.concatenate([_from_row_tiles(y0_ref[:, c] + y1_ref[:, c]) for c in range(y0_ref.shape[1])], axis=1)
    o_ref[...] = _rms(x, g_ref[...])


def final_norm(z, g, tm=512):
    n, d = z.shape[0] * SUBLANES, z.shape[1] // 3 * LANES
    tm = min(tm, n)
    assert n % tm == 0
    return pl.pallas_call(
        _final_norm_kernel,
        grid=(n // tm,),
        in_specs=[pl.BlockSpec(_row_tile_shape(tm, d), lambda i: (i, 0, 0, 0)),
                  pl.BlockSpec(_row_tile_shape(tm, d), lambda i: (i, 2, 0, 0)),
                  pl.BlockSpec((1, d), lambda i: (0, 0))],
        out_specs=pl.BlockSpec((tm, d), lambda i: (i, 0)),
        out_shape=jax.ShapeDtypeStruct((n, d), F32),
        compiler_params=_params("parallel"),
        name="final_norm",
    )(z, z, g)


def moe_and_norm(z, aff, w_e_gate, w_e_up, w_e_down, g_final):
    n = z.shape[0] * SUBLANES
    cap = EC_CAPACITY * n // N_EXPERTS
    sel, pos, off = select(aff, cap)
    choff = off[:, :, 0].astype(I32).reshape(-1)
    idx, rec = compact(sel, pos, choff, aff, cap)
    z = expert_ffn(idx, rec, z, w_e_gate, w_e_up, w_e_down)
    return final_norm(z, g_final)


def encoder_group(x, mem, w, experts, g_final):
    b, t, d = x.shape
    z, aff = layer_front(x, mem, w)
    return moe_and_norm(z, aff, *experts, g_final).reshape(b, t, d)


def kernel(x_prompt, x_sample, mem_prompt, mem_sample, g_mix, g_mem, w_in, w_mem_kv, na_rpb, t5_table, sw_sink,
           w_na_o, w_sw_o, w_mx_o, w_out, g_ffn, w_router, w_e_gate, w_e_up, w_e_down, g_final):
    assert g_mix.shape[0] == 1, "single-layer trunk"
    w = prep_weights(g_mix[0], g_mem[0], w_in[0], w_mem_kv[0], na_rpb[0], t5_table, sw_sink[0], w_na_o[0],
                     w_sw_o[0], w_mx_o[0], w_out[0], g_ffn[0], w_router[0])
    experts = (w_e_gate[0].astype(BF16), w_e_up[0].astype(BF16), w_e_down[0].astype(BF16))
    gf = g_final.reshape(1, -1).astype(F32)
    y_prompt = encoder_group(x_prompt, mem_prompt, w, experts, gf)
    y_sample = encoder_group(x_sample, mem_sample, w, experts, gf)
    return (y_prompt, y_sample)
```

```python
import functools
import math

import numpy as np
import jax
import jax.numpy as jnp
from jax import lax
from jax.experimental import pallas as pl
from jax.experimental.pallas import tpu as pltpu

F32 = jnp.float32
BF16 = jnp.bfloat16
I32 = jnp.int32

RMS_EPS = 1e-6
NEG_INF = -1e30

GRID_W = 64
NA_HEADS = 8
NA_HEAD_DIM = 64
NA_KR = 8
NA_KC = 16
SW_HEADS = 16
SW_KV_HEADS = 4
SW_HEAD_DIM = 64
SW_WINDOW = 128
SW_BLOCK = 128
MX_HEADS = 4
T5_BUCKETS = 32
T5_MAX_DIST = 128
N_BRANCHES = 3
N_EXPERTS = 16
EC_CAPACITY = 2

LANES = 128
SUBLANES = 8
V7X_VMEM_BYTES = 64 * 1024 * 1024
VMEM_LIMIT = V7X_VMEM_BYTES * 7 // 8

CHUNK_UNROLL = 8
EXPERT_TS = 512

NT_DIMS = (((1,), (1,)), ((), ()))
TN_DIMS = (((0,), (0,)), ((), ()))


def _params(*sem):
    return pltpu.CompilerParams(dimension_semantics=sem, vmem_limit_bytes=VMEM_LIMIT)


def _rms(x, g):
    return x * lax.rsqrt(jnp.mean(x * x, axis=-1, keepdims=True) + RMS_EPS) * g


def _to_row_tiles(strip):
    return strip.reshape(strip.shape[0] // SUBLANES, SUBLANES, LANES)


def _from_row_tiles(tiles):
    return tiles.reshape(tiles.shape[0] * SUBLANES, LANES)


def _row_tile_shape(rows, d):
    return (rows // SUBLANES, d // LANES, SUBLANES, LANES)


def _norm_proj_kernel(x_ref, g_ref, w_ref, *o_refs, scales):
    h = _rms(x_ref[...], g_ref[...]).astype(BF16)
    off = 0
    for o_ref, sc in zip(o_refs, scales):
        width = o_ref.shape[1]
        for c0 in range(0, width, 512):
            cw = min(512, width - c0)
            r = jnp.dot(h, w_ref[:, off + c0:off + c0 + cw], preferred_element_type=F32)
            if sc != 1.0:
                r = r * sc
            o_ref[:, c0:c0 + cw] = r.astype(o_ref.dtype)
        off += width


def norm_proj(x, g, w, widths, scales, tm):
    n, d = x.shape
    tm = min(tm, n)
    assert n % tm == 0
    return pl.pallas_call(
        functools.partial(_norm_proj_kernel, scales=tuple(scales)),
        grid=(n // tm,),
        in_specs=[pl.BlockSpec((tm, d), lambda i: (i, 0)),
                  pl.BlockSpec((1, d), lambda i: (0, 0)),
                  pl.BlockSpec(w.shape, lambda i: (0, 0))],
        out_specs=[pl.BlockSpec((tm, c), lambda i: (i, 0)) for c in widths],
        out_shape=[jax.ShapeDtypeStruct((n, c), BF16) for c in widths],
        compiler_params=_params("parallel"),
        name="norm_proj",
    )(x, g, w)


def _softmax_pv(s, v, extra_logit=None):
    m = jnp.max(s, axis=-1, keepdims=True)
    if extra_logit is not None:
        m = jnp.maximum(m, extra_logit)
    e = jnp.exp(s - m)
    den = jnp.sum(e, axis=-1, keepdims=True)
    if extra_logit is not None:
        den = den + jnp.exp(extra_logit - m)
    o = jnp.dot(e.astype(BF16), v, preferred_element_type=F32)
    return o / den


def _softmax_pv_mxu_sum(s, v, extra_logit):
    rows, keys = s.shape
    mb = jnp.maximum(jnp.broadcast_to(jnp.max(s, axis=-1, keepdims=True), (rows, LANES)), extra_logit)
    e = jnp.concatenate([jnp.exp(s[:, t * LANES:(t + 1) * LANES] - mb) for t in range(keys // LANES)],
                        axis=1).astype(BF16)
    den = jnp.dot(e, jnp.ones((keys, v.shape[1]), BF16), preferred_element_type=F32) + jnp.exp(extra_logit - mb)
    return jnp.dot(e, v, preferred_element_type=F32) / den


def _na_kernel(q_ref, k_ref, v_ref, bias_ref, o_ref, *, rows, rb):
    j = pl.program_id(1)
    lo = lax.broadcasted_iota(I32, (GRID_W, LANES), 1) < NA_HEAD_DIM
    nkeys = NA_KR * GRID_W

    def body(i, carry):
        r = j * rb + i
        rs = jnp.clip(r - NA_KR // 2, 0, rows - NA_KR)
        off = r - rs
        q = q_ref[0, pl.ds(pl.multiple_of(i * GRID_W, GRID_W), GRID_W), :]
        kk = k_ref[0, pl.ds(pl.multiple_of(rs * GRID_W, GRID_W), nkeys), :]
        vv = v_ref[0, pl.ds(pl.multiple_of(rs * GRID_W, GRID_W), nkeys), :]
        scores = []
        for p in range(NA_HEADS // 2):
            qp = q[:, p * LANES:(p + 1) * LANES]
            kp = kk[:, p * LANES:(p + 1) * LANES]
            zero = jnp.zeros_like(qp)
            q2 = jnp.concatenate([jnp.where(lo, qp, zero), jnp.where(lo, zero, qp)], axis=0)
            s = lax.dot_general(q2, kp, NT_DIMS, preferred_element_type=F32)
            scores.append(s + bias_ref[off, 2 * p:2 * p + 2].reshape(2 * GRID_W, nkeys))
        outs = []
        for p in range(NA_HEADS // 2):
            o2 = _softmax_pv(scores[p], vv[:, p * LANES:(p + 1) * LANES])
            outs.append(jnp.where(lo, o2[:GRID_W], o2[GRID_W:]))
        o_ref[0, pl.ds(pl.multiple_of(i * GRID_W, GRID_W), GRID_W), :] = (
            jnp.concatenate(outs, axis=1).astype(o_ref.dtype))
        return carry

    lax.fori_loop(0, rb, body, 0, unroll=2)


def na_attention(q, k, v, bias, rb=8):
    b, t, c = q.shape
    rows = t // GRID_W
    assert rows >= NA_KR and rows % rb == 0
    return pl.pallas_call(
        functools.partial(_na_kernel, rows=rows, rb=rb),
        grid=(b, rows // rb),
        in_specs=[pl.BlockSpec((1, rb * GRID_W, c), lambda i, j: (i, j, 0)),
                  pl.BlockSpec((1, t, c), lambda i, j: (i, 0, 0)),
                  pl.BlockSpec((1, t, c), lambda i, j: (i, 0, 0)),
                  pl.BlockSpec(bias.shape, lambda i, j: (0, 0, 0, 0))],
        out_specs=pl.BlockSpec((1, rb * GRID_W, c), lambda i, j: (i, j, 0)),
        out_shape=jax.ShapeDtypeStruct((b, t, c), BF16),
        compiler_params=_params("parallel", "arbitrary"),
        name="na_attention",
    )(q, k, v, bias)


def na_bias_table(rpb):
    col = np.arange(GRID_W)
    col_start = np.clip(col - NA_KC // 2, 0, GRID_W - NA_KC)
    in_win = (col[None, :] >= col_start[:, None]) & (col[None, :] < col_start[:, None] + NA_KC)
    dc = np.clip(col[None, :] - col[:, None] + NA_KC - 1, 0, 2 * NA_KC - 2)
    heads = rpb.shape[0]
    by_col = jnp.take(rpb.astype(F32), jnp.asarray(dc.reshape(-1)), axis=2)
    by_col = by_col.reshape(heads, 2 * NA_KR - 1, GRID_W, GRID_W)
    by_col = jnp.where(jnp.asarray(in_win)[None, None], by_col, NEG_INF)
    per_off = [by_col[:, NA_KR - 1 - off:2 * NA_KR - 1 - off].transpose(0, 2, 1, 3) for off in range(NA_KR)]
    return jnp.stack(per_off).reshape(NA_KR, heads, GRID_W, NA_KR * GRID_W)


def _sw_kernel(sink_ref, q_ref, k_ref, v_ref, bias_ref, o_ref, s_scr, *, nb):
    n = pl.program_id(1)
    blk = SW_BLOCK
    group = SW_HEADS // SW_KV_HEADS

    def rows_of(ref, c):
        return ref[0, pl.ds(pl.multiple_of(c * blk, blk), blk), :]

    cl = jnp.maximum(n - 1, 0)
    cr = jnp.minimum(n + 1, nb - 1)
    k3 = jnp.concatenate([rows_of(k_ref, cl), rows_of(k_ref, n), rows_of(k_ref, cr)], axis=0)
    v3 = jnp.concatenate([rows_of(v_ref, cl), rows_of(v_ref, n), rows_of(v_ref, cr)], axis=0)
    pen_l = jnp.where(n > 0, 0.0, NEG_INF).astype(F32)
    pen_r = jnp.where(n < nb - 1, 0.0, NEG_INF).astype(F32)
    key = lax.broadcasted_iota(I32, (1, 3 * blk), 1)
    pen = jnp.where(key < blk, pen_l, jnp.where(key >= 2 * blk, pen_r, 0.0))
    lo = lax.broadcasted_iota(I32, (blk, LANES), 1) < SW_HEAD_DIM

    for c in range(SW_KV_HEADS):
        pair, half = divmod(c, 2)
        kp = k3[:, pair * LANES:(pair + 1) * LANES]
        keep = lo if half == 0 else jnp.logical_not(lo)
        qs = []
        for g in range(group):
            t = pair * group + g
            qt = q_ref[0, :, t * LANES:(t + 1) * LANES]
            qs.append(jnp.where(keep, qt, jnp.zeros_like(qt)))
        qq = jnp.concatenate(qs, axis=0)
        s = lax.dot_general(qq, kp, NT_DIMS, preferred_element_type=F32)
        s_scr[c] = s + bias_ref[c * group:(c + 1) * group].reshape(group * blk, 3 * blk) + pen

    for pair in range(SW_KV_HEADS // 2):
        vp = v3[:, pair * LANES:(pair + 1) * LANES]
        per_half = []
        for half in range(2):
            c = 2 * pair + half
            sink = jnp.concatenate(
                [jnp.full((blk, LANES), sink_ref[c * group + g], F32) for g in range(group)], axis=0)
            per_half.append(_softmax_pv_mxu_sum(s_scr[c], vp, sink))
        for g in range(group):
            t = pair * group + g
            o = jnp.where(lo, per_half[0][g * blk:(g + 1) * blk], per_half[1][g * blk:(g + 1) * blk])
            o_ref[0, :, t * LANES:(t + 1) * LANES] = o.astype(o_ref.dtype)


def sw_attention(q, k, v, bias, sink):
    b, t, c = q.shape
    nb = t // SW_BLOCK
    kvw = k.shape[2]
    grid_spec = pltpu.PrefetchScalarGridSpec(
        num_scalar_prefetch=1,
        grid=(b, nb),
        in_specs=[pl.BlockSpec((1, SW_BLOCK, c), lambda i, j, s: (i, j, 0)),
                  pl.BlockSpec((1, t, kvw), lambda i, j, s: (i, 0, 0)),
                  pl.BlockSpec((1, t, kvw), lambda i, j, s: (i, 0, 0)),
                  pl.BlockSpec(bias.shape, lambda i, j, s: (0, 0, 0))],
        out_specs=pl.BlockSpec((1, SW_BLOCK, c), lambda i, j, s: (i, j, 0)),
        scratch_shapes=[pltpu.VMEM((SW_KV_HEADS, (SW_HEADS // SW_KV_HEADS) * SW_BLOCK, 3 * SW_BLOCK), F32)],
    )
    return pl.pallas_call(
        functools.partial(_sw_kernel, nb=nb),
        grid_spec=grid_spec,
        out_shape=jax.ShapeDtypeStruct((b, t, c), BF16),
        compiler_params=_params("parallel", "arbitrary"),
        name="sw_attention",
    )(sink, q, k, v, bias)


def _t5_bucket(rel):
    half = T5_BUCKETS // 2
    max_exact = half // 2
    n = jnp.abs(rel)
    nf = jnp.maximum(n, 1).astype(F32)
    large = max_exact + (jnp.log(nf / max_exact) / math.log(T5_MAX_DIST / max_exact)
                         * (half - max_exact)).astype(jnp.int32)
    large = jnp.minimum(large, half - 1)
    return jnp.where(rel > 0, half, 0) + jnp.where(n < max_exact, n, large)


def sw_bias_table(t5_table):
    span = SW_BLOCK + 2 * SW_WINDOW
    rel = np.arange(span)[None, :] - SW_WINDOW - np.arange(SW_BLOCK)[:, None]
    bucket = _t5_bucket(jnp.asarray(rel, dtype=jnp.int32))[None]
    table = t5_table.astype(F32).T[:, :, None, None]
    bias = jnp.zeros((t5_table.shape[1], SW_BLOCK, span), F32)
    for b in range(T5_BUCKETS):
        bias = jnp.where(bucket == b, table[:, b], bias)
    return jnp.where(jnp.asarray(np.abs(rel) <= SW_WINDOW)[None], bias, NEG_INF)


def sw_pair_heads(w, axis):
    group = SW_HEADS // SW_KV_HEADS
    shape = w.shape
    split = shape[:axis] + (SW_KV_HEADS // 2, 2, group, SW_HEAD_DIM) + shape[axis + 1:]
    order = list(range(len(split)))
    order[axis + 1], order[axis + 2] = axis + 2, axis + 1
    return w.reshape(split).transpose(order).reshape(shape)


def _mx_kernel(q_ref, mk_ref, mv_ref, o_ref, *, scale):
    hd = q_ref.shape[2] // MX_HEADS
    for h in range(MX_HEADS):
        sl = slice(h * hd, (h + 1) * hd)
        s = lax.dot_general(q_ref[0, :, sl], mk_ref[0, :, sl], NT_DIMS, preferred_element_type=F32) * scale
        o_ref[0, :, sl] = _softmax_pv(s, mv_ref[0, :, sl]).astype(o_ref.dtype)


def mx_attention(q, mkv, tq=512):
    b, t, c = q.shape
    m = mkv.shape[1]
    tq = min(tq, t)
    return pl.pallas_call(
        functools.partial(_mx_kernel, scale=float((c // MX_HEADS) ** -0.5)),
        grid=(b, t // tq),
        in_specs=[pl.BlockSpec((1, tq, c), lambda i, j: (i, j, 0)),
                  pl.BlockSpec((1, m, c), lambda i, j: (i, 0, 0)),
                  pl.BlockSpec((1, m, c), lambda i, j: (i, 0, 1))],
        out_specs=pl.BlockSpec((1, tq, c), lambda i, j: (i, j, 0)),
        out_shape=jax.ShapeDtypeStruct((b, t, c), BF16),
        compiler_params=_params("parallel", "arbitrary"),
        name="mx_attention",
    )(q, mkv, mkv)


def _merge_kernel(x_ref, ona_ref, osw_ref, omx_ref, gmix_ref, wg_ref, wna_ref, wsw_ref, wmx_ref, wout_ref,
                  gffn_ref, wrh_ref, wrl_ref, z_ref, aff_ref, h_scr):
    j = pl.program_id(1)
    nstrip = z_ref.shape[1] // 3
    strips = range(nstrip)

    @pl.when(j == 0)
    def _():
        x = x_ref[...]
        h_scr[...] = _rms(x, gmix_ref[...]).astype(BF16)
        for c in strips:
            z_ref[:, c] = _to_row_tiles(x[:, c * LANES:(c + 1) * LANES])

    h = h_scr[...]
    merged = None
    for o_ref, w_ref, b in ((ona_ref, wna_ref, 0), (osw_ref, wsw_ref, 1), (omx_ref, wmx_ref, 2)):
        gate = jax.nn.sigmoid(jnp.dot(h, wg_ref[b], preferred_element_type=F32))
        term = gate * jnp.dot(o_ref[...], w_ref[...], preferred_element_type=F32)
        merged = term if merged is None else merged + term
    part = jnp.dot(merged.astype(BF16), wout_ref[...], preferred_element_type=F32)
    for c in strips:
        z_ref[:, c] += _to_row_tiles(part[:, c * LANES:(c + 1) * LANES])

    @pl.when(j == pl.num_programs(1) - 1)
    def _():
        x2 = jnp.concatenate([_from_row_tiles(z_ref[:, c]) for c in strips], axis=1)
        h2 = _rms(x2, gffn_ref[...])
        for c in strips:
            z_ref[:, nstrip + c] = _to_row_tiles(h2[:, c * LANES:(c + 1) * LANES])
            z_ref[:, 2 * nstrip + c] = jnp.zeros((z_ref.shape[0], SUBLANES, LANES), F32)
        hi = h2.astype(BF16)
        lo = (h2 - hi.astype(F32)).astype(BF16)
        wh = wrh_ref[...]
        logits = (lax.dot_general(wh, hi, NT_DIMS, preferred_element_type=F32)
                  + lax.dot_general(wh, lo, NT_DIMS, preferred_element_type=F32)
                  + lax.dot_general(wrl_ref[...], hi, NT_DIMS, preferred_element_type=F32))
        m = jnp.max(logits, axis=0, keepdims=True)
        e = jnp.exp(logits - m)
        aff = e / jnp.sum(e, axis=0, keepdims=True)
        for c in range(aff_ref.shape[0]):
            aff_ref[c] = aff[:, c * LANES:(c + 1) * LANES]


def merge(x, o_na, o_sw, o_mx, g_mix, w_gate, w_na_o, w_sw_o, w_mx_o, w_out, g_ffn, wr_hi, wr_lo,
          tm=512, tn=256):
    n, d = x.shape
    tm = min(tm, n)
    assert n % tm == 0 and tm % LANES == 0 and d % LANES == 0
    ne = wr_hi.shape[0]
    row = lambda i, j: (i, 0)
    return pl.pallas_call(
        _merge_kernel,
        grid=(n // tm, d // tn),
        in_specs=[pl.BlockSpec((tm, d), row),
                  pl.BlockSpec((tm, o_na.shape[1]), row),
                  pl.BlockSpec((tm, o_sw.shape[1]), row),
                  pl.BlockSpec((tm, o_mx.shape[1]), row),
                  pl.BlockSpec((1, d), lambda i, j: (0, 0)),
                  pl.BlockSpec((N_BRANCHES, d, tn), lambda i, j: (0, 0, j)),
                  pl.BlockSpec((w_na_o.shape[0], tn), lambda i, j: (0, j)),
                  pl.BlockSpec((w_sw_o.shape[0], tn), lambda i, j: (0, j)),
                  pl.BlockSpec((w_mx_o.shape[0], tn), lambda i, j: (0, j)),
                  pl.BlockSpec((tn, d), lambda i, j: (j, 0)),
                  pl.BlockSpec((1, d), lambda i, j: (0, 0)),
                  pl.BlockSpec((ne, d), lambda i, j: (0, 0)),
                  pl.BlockSpec((ne, d), lambda i, j: (0, 0))],
        out_specs=[pl.BlockSpec(_row_tile_shape(tm, 3 * d), lambda i, j: (i, 0, 0, 0)),
                   pl.BlockSpec((tm // LANES, ne, LANES), lambda i, j: (i, 0, 0))],
        out_shape=[jax.ShapeDtypeStruct(_row_tile_shape(n, 3 * d), F32),
                   jax.ShapeDtypeStruct((n // LANES, ne, LANES), F32)],
        scratch_shapes=[pltpu.VMEM((tm, d), BF16)],
        compiler_params=_params("parallel", "arbitrary"),
        name="merge",
    )(x, o_na, o_sw, o_mx, g_mix, w_gate, w_na_o, w_sw_o, w_mx_o, w_out, g_ffn, wr_hi, wr_lo)


NA_WIDTH = NA_HEADS * NA_HEAD_DIM
SW_WIDTH = SW_HEADS * SW_HEAD_DIM
SW_KV_WIDTH = SW_KV_HEADS * SW_HEAD_DIM
QKV_WIDTHS = (NA_WIDTH, NA_WIDTH, NA_WIDTH, SW_WIDTH, SW_KV_WIDTH, SW_KV_WIDTH)


def prep_weights(g_mix, g_mem, w_in, w_mem_kv, na_rpb, t5_table, sw_sink, w_na_o, w_sw_o, w_mx_o, w_out,
                 g_ffn, w_router):
    d = w_in.shape[0]
    mx_width = w_mx_o.shape[0]
    widths = QKV_WIDTHS + (mx_width,)
    offs = np.concatenate([[0], np.cumsum(widths)])
    cols = [w_in[:, offs[i]:offs[i + 1]] for i in range(len(widths))]
    cols[3] = sw_pair_heads(cols[3], axis=1)
    w_gate = w_in[:, offs[-1]:].reshape(d, N_BRANCHES, d).transpose(1, 0, 2)
    wr = w_router.T.astype(F32)
    wr_hi = wr.astype(BF16)
    return dict(
        g_mix=g_mix.reshape(1, d).astype(F32), g_mem=g_mem.reshape(1, d).astype(F32),
        g_ffn=g_ffn.reshape(1, d).astype(F32),
        w_qkv=jnp.concatenate(cols, axis=1).astype(BF16), qkv_widths=widths,
        qkv_scales=(NA_HEAD_DIM ** -0.5, 1.0, 1.0, SW_HEAD_DIM ** -0.5, 1.0, 1.0, 1.0),
        w_mem_kv=w_mem_kv.astype(BF16),
        na_bias=na_bias_table(na_rpb), sw_bias=sw_bias_table(t5_table), sw_sink=sw_sink.astype(F32),
        w_gate=w_gate.astype(BF16), w_na_o=w_na_o.astype(BF16), w_sw_o=sw_pair_heads(w_sw_o, axis=0).astype(BF16),
        w_mx_o=w_mx_o.astype(BF16), w_out=w_out.astype(BF16),
        wr_hi=wr_hi, wr_lo=(wr - wr_hi.astype(F32)).astype(BF16))


def layer_front(x, mem, w, debug=False):
    b, t, d = x.shape
    m = mem.shape[1]
    xf = x.reshape(b * t, d)
    na_q, na_k, na_v, sw_q, sw_k, sw_v, mx_q = norm_proj(xf, w["g_mix"], w["w_qkv"], w["qkv_widths"],
                                                         w["qkv_scales"], tm=512)
    (mkv,) = norm_proj(mem.reshape(b * m, d), w["g_mem"], w["w_mem_kv"], (w["w_mem_kv"].shape[1],), (1.0,),
                       tm=512)
    r3 = lambda a: a.reshape(b, t, a.shape[1])
    o_na = na_attention(r3(na_q), r3(na_k), r3(na_v), w["na_bias"])
    o_sw = sw_attention(r3(sw_q), r3(sw_k), r3(sw_v), w["sw_bias"], w["sw_sink"])
    o_mx = mx_attention(r3(mx_q), mkv.reshape(b, m, mkv.shape[1]))
    f2 = lambda a: a.reshape(b * t, a.shape[2])
    z, aff = merge(xf, f2(o_na), f2(o_sw), f2(o_mx), w["g_mix"], w["w_gate"], w["w_na_o"], w["w_sw_o"],
                   w["w_mx_o"], w["w_out"], w["g_ffn"], w["wr_hi"], w["wr_lo"])
    if debug:
        return o_na, o_sw, o_mx, z, aff
    return z, aff


def _select_kernel(aff_ref, sel_ref, pos_ref, off_ref, *, cap):
    nc, ne, _ = aff_ref.shape
    bits = lax.bitcast_convert_type(aff_ref[...], I32)
    tok = (lax.broadcasted_iota(I32, bits.shape, 0) * LANES + lax.broadcasted_iota(I32, bits.shape, 2))

    def count(flags):
        return jnp.sum(flags, axis=(0, 2), keepdims=True)

    def value_step(i, prefix):
        cand = prefix | lax.shift_left(jnp.int32(1), 30 - i)
        cnt = count(jnp.where(bits >= cand, 1.0, 0.0))
        return jnp.where(cnt >= cap, cand, prefix)

    tau = lax.fori_loop(0, 31, value_step, jnp.zeros((1, ne, 1), I32))
    gt = bits > tau
    eq = bits == tau
    need = cap - count(jnp.where(gt, 1.0, 0.0))

    def index_step(i, last):
        cand = last | lax.shift_left(jnp.int32(1), 15 - i)
        cnt = count(jnp.where(eq, jnp.where(tok < cand, 1.0, 0.0), 0.0))
        return jnp.where(cnt < need, cand, last)

    last = lax.fori_loop(0, 16, index_step, jnp.zeros((1, ne, 1), I32))
    sel_ref[...] = jnp.where(gt, 1.0, jnp.where(eq, jnp.where(tok <= last, 1.0, 0.0), 0.0))

    upper = (lax.broadcasted_iota(I32, (LANES, LANES), 0) <= lax.broadcasted_iota(I32, (LANES, LANES), 1))
    upper = jnp.where(upper, 1.0, 0.0).astype(BF16)

    def chunk_step(c, off):
        s = sel_ref[c]
        cum = jnp.dot(s.astype(BF16), upper, preferred_element_type=F32)
        pos_ref[c] = off + cum - s
        off_ref[c] = jnp.broadcast_to(off, s.shape)
        return off + cum[:, LANES - 1:LANES]

    lax.fori_loop(0, nc, chunk_step, jnp.zeros((ne, 1), F32), unroll=CHUNK_UNROLL)


def select(aff, cap):
    nc, ne, _ = aff.shape
    assert nc * LANES <= 65536
    shape = jax.ShapeDtypeStruct(aff.shape, F32)
    return pl.pallas_call(
        functools.partial(_select_kernel, cap=float(cap)),
        out_shape=[shape, shape, shape],
        compiler_params=pltpu.CompilerParams(vmem_limit_bytes=VMEM_LIMIT),
        name="select",
    )(aff)


TOK_COL, CHUNK_COL, HI_COL = 0, 1, 2


def _compact_kernel(choff_ref, sel_ref, pos_ref, aff_ref, o_ref, tv_scr):
    e = pl.program_id(0)
    nc, ne, _ = aff_ref.shape
    win = 2 * LANES
    col = lax.broadcasted_iota(I32, (LANES, LANES), 1)
    row = lax.broadcasted_iota(I32, (LANES, LANES), 0)

    @pl.when(e == 0)
    def _():
        ecol = lax.broadcasted_iota(I32, (ne, LANES), 1) - lax.broadcasted_iota(I32, (ne, LANES), 0)
        place = [jnp.where(ecol == HI_COL + k * ne, 1.0, 0.0).astype(BF16) for k in range(3)]

        def build(c, carry):
            a = aff_ref[c]
            hi = a.astype(BF16)
            r1 = a - hi.astype(F32)
            mid = r1.astype(BF16)
            lo = (r1 - mid.astype(F32)).astype(BF16)
            rec = (lax.dot_general(hi, place[0], TN_DIMS, preferred_element_type=F32)
                   + lax.dot_general(mid, place[1], TN_DIMS, preferred_element_type=F32)
                   + lax.dot_general(lo, place[2], TN_DIMS, preferred_element_type=F32))
            rec = rec + jnp.where(col == TOK_COL, row, jnp.where(col == CHUNK_COL, c, 0)).astype(F32)
            tv_scr[c] = rec.astype(BF16)
            return carry

        lax.fori_loop(0, nc, build, 0, unroll=CHUNK_UNROLL)

    o_ref[...] = jnp.zeros_like(o_ref)
    slot0 = lax.broadcasted_iota(I32, (win, LANES), 0).astype(F32)

    def body(c, carry):
        off = choff_ref[c * ne + e]
        base = pl.multiple_of((off // LANES) * LANES, LANES)
        s = sel_ref[c, pl.ds(e, 1), :]
        p = pos_ref[c, pl.ds(e, 1), :] - base.astype(F32)
        onehot = jnp.where(s > 0.0, jnp.where(slot0 == p, 1.0, 0.0), 0.0).astype(BF16)
        o_ref[0, pl.ds(base, win), :] += jnp.dot(onehot, tv_scr[c], preferred_element_type=F32)
        return carry

    lax.fori_loop(0, nc, body, 0, unroll=CHUNK_UNROLL)


def compact(sel, pos, choff, aff, cap):
    nc, ne, _ = aff.shape
    assert nc <= 256 and HI_COL + 3 * ne <= LANES and cap % LANES == 0
    rows = cap + 2 * LANES
    full = pl.BlockSpec(aff.shape, lambda e, s: (0, 0, 0))
    rec = pl.pallas_call(
        _compact_kernel,
        grid_spec=pltpu.PrefetchScalarGridSpec(
            num_scalar_prefetch=1, grid=(ne,),
            in_specs=[full, full, full],
            out_specs=pl.BlockSpec((1, rows, LANES), lambda e, s: (e, 0, 0)),
            scratch_shapes=[pltpu.VMEM((nc, LANES, LANES), BF16)]),
        out_shape=jax.ShapeDtypeStruct((ne, rows, LANES), F32),
        compiler_params=_params("arbitrary"),
        name="compact",
    )(choff, sel, pos, aff)
    idx = (rec[:, :cap, CHUNK_COL] * LANES + rec[:, :cap, TOK_COL]).astype(I32)
    return idx, rec


def _expert_kernel(idx_prev, idx_cur, idx_next, rec_ref, zin_ref, wg_ref, wu_ref, wd_ref, z_ref,
                   buf, xs, acc, gsem, ssem, *, ts, nf, s_tiles, ne):
    del zin_ref
    sub = z_ref.shape[1] // 3
    t = pl.program_id(0)
    f = pl.program_id(1)
    last_t = pl.num_programs(0) - 1
    rps = ts // nf
    slot, nslot, pslot = t % 3, (t + 1) % 3, (t + 2) % 3
    odd = (t // s_tiles) % 2
    nodd = (jnp.minimum(t + 1, last_t) // s_tiles) % 2
    podd = (jnp.maximum(t - 1, 0) // s_tiles) % 2

    def rows(ref, row, first, count):
        return ref.at[row >> 3, pl.ds(first, count), pl.ds(row & (SUBLANES - 1), 1), :]

    def gather(token, ra, rs, to_slot, is_odd):
        return pltpu.make_async_copy(rows(z_ref, token, is_odd * sub, 2 * sub),
                                     buf.at[to_slot, ra, :, pl.ds(rs, 1), :], gsem.at[to_slot])

    def scatter(token, ra, rs, from_slot, is_odd):
        return pltpu.make_async_copy(buf.at[from_slot, ra, pl.ds(is_odd * sub, sub), pl.ds(rs, 1), :],
                                     rows(z_ref, token, is_odd * 2 * sub, sub), ssem.at[from_slot])

    def wait_gather(s):
        pltpu.make_async_copy(z_ref.at[pl.ds(0, ts // SUBLANES), pl.ds(0, 2 * sub)], buf.at[s], gsem.at[s]).wait()

    def wait_scatter(s):
        pltpu.make_async_copy(buf.at[s, :, pl.ds(0, sub)], z_ref.at[pl.ds(0, ts // SUBLANES), pl.ds(0, sub)],
                              ssem.at[s]).wait()

    def for_rows(fn):
        def body(r, carry):
            fn(idx_cur[0, 0, r], r >> 3, r & (SUBLANES - 1))
            return carry
        lax.fori_loop(0, ts, body, 0)

    @pl.when(f == 0)
    def _():
        @pl.when(t == 0)
        def _():
            def start(token, ra, rs):
                gather(token, ra, rs, 0, 0).start()
                gather(token, ra, rs, 2, 0).start()
            for_rows(start)
            wait_gather(2)

        acc[...] = jnp.zeros(acc.shape, F32)
        wait_gather(slot)
        h2_first = (1 - odd) * sub
        for c in range(sub):
            xs[:, c * LANES:(c + 1) * LANES] = _from_row_tiles(buf[slot, :, h2_first + c]).astype(BF16)

    for k in range(rps):
        r = f * rps + k
        ra, rs = f * (rps // SUBLANES) + k // SUBLANES, k % SUBLANES
        gather(idx_next[0, 0, r], ra, rs, nslot, nodd).start()
        scatter(idx_prev[0, 0, r], ra, rs, pslot, podd).start()

    x = xs[...]
    a = jnp.dot(x, wg_ref[0], preferred_element_type=F32)
    b = jnp.dot(x, wu_ref[0], preferred_element_type=F32)
    hm = (jax.nn.silu(a) * b).astype(BF16)
    acc[...] += jnp.dot(hm, wd_ref[0], preferred_element_type=F32)

    @pl.when(f == nf - 1)
    def _():
        lane = lax.broadcasted_iota(I32, (1, LANES), 1) - (HI_COL + t // s_tiles)
        own = (lane == 0) | (lane == ne) | (lane == 2 * ne)
        contrib = acc[...] * jnp.sum(jnp.where(own, rec_ref[0], 0.0), axis=1, keepdims=True)
        y_first = odd * sub
        for c in range(sub):
            buf[slot, :, y_first + c] += _to_row_tiles(contrib[:, c * LANES:(c + 1) * LANES])
        wait_scatter(pslot)

        @pl.when(t == last_t)
        def _():
            wait_gather(nslot)
            for_rows(lambda token, ra, rs: scatter(token, ra, rs, slot, odd).start())
            wait_scatter(slot)


def expert_ffn(idx, rec, z, w_gate, w_up, w_down, tf=512):
    ne, cap = idx.shape
    d = z.shape[1] // 3 * LANES
    ff = w_gate.shape[2]
    ts = EXPERT_TS
    tf = min(tf, ff)
    nf = ff // tf
    s_tiles = cap // ts
    assert cap % ts == 0 and s_tiles >= 2 and ts % nf == 0
    nt = ne * s_tiles
    idx3 = idx.reshape(nt, 1, ts)
    smem_tile = lambda shift: pl.BlockSpec(
        (1, 1, ts), lambda t, f: (jnp.clip(t + shift, 0, nt - 1), 0, 0), memory_space=pltpu.SMEM)
    return pl.pallas_call(
        functools.partial(_expert_kernel, ts=ts, nf=nf, s_tiles=s_tiles, ne=ne),
        grid=(nt, nf),
        in_specs=[smem_tile(-1), smem_tile(0), smem_tile(1),
                  pl.BlockSpec((1, ts, LANES), lambda t, f: (t // s_tiles, t % s_tiles, 0)),
                  pl.BlockSpec(memory_space=pl.ANY),
                  pl.BlockSpec((1, d, tf), lambda t, f: (t // s_tiles, 0, f)),
                  pl.BlockSpec((1, d, tf), lambda t, f: (t // s_tiles, 0, f)),
                  pl.BlockSpec((1, tf, d), lambda t, f: (t // s_tiles, f, 0))],
        out_specs=pl.BlockSpec(memory_space=pl.ANY),
        out_shape=jax.ShapeDtypeStruct(z.shape, F32),
        scratch_shapes=[pltpu.VMEM((3,) + _row_tile_shape(ts, 2 * d), F32), pltpu.VMEM((ts, d), BF16),
                        pltpu.VMEM((ts, d), F32), pltpu.SemaphoreType.DMA((3,)), pltpu.SemaphoreType.DMA((3,))],
        input_output_aliases={4: 0},
        compiler_params=_params("arbitrary", "arbitrary"),
        name="expert_ffn",
    )(idx3, idx3, idx3, rec, z, w_gate, w_up, w_down)


def _final_norm_kernel(y0_ref, y1_ref, g_ref, o_ref):
    x = jnp.concatenate([_from_row_tiles(y0_ref[:, c] + y1_ref[:, c]) for c in range(y0_ref.shape[1])], axis=1)
    o_ref[...] = _rms(x, g_ref[...])


def final_norm(z, g, tm=512):
    n, d = z.shape[0] * SUBLANES, z.shape[1] // 3 * LANES
    tm = min(tm, n)
    assert n % tm == 0
    return pl.pallas_call(
        _final_norm_kernel,
        grid=(n // tm,),
        in_specs=[pl.BlockSpec(_row_tile_shape(tm, d), lambda i: (i, 0, 0, 0)),
                  pl.BlockSpec(_row_tile_shape(tm, d), lambda i: (i, 2, 0, 0)),
                  pl.BlockSpec((1, d), lambda i: (0, 0))],
        out_specs=pl.BlockSpec((tm, d), lambda i: (i, 0)),
        out_shape=jax.ShapeDtypeStruct((n, d), F32),
        compiler_params=_params("parallel"),
        name="final_norm",
    )(z, z, g)


def moe_and_norm(z, aff, w_e_gate, w_e_up, w_e_down, g_final):
    n = z.shape[0] * SUBLANES
    cap = EC_CAPACITY * n // N_EXPERTS
    sel, pos, off = select(aff, cap)
    choff = off[:, :, 0].astype(I32).reshape(-1)
    idx, rec = compact(sel, pos, choff, aff, cap)
    z = expert_ffn(idx, rec, z, w_e_gate, w_e_up, w_e_down)
    return final_norm(z, g_final)


def encoder_group(x, mem, w, experts, g_final):
    b, t, d = x.shape
    z, aff = layer_front(x, mem, w)
    return moe_and_norm(z, aff, *experts, g_final).reshape(b, t, d)


def kernel(x_prompt, x_sample, mem_prompt, mem_sample, g_mix, g_mem, w_in, w_mem_kv, na_rpb, t5_table, sw_sink,
           w_na_o, w_sw_o, w_mx_o, w_out, g_ffn, w_router, w_e_gate, w_e_up, w_e_down, g_final):
    assert g_mix.shape[0] == 1, "single-layer trunk"
    w = prep_weights(g_mix[0], g_mem[0], w_in[0], w_mem_kv[0], na_rpb[0], t5_table, sw_sink[0], w_na_o[0],
                     w_sw_o[0], w_mx_o[0], w_out[0], g_ffn[0], w_router[0])
    experts = (w_e_gate[0].astype(BF16), w_e_up[0].astype(BF16), w_e_down[0].astype(BF16))
    gf = g_final.reshape(1, -1).astype(F32)
    y_prompt = encoder_group(x_prompt, mem_prompt, w, experts, gf)
    y_sample = encoder_group(x_sample, mem_sample, w, experts, gf)
    return (y_prompt, y_sample)
```

```python
import functools
import math

import numpy as np
import jax
import jax.numpy as jnp
from jax import lax
from jax.experimental import pallas as pl
from jax.experimental.pallas import tpu as pltpu

F32 = jnp.float32
BF16 = jnp.bfloat16
I32 = jnp.int32

RMS_EPS = 1e-6
NEG_INF = -1e30

GRID_W = 64
NA_HEADS = 8
NA_HEAD_DIM = 64
NA_KR = 8
NA_KC = 16
SW_HEADS = 16
SW_KV_HEADS = 4
SW_HEAD_DIM = 64
SW_WINDOW = 128
SW_BLOCK = 128
MX_HEADS = 4
T5_BUCKETS = 32
T5_MAX_DIST = 128
N_BRANCHES = 3
N_EXPERTS = 16
EC_CAPACITY = 2

LANES = 128
SUBLANES = 8
V7X_VMEM_BYTES = 64 * 1024 * 1024
VMEM_LIMIT = V7X_VMEM_BYTES * 7 // 8

CHUNK_UNROLL = 8
EXPERT_TS = 512

NT_DIMS = (((1,), (1,)), ((), ()))
TN_DIMS = (((0,), (0,)), ((), ()))


def _params(*sem):
    return pltpu.CompilerParams(dimension_semantics=sem, vmem_limit_bytes=VMEM_LIMIT)


def _rms(x, g):
    return x * lax.rsqrt(jnp.mean(x * x, axis=-1, keepdims=True) + RMS_EPS) * g


def _to_row_tiles(strip):
    return strip.reshape(strip.shape[0] // SUBLANES, SUBLANES, LANES)


def _from_row_tiles(tiles):
    return tiles.reshape(tiles.shape[0] * SUBLANES, LANES)


def _row_tile_shape(rows, d):
    return (rows // SUBLANES, d // LANES, SUBLANES, LANES)


def _norm_proj_kernel(x_ref, g_ref, w_ref, *o_refs, scales):
    h = _rms(x_ref[...], g_ref[...]).astype(BF16)
    off = 0
    for o_ref, sc in zip(o_refs, scales):
        width = o_ref.shape[1]
        for c0 in range(0, width, 512):
            cw = min(512, width - c0)
            r = jnp.dot(h, w_ref[:, off + c0:off + c0 + cw], preferred_element_type=F32)
            if sc != 1.0:
                r = r * sc
            o_ref[:, c0:c0 + cw] = r.astype(o_ref.dtype)
        off += width


def norm_proj(x, g, w, widths, scales, tm):
    n, d = x.shape
    tm = min(tm, n)
    assert n % tm == 0
    return pl.pallas_call(
        functools.partial(_norm_proj_kernel, scales=tuple(scales)),
        grid=(n // tm,),
        in_specs=[pl.BlockSpec((tm, d), lambda i: (i, 0)),
                  pl.BlockSpec((1, d), lambda i: (0, 0)),
                  pl.BlockSpec(w.shape, lambda i: (0, 0))],
        out_specs=[pl.BlockSpec((tm, c), lambda i: (i, 0)) for c in widths],
        out_shape=[jax.ShapeDtypeStruct((n, c), BF16) for c in widths],
        compiler_params=_params("parallel"),
        name="norm_proj",
    )(x, g, w)


def _softmax_pv(s, v, extra_logit=None):
    m = jnp.max(s, axis=-1, keepdims=True)
    if extra_logit is not None:
        m = jnp.maximum(m, extra_logit)
    e = jnp.exp(s - m)
    den = jnp.sum(e, axis=-1, keepdims=True)
    if extra_logit is not None:
        den = den + jnp.exp(extra_logit - m)
    o = jnp.dot(e.astype(BF16), v, preferred_element_type=F32)
    return o / den


def _softmax_pv_half(s, v, extra_logit, use_lo):
    rows, keys = s.shape
    mb = jnp.maximum(jnp.broadcast_to(jnp.max(s, axis=-1, keepdims=True), (rows, LANES)), extra_logit)
    e = jnp.concatenate([jnp.exp(s[:, t * LANES:(t + 1) * LANES] - mb) for t in range(keys // LANES)],
                        axis=1).astype(BF16)
    lo = lax.broadcasted_iota(I32, v.shape, 1) < LANES // 2
    keep = lo if use_lo else jnp.logical_not(lo)
    o = jnp.dot(e, jnp.where(keep, v, jnp.ones_like(v)), preferred_element_type=F32)
    den = pltpu.roll(o + jnp.exp(extra_logit - mb), LANES // 2, 1)
    return o / den


def _na_kernel(q_ref, k_ref, v_ref, bias_ref, o_ref, *, rows, rb):
    j = pl.program_id(1)
    lo = lax.broadcasted_iota(I32, (GRID_W, LANES), 1) < NA_HEAD_DIM
    nkeys = NA_KR * GRID_W

    def body(i, carry):
        r = j * rb + i
        rs = jnp.clip(r - NA_KR // 2, 0, rows - NA_KR)
        off = r - rs
        q = q_ref[0, pl.ds(pl.multiple_of(i * GRID_W, GRID_W), GRID_W), :]
        kk = k_ref[0, pl.ds(pl.multiple_of(rs * GRID_W, GRID_W), nkeys), :]
        vv = v_ref[0, pl.ds(pl.multiple_of(rs * GRID_W, GRID_W), nkeys), :]
        scores = []
        for p in range(NA_HEADS // 2):
            qp = q[:, p * LANES:(p + 1) * LANES]
            kp = kk[:, p * LANES:(p + 1) * LANES]
            zero = jnp.zeros_like(qp)
            q2 = jnp.concatenate([jnp.where(lo, qp, zero), jnp.where(lo, zero, qp)], axis=0)
            s = lax.dot_general(q2, kp, NT_DIMS, preferred_element_type=F32)
            scores.append(s + bias_ref[off, 2 * p:2 * p + 2].reshape(2 * GRID_W, nkeys))
        outs = []
        for p in range(NA_HEADS // 2):
            o2 = _softmax_pv(scores[p], vv[:, p * LANES:(p + 1) * LANES])
            outs.append(jnp.where(lo, o2[:GRID_W], o2[GRID_W:]))
        o_ref[0, pl.ds(pl.multiple_of(i * GRID_W, GRID_W), GRID_W), :] = (
            jnp.concatenate(outs, axis=1).astype(o_ref.dtype))
        return carry

    lax.fori_loop(0, rb, body, 0, unroll=2)


def na_attention(q, k, v, bias, rb=8):
    b, t, c = q.shape
    rows = t // GRID_W
    assert rows >= NA_KR and rows % rb == 0
    return pl.pallas_call(
        functools.partial(_na_kernel, rows=rows, rb=rb),
        grid=(b, rows // rb),
        in_specs=[pl.BlockSpec((1, rb * GRID_W, c), lambda i, j: (i, j, 0)),
                  pl.BlockSpec((1, t, c), lambda i, j: (i, 0, 0)),
                  pl.BlockSpec((1, t, c), lambda i, j: (i, 0, 0)),
                  pl.BlockSpec(bias.shape, lambda i, j: (0, 0, 0, 0))],
        out_specs=pl.BlockSpec((1, rb * GRID_W, c), lambda i, j: (i, j, 0)),
        out_shape=jax.ShapeDtypeStruct((b, t, c), BF16),
        compiler_params=_params("parallel", "arbitrary"),
        name="na_attention",
    )(q, k, v, bias)


def na_bias_table(rpb):
    col = np.arange(GRID_W)
    col_start = np.clip(col - NA_KC // 2, 0, GRID_W - NA_KC)
    in_win = (col[None, :] >= col_start[:, None]) & (col[None, :] < col_start[:, None] + NA_KC)
    dc = np.clip(col[None, :] - col[:, None] + NA_KC - 1, 0, 2 * NA_KC - 2)
    heads = rpb.shape[0]
    by_col = jnp.take(rpb.astype(F32), jnp.asarray(dc.reshape(-1)), axis=2)
    by_col = by_col.reshape(heads, 2 * NA_KR - 1, GRID_W, GRID_W)
    by_col = jnp.where(jnp.asarray(in_win)[None, None], by_col, NEG_INF)
    per_off = [by_col[:, NA_KR - 1 - off:2 * NA_KR - 1 - off].transpose(0, 2, 1, 3) for off in range(NA_KR)]
    return jnp.stack(per_off).reshape(NA_KR, heads, GRID_W, NA_KR * GRID_W)


def _sw_kernel(sink_ref, q_ref, k_ref, v_ref, bias_ref, o_ref, s_scr, *, nb):
    n = pl.program_id(1)
    blk = SW_BLOCK
    group = SW_HEADS // SW_KV_HEADS

    def rows_of(ref, c):
        return ref[0, pl.ds(pl.multiple_of(c * blk, blk), blk), :]

    cl = jnp.maximum(n - 1, 0)
    cr = jnp.minimum(n + 1, nb - 1)
    k3 = jnp.concatenate([rows_of(k_ref, cl), rows_of(k_ref, n), rows_of(k_ref, cr)], axis=0)
    v3 = jnp.concatenate([rows_of(v_ref, cl), rows_of(v_ref, n), rows_of(v_ref, cr)], axis=0)
    pen_l = jnp.where(n > 0, 0.0, NEG_INF).astype(F32)
    pen_r = jnp.where(n < nb - 1, 0.0, NEG_INF).astype(F32)
    key = lax.broadcasted_iota(I32, (1, 3 * blk), 1)
    pen = jnp.where(key < blk, pen_l, jnp.where(key >= 2 * blk, pen_r, 0.0))
    lo = lax.broadcasted_iota(I32, (blk, LANES), 1) < SW_HEAD_DIM

    for c in range(SW_KV_HEADS):
        pair, half = divmod(c, 2)
        kp = k3[:, pair * LANES:(pair + 1) * LANES]
        keep = lo if half == 0 else jnp.logical_not(lo)
        qs = []
        for g in range(group):
            t = pair * group + g
            qt = q_ref[0, :, t * LANES:(t + 1) * LANES]
            qs.append(jnp.where(keep, qt, jnp.zeros_like(qt)))
        qq = jnp.concatenate(qs, axis=0)
        s = lax.dot_general(qq, kp, NT_DIMS, preferred_element_type=F32)
        s_scr[c] = s + bias_ref[c * group:(c + 1) * group].reshape(group * blk, 3 * blk) + pen

    for pair in range(SW_KV_HEADS // 2):
        vp = v3[:, pair * LANES:(pair + 1) * LANES]
        per_half = []
        for half in range(2):
            c = 2 * pair + half
            sink = jnp.concatenate(
                [jnp.full((blk, LANES), sink_ref[c * group + g], F32) for g in range(group)], axis=0)
            per_half.append(_softmax_pv_half(s_scr[c], vp, sink, use_lo=(half == 0)))
        for g in range(group):
            t = pair * group + g
            o = jnp.where(lo, per_half[0][g * blk:(g + 1) * blk], per_half[1][g * blk:(g + 1) * blk])
            o_ref[0, :, t * LANES:(t + 1) * LANES] = o.astype(o_ref.dtype)


def sw_attention(q, k, v, bias, sink):
    b, t, c = q.shape
    nb = t // SW_BLOCK
    kvw = k.shape[2]
    grid_spec = pltpu.PrefetchScalarGridSpec(
        num_scalar_prefetch=1,
        grid=(b, nb),
        in_specs=[pl.BlockSpec((1, SW_BLOCK, c), lambda i, j, s: (i, j, 0)),
                  pl.BlockSpec((1, t, kvw), lambda i, j, s: (i, 0, 0)),
                  pl.BlockSpec((1, t, kvw), lambda i, j, s: (i, 0, 0)),
                  pl.BlockSpec(bias.shape, lambda i, j, s: (0, 0, 0))],
        out_specs=pl.BlockSpec((1, SW_BLOCK, c), lambda i, j, s: (i, j, 0)),
        scratch_shapes=[pltpu.VMEM((SW_KV_HEADS, (SW_HEADS // SW_KV_HEADS) * SW_BLOCK, 3 * SW_BLOCK), F32)],
    )
    return pl.pallas_call(
        functools.partial(_sw_kernel, nb=nb),
        grid_spec=grid_spec,
        out_shape=jax.ShapeDtypeStruct((b, t, c), BF16),
        compiler_params=_params("parallel", "arbitrary"),
        name="sw_attention",
    )(sink, q, k, v, bias)


def _t5_bucket(rel):
    half = T5_BUCKETS // 2
    max_exact = half // 2
    n = jnp.abs(rel)
    nf = jnp.maximum(n, 1).astype(F32)
    large = max_exact + (jnp.log(nf / max_exact) / math.log(T5_MAX_DIST / max_exact)
                         * (half - max_exact)).astype(jnp.int32)
    large = jnp.minimum(large, half - 1)
    return jnp.where(rel > 0, half, 0) + jnp.where(n < max_exact, n, large)


def sw_bias_table(t5_table):
    span = SW_BLOCK + 2 * SW_WINDOW
    rel = np.arange(span)[None, :] - SW_WINDOW - np.arange(SW_BLOCK)[:, None]
    bucket = _t5_bucket(jnp.asarray(rel, dtype=jnp.int32))[None]
    table = t5_table.astype(F32).T[:, :, None, None]
    bias = jnp.zeros((t5_table.shape[1], SW_BLOCK, span), F32)
    for b in range(T5_BUCKETS):
        bias = jnp.where(bucket == b, table[:, b], bias)
    return jnp.where(jnp.asarray(np.abs(rel) <= SW_WINDOW)[None], bias, NEG_INF)


def sw_pair_heads(w, axis):
    group = SW_HEADS // SW_KV_HEADS
    shape = w.shape
    split = shape[:axis] + (SW_KV_HEADS // 2, 2, group, SW_HEAD_DIM) + shape[axis + 1:]
    order = list(range(len(split)))
    order[axis + 1], order[axis + 2] = axis + 2, axis + 1
    return w.reshape(split).transpose(order).reshape(shape)


def _mx_kernel(q_ref, mk_ref, mv_ref, o_ref, *, scale):
    hd = q_ref.shape[2] // MX_HEADS
    heads = [slice(h * hd, (h + 1) * hd) for h in range(MX_HEADS)]
    scores = [lax.dot_general(q_ref[0, :, sl], mk_ref[0, :, sl], NT_DIMS, preferred_element_type=F32) * scale
              for sl in heads]
    for sl, s in zip(heads, scores):
        o_ref[0, :, sl] = _softmax_pv(s, mv_ref[0, :, sl]).astype(o_ref.dtype)


def mx_attention(q, mkv, tq=512):
    b, t, c = q.shape
    m = mkv.shape[1]
    tq = min(tq, t)
    return pl.pallas_call(
        functools.partial(_mx_kernel, scale=float((c // MX_HEADS) ** -0.5)),
        grid=(b, t // tq),
        in_specs=[pl.BlockSpec((1, tq, c), lambda i, j: (i, j, 0)),
                  pl.BlockSpec((1, m, c), lambda i, j: (i, 0, 0)),
                  pl.BlockSpec((1, m, c), lambda i, j: (i, 0, 1))],
        out_specs=pl.BlockSpec((1, tq, c), lambda i, j: (i, j, 0)),
        out_shape=jax.ShapeDtypeStruct((b, t, c), BF16),
        compiler_params=_params("parallel", "arbitrary"),
        name="mx_attention",
    )(q, mkv, mkv)


def _merge_kernel(x_ref, ona_ref, osw_ref, omx_ref, gmix_ref, wg_ref, wna_ref, wsw_ref, wmx_ref, wout_ref,
                  gffn_ref, wrh_ref, wrl_ref, z_ref, aff_ref, h_scr):
    j = pl.program_id(1)
    nstrip = z_ref.shape[1] // 3
    strips = range(nstrip)

    @pl.when(j == 0)
    def _():
        x = x_ref[...]
        h_scr[...] = _rms(x, gmix_ref[...]).astype(BF16)
        for c in strips:
            z_ref[:, c] = _to_row_tiles(x[:, c * LANES:(c + 1) * LANES])

    h = h_scr[...]
    merged = None
    for o_ref, w_ref, b in ((ona_ref, wna_ref, 0), (osw_ref, wsw_ref, 1), (omx_ref, wmx_ref, 2)):
        gate = jax.nn.sigmoid(jnp.dot(h, wg_ref[b], preferred_element_type=F32))
        term = gate * jnp.dot(o_ref[...], w_ref[...], preferred_element_type=F32)
        merged = term if merged is None else merged + term
    part = jnp.dot(merged.astype(BF16), wout_ref[...], preferred_element_type=F32)
    for c in strips:
        z_ref[:, c] += _to_row_tiles(part[:, c * LANES:(c + 1) * LANES])

    @pl.when(j == pl.num_programs(1) - 1)
    def _():
        x2 = jnp.concatenate([_from_row_tiles(z_ref[:, c]) for c in strips], axis=1)
        h2 = _rms(x2, gffn_ref[...])
        for c in strips:
            z_ref[:, nstrip + c] = _to_row_tiles(h2[:, c * LANES:(c + 1) * LANES])
            z_ref[:, 2 * nstrip + c] = jnp.zeros((z_ref.shape[0], SUBLANES, LANES), F32)
        hi = h2.astype(BF16)
        lo = (h2 - hi.astype(F32)).astype(BF16)
        wh = wrh_ref[...]
        logits = (lax.dot_general(wh, hi, NT_DIMS, preferred_element_type=F32)
                  + lax.dot_general(wh, lo, NT_DIMS, preferred_element_type=F32)
                  + lax.dot_general(wrl_ref[...], hi, NT_DIMS, preferred_element_type=F32))
        m = jnp.max(logits, axis=0, keepdims=True)
        e = jnp.exp(logits - m)
        aff = e / jnp.sum(e, axis=0, keepdims=True)
        for c in range(aff_ref.shape[0]):
            aff_ref[c] = aff[:, c * LANES:(c + 1) * LANES]


def merge(x, o_na, o_sw, o_mx, g_mix, w_gate, w_na_o, w_sw_o, w_mx_o, w_out, g_ffn, wr_hi, wr_lo,
          tm=512, tn=256):
    n, d = x.shape
    tm = min(tm, n)
    assert n % tm == 0 and tm % LANES == 0 and d % LANES == 0
    ne = wr_hi.shape[0]
    row = lambda i, j: (i, 0)
    return pl.pallas_call(
        _merge_kernel,
        grid=(n // tm, d // tn),
        in_specs=[pl.BlockSpec((tm, d), row),
                  pl.BlockSpec((tm, o_na.shape[1]), row),
                  pl.BlockSpec((tm, o_sw.shape[1]), row),
                  pl.BlockSpec((tm, o_mx.shape[1]), row),
                  pl.BlockSpec((1, d), lambda i, j: (0, 0)),
                  pl.BlockSpec((N_BRANCHES, d, tn), lambda i, j: (0, 0, j)),
                  pl.BlockSpec((w_na_o.shape[0], tn), lambda i, j: (0, j)),
                  pl.BlockSpec((w_sw_o.shape[0], tn), lambda i, j: (0, j)),
                  pl.BlockSpec((w_mx_o.shape[0], tn), lambda i, j: (0, j)),
                  pl.BlockSpec((tn, d), lambda i, j: (j, 0)),
                  pl.BlockSpec((1, d), lambda i, j: (0, 0)),
                  pl.BlockSpec((ne, d), lambda i, j: (0, 0)),
                  pl.BlockSpec((ne, d), lambda i, j: (0, 0))],
        out_specs=[pl.BlockSpec(_row_tile_shape(tm, 3 * d), lambda i, j: (i, 0, 0, 0)),
                   pl.BlockSpec((tm // LANES, ne, LANES), lambda i, j: (i, 0, 0))],
        out_shape=[jax.ShapeDtypeStruct(_row_tile_shape(n, 3 * d), F32),
                   jax.ShapeDtypeStruct((n // LANES, ne, LANES), F32)],
        scratch_shapes=[pltpu.VMEM((tm, d), BF16)],
        compiler_params=_params("parallel", "arbitrary"),
        name="merge",
    )(x, o_na, o_sw, o_mx, g_mix, w_gate, w_na_o, w_sw_o, w_mx_o, w_out, g_ffn, wr_hi, wr_lo)


NA_WIDTH = NA_HEADS * NA_HEAD_DIM
SW_WIDTH = SW_HEADS * SW_HEAD_DIM
SW_KV_WIDTH = SW_KV_HEADS * SW_HEAD_DIM
QKV_WIDTHS = (NA_WIDTH, NA_WIDTH, NA_WIDTH, SW_WIDTH, SW_KV_WIDTH, SW_KV_WIDTH)


def prep_weights(g_mix, g_mem, w_in, w_mem_kv, na_rpb, t5_table, sw_sink, w_na_o, w_sw_o, w_mx_o, w_out,
                 g_ffn, w_router):
    d = w_in.shape[0]
    mx_width = w_mx_o.shape[0]
    widths = QKV_WIDTHS + (mx_width,)
    offs = np.concatenate([[0], np.cumsum(widths)])
    cols = [w_in[:, offs[i]:offs[i + 1]] for i in range(len(widths))]
    cols[3] = sw_pair_heads(cols[3], axis=1)
    w_gate = w_in[:, offs[-1]:].reshape(d, N_BRANCHES, d).transpose(1, 0, 2)
    wr = w_router.T.astype(F32)
    wr_hi = wr.astype(BF16)
    return dict(
        g_mix=g_mix.reshape(1, d).astype(F32), g_mem=g_mem.reshape(1, d).astype(F32),
        g_ffn=g_ffn.reshape(1, d).astype(F32),
        w_qkv=jnp.concatenate(cols, axis=1).astype(BF16), qkv_widths=widths,
        qkv_scales=(NA_HEAD_DIM ** -0.5, 1.0, 1.0, SW_HEAD_DIM ** -0.5, 1.0, 1.0, 1.0),
        w_mem_kv=w_mem_kv.astype(BF16),
        na_bias=na_bias_table(na_rpb), sw_bias=sw_bias_table(t5_table), sw_sink=sw_sink.astype(F32),
        w_gate=w_gate.astype(BF16), w_na_o=w_na_o.astype(BF16), w_sw_o=sw_pair_heads(w_sw_o, axis=0).astype(BF16),
        w_mx_o=w_mx_o.astype(BF16), w_out=w_out.astype(BF16),
        wr_hi=wr_hi, wr_lo=(wr - wr_hi.astype(F32)).astype(BF16))


def layer_front(x, mem, w, debug=False):
    b, t, d = x.shape
    m = mem.shape[1]
    xf = x.reshape(b * t, d)
    na_q, na_k, na_v, sw_q, sw_k, sw_v, mx_q = norm_proj(xf, w["g_mix"], w["w_qkv"], w["qkv_widths"],
                                                         w["qkv_scales"], tm=512)
    (mkv,) = norm_proj(mem.reshape(b * m, d), w["g_mem"], w["w_mem_kv"], (w["w_mem_kv"].shape[1],), (1.0,),
                       tm=512)
    r3 = lambda a: a.reshape(b, t, a.shape[1])
    o_na = na_attention(r3(na_q), r3(na_k), r3(na_v), w["na_bias"])
    o_sw = sw_attention(r3(sw_q), r3(sw_k), r3(sw_v), w["sw_bias"], w["sw_sink"])
    o_mx = mx_attention(r3(mx_q), mkv.reshape(b, m, mkv.shape[1]))
    f2 = lambda a: a.reshape(b * t, a.shape[2])
    z, aff = merge(xf, f2(o_na), f2(o_sw), f2(o_mx), w["g_mix"], w["w_gate"], w["w_na_o"], w["w_sw_o"],
                   w["w_mx_o"], w["w_out"], w["g_ffn"], w["wr_hi"], w["wr_lo"])
    if debug:
        return o_na, o_sw, o_mx, z, aff
    return z, aff


def _select_kernel(aff_ref, sel_ref, pos_ref, off_ref, *, cap):
    nc, ne, _ = aff_ref.shape
    bits = lax.bitcast_convert_type(aff_ref[...], I32)
    tok = (lax.broadcasted_iota(I32, bits.shape, 0) * LANES + lax.broadcasted_iota(I32, bits.shape, 2))

    def count(flags):
        return jnp.sum(flags, axis=(0, 2), keepdims=True)

    def value_step(i, prefix):
        cand = prefix | lax.shift_left(jnp.int32(1), 30 - i)
        cnt = count(jnp.where(bits >= cand, 1.0, 0.0))
        return jnp.where(cnt >= cap, cand, prefix)

    tau = lax.fori_loop(0, 31, value_step, jnp.zeros((1, ne, 1), I32))
    gt = bits > tau
    eq = bits == tau
    need = cap - count(jnp.where(gt, 1.0, 0.0))

    def index_step(i, last):
        cand = last | lax.shift_left(jnp.int32(1), 15 - i)
        cnt = count(jnp.where(eq, jnp.where(tok < cand, 1.0, 0.0), 0.0))
        return jnp.where(cnt < need, cand, last)

    last = lax.fori_loop(0, 16, index_step, jnp.zeros((1, ne, 1), I32))
    sel_ref[...] = jnp.where(gt, 1.0, jnp.where(eq, jnp.where(tok <= last, 1.0, 0.0), 0.0))

    upper = (lax.broadcasted_iota(I32, (LANES, LANES), 0) <= lax.broadcasted_iota(I32, (LANES, LANES), 1))
    upper = jnp.where(upper, 1.0, 0.0).astype(BF16)

    def chunk_step(c, off):
        s = sel_ref[c]
        cum = jnp.dot(s.astype(BF16), upper, preferred_element_type=F32)
        pos_ref[c] = off + cum - s
        off_ref[c] = jnp.broadcast_to(off, s.shape)
        return off + cum[:, LANES - 1:LANES]

    lax.fori_loop(0, nc, chunk_step, jnp.zeros((ne, 1), F32), unroll=CHUNK_UNROLL)


def select(aff, cap):
    nc, ne, _ = aff.shape
    assert nc * LANES <= 65536
    shape = jax.ShapeDtypeStruct(aff.shape, F32)
    return pl.pallas_call(
        functools.partial(_select_kernel, cap=float(cap)),
        out_shape=[shape, shape, shape],
        compiler_params=pltpu.CompilerParams(vmem_limit_bytes=VMEM_LIMIT),
        name="select",
    )(aff)


TOK_COL, CHUNK_COL, HI_COL = 0, 1, 2


def _compact_kernel(choff_ref, sel_ref, pos_ref, aff_ref, o_ref, tv_scr):
    e = pl.program_id(0)
    nc, ne, _ = aff_ref.shape
    win = 2 * LANES
    col = lax.broadcasted_iota(I32, (LANES, LANES), 1)
    row = lax.broadcasted_iota(I32, (LANES, LANES), 0)

    @pl.when(e == 0)
    def _():
        ecol = lax.broadcasted_iota(I32, (ne, LANES), 1) - lax.broadcasted_iota(I32, (ne, LANES), 0)
        place = [jnp.where(ecol == HI_COL + k * ne, 1.0, 0.0).astype(BF16) for k in range(3)]

        def build(c, carry):
            a = aff_ref[c]
            hi = a.astype(BF16)
            r1 = a - hi.astype(F32)
            mid = r1.astype(BF16)
            lo = (r1 - mid.astype(F32)).astype(BF16)
            rec = (lax.dot_general(hi, place[0], TN_DIMS, preferred_element_type=F32)
                   + lax.dot_general(mid, place[1], TN_DIMS, preferred_element_type=F32)
                   + lax.dot_general(lo, place[2], TN_DIMS, preferred_element_type=F32))
            rec = rec + jnp.where(col == TOK_COL, row, jnp.where(col == CHUNK_COL, c, 0)).astype(F32)
            tv_scr[c] = rec.astype(BF16)
            return carry

        lax.fori_loop(0, nc, build, 0, unroll=CHUNK_UNROLL)

    o_ref[...] = jnp.zeros_like(o_ref)
    slot0 = lax.broadcasted_iota(I32, (win, LANES), 0).astype(F32)

    def body(c, carry):
        off = choff_ref[c * ne + e]
        base = pl.multiple_of((off // LANES) * LANES, LANES)
        s = sel_ref[c, pl.ds(e, 1), :]
        p = pos_ref[c, pl.ds(e, 1), :] - base.astype(F32)
        onehot = jnp.where(s > 0.0, jnp.where(slot0 == p, 1.0, 0.0), 0.0).astype(BF16)
        o_ref[0, pl.ds(base, win), :] += jnp.dot(onehot, tv_scr[c], preferred_element_type=F32)
        return carry

    lax.fori_loop(0, nc, body, 0, unroll=CHUNK_UNROLL)


def compact(sel, pos, choff, aff, cap):
    nc, ne, _ = aff.shape
    assert nc <= 256 and HI_COL + 3 * ne <= LANES and cap % LANES == 0
    rows = cap + 2 * LANES
    full = pl.BlockSpec(aff.shape, lambda e, s: (0, 0, 0))
    rec = pl.pallas_call(
        _compact_kernel,
        grid_spec=pltpu.PrefetchScalarGridSpec(
            num_scalar_prefetch=1, grid=(ne,),
            in_specs=[full, full, full],
            out_specs=pl.BlockSpec((1, rows, LANES), lambda e, s: (e, 0, 0)),
            scratch_shapes=[pltpu.VMEM((nc, LANES, LANES), BF16)]),
        out_shape=jax.ShapeDtypeStruct((ne, rows, LANES), F32),
        compiler_params=_params("arbitrary"),
        name="compact",
    )(choff, sel, pos, aff)
    idx = (rec[:, :cap, CHUNK_COL] * LANES + rec[:, :cap, TOK_COL]).astype(I32)
    return idx, rec


def _expert_kernel(idx_prev, idx_cur, idx_next, rec_ref, zin_ref, wg_ref, wu_ref, wd_ref, z_ref,
                   buf, xs, acc, gsem, ssem, *, ts, nf, s_tiles, ne):
    del zin_ref
    sub = z_ref.shape[1] // 3
    t = pl.program_id(0)
    f = pl.program_id(1)
    last_t = pl.num_programs(0) - 1
    rps = ts // nf
    slot, nslot, pslot = t % 3, (t + 1) % 3, (t + 2) % 3
    odd = (t // s_tiles) % 2
    nodd = (jnp.minimum(t + 1, last_t) // s_tiles) % 2
    podd = (jnp.maximum(t - 1, 0) // s_tiles) % 2

    def rows(ref, row, first, count):
        return ref.at[row >> 3, pl.ds(first, count), pl.ds(row & (SUBLANES - 1), 1), :]

    def gather(token, ra, rs, to_slot, is_odd):
        return pltpu.make_async_copy(rows(z_ref, token, is_odd * sub, 2 * sub),
                                     buf.at[to_slot, ra, :, pl.ds(rs, 1), :], gsem.at[to_slot])

    def scatter(token, ra, rs, from_slot, is_odd):
        return pltpu.make_async_copy(buf.at[from_slot, ra, pl.ds(is_odd * sub, sub), pl.ds(rs, 1), :],
                                     rows(z_ref, token, is_odd * 2 * sub, sub), ssem.at[from_slot])

    def wait_gather(s):
        pltpu.make_async_copy(z_ref.at[pl.ds(0, ts // SUBLANES), pl.ds(0, 2 * sub)], buf.at[s], gsem.at[s]).wait()

    def wait_scatter(s):
        pltpu.make_async_copy(buf.at[s, :, pl.ds(0, sub)], z_ref.at[pl.ds(0, ts // SUBLANES), pl.ds(0, sub)],
                              ssem.at[s]).wait()

    def for_rows(fn):
        def body(r, carry):
            fn(idx_cur[0, 0, r], r >> 3, r & (SUBLANES - 1))
            return carry
        lax.fori_loop(0, ts, body, 0)

    @pl.when(f == 0)
    def _():
        @pl.when(t == 0)
        def _():
            def start(token, ra, rs):
                gather(token, ra, rs, 0, 0).start()
                gather(token, ra, rs, 2, 0).start()
            for_rows(start)
            wait_gather(2)
            acc[...] = jnp.zeros(acc.shape, F32)

        wait_gather(slot)
        h2_first = (1 - odd) * sub
        for c in range(sub):
            xs[:, c * LANES:(c + 1) * LANES] = _from_row_tiles(buf[slot, :, h2_first + c]).astype(BF16)

    for k in range(rps):
        r = f * rps + k
        ra, rs = f * (rps // SUBLANES) + k // SUBLANES, k % SUBLANES
        gather(idx_next[0, 0, r], ra, rs, nslot, nodd).start()
        scatter(idx_prev[0, 0, r], ra, rs, pslot, podd).start()

    x = xs[...]
    a = jnp.dot(x, wg_ref[0], preferred_element_type=F32)
    b = jnp.dot(x, wu_ref[0], preferred_element_type=F32)
    hm = (jax.nn.silu(a) * b).astype(BF16)
    acc[...] = jnp.where(f == 0, 0.0, acc[...]) + jnp.dot(hm, wd_ref[0], preferred_element_type=F32)

    @pl.when(f == nf - 1)
    def _():
        lane = lax.broadcasted_iota(I32, (1, LANES), 1) - (HI_COL + t // s_tiles)
        own = (lane == 0) | (lane == ne) | (lane == 2 * ne)
        contrib = acc[...] * jnp.sum(jnp.where(own, rec_ref[0], 0.0), axis=1, keepdims=True)
        y_first = odd * sub
        for c in range(sub):
            buf[slot, :, y_first + c] += _to_row_tiles(contrib[:, c * LANES:(c + 1) * LANES])
        wait_scatter(pslot)

        @pl.when(t == last_t)
        def _():
            wait_gather(nslot)
            for_rows(lambda token, ra, rs: scatter(token, ra, rs, slot, odd).start())
            wait_scatter(slot)


def expert_ffn(idx, rec, z, w_gate, w_up, w_down, tf=512):
    ne, cap = idx.shape
    d = z.shape[1] // 3 * LANES
    ff = w_gate.shape[2]
    ts = EXPERT_TS
    tf = min(tf, ff)
    nf = ff // tf
    s_tiles = cap // ts
    assert cap % ts == 0 and s_tiles >= 2 and ts % nf == 0
    nt = ne * s_tiles
    idx3 = idx.reshape(nt, 1, ts)
    smem_tile = lambda shift: pl.BlockSpec(
        (1, 1, ts), lambda t, f: (jnp.clip(t + shift, 0, nt - 1), 0, 0), memory_space=pltpu.SMEM)
    return pl.pallas_call(
        functools.partial(_expert_kernel, ts=ts, nf=nf, s_tiles=s_tiles, ne=ne),
        grid=(nt, nf),
        in_specs=[smem_tile(-1), smem_tile(0), smem_tile(1),
                  pl.BlockSpec((1, ts, LANES), lambda t, f: (t // s_tiles, t % s_tiles, 0)),
                  pl.BlockSpec(memory_space=pl.ANY),
                  pl.BlockSpec((1, d, tf), lambda t, f: (t // s_tiles, 0, f)),
                  pl.BlockSpec((1, d, tf), lambda t, f: (t // s_tiles, 0, f)),
                  pl.BlockSpec((1, tf, d), lambda t, f: (t // s_tiles, f, 0))],
        out_specs=pl.BlockSpec(memory_space=pl.ANY),
        out_shape=jax.ShapeDtypeStruct(z.shape, F32),
        scratch_shapes=[pltpu.VMEM((3,) + _row_tile_shape(ts, 2 * d), F32), pltpu.VMEM((ts, d), BF16),
                        pltpu.VMEM((ts, d), F32), pltpu.SemaphoreType.DMA((3,)), pltpu.SemaphoreType.DMA((3,))],
        input_output_aliases={4: 0},
        compiler_params=_params("arbitrary", "arbitrary"),
        name="expert_ffn",
    )(idx3, idx3, idx3, rec, z, w_gate, w_up, w_down)


def _final_norm_kernel(y0_ref, y1_ref, g_ref, o_ref):
    x = jnp.concatenate([_from_row_tiles(y0_ref[:, c] + y1_ref[:, c]) for c in range(y0_ref.shape[1])], axis=1)
    o_ref[...] = _rms(x, g_ref[...])


def final_norm(z, g, tm=512):
    n, d = z.shape[0] * SUBLANES, z.shape[1] // 3 * LANES
    tm = min(tm, n)
    assert n % tm == 0
    return pl.pallas_call(
        _final_norm_kernel,
        grid=(n // tm,),
        in_specs=[pl.BlockSpec(_row_tile_shape(tm, d), lambda i: (i, 0, 0, 0)),
                  pl.BlockSpec(_row_tile_shape(tm, d), lambda i: (i, 2, 0, 0)),
                  pl.BlockSpec((1, d), lambda i: (0, 0))],
        out_specs=pl.BlockSpec((tm, d), lambda i: (i, 0)),
        out_shape=jax.ShapeDtypeStruct((n, d), F32),
        compiler_params=_params("parallel"),
        name="final_norm",
    )(z, z, g)


def moe_and_norm(z, aff, w_e_gate, w_e_up, w_e_down, g_final):
    n = z.shape[0] * SUBLANES
    cap = EC_CAPACITY * n // N_EXPERTS
    sel, pos, off = select(aff, cap)
    choff = off[:, :, 0].astype(I32).reshape(-1)
    idx, rec = compact(sel, pos, choff, aff, cap)
    z = expert_ffn(idx, rec, z, w_e_gate, w_e_up, w_e_down)
    return final_norm(z, g_final)


def encoder_group(x, mem, w, experts, g_final):
    b, t, d = x.shape
    z, aff = layer_front(x, mem, w)
    return moe_and_norm(z, aff, *experts, g_final).reshape(b, t, d)


def kernel(x_prompt, x_sample, mem_prompt, mem_sample, g_mix, g_mem, w_in, w_mem_kv, na_rpb, t5_table, sw_sink,
           w_na_o, w_sw_o, w_mx_o, w_out, g_ffn, w_router, w_e_gate, w_e_up, w_e_down, g_final):
    assert g_mix.shape[0] == 1, "single-layer trunk"
    w = prep_weights(g_mix[0], g_mem[0], w_in[0], w_mem_kv[0], na_rpb[0], t5_table, sw_sink[0], w_na_o[0],
                     w_sw_o[0], w_mx_o[0], w_out[0], g_ffn[0], w_router[0])
    experts = (w_e_gate[0].astype(BF16), w_e_up[0].astype(BF16), w_e_down[0].astype(BF16))
    gf = g_final.reshape(1, -1).astype(F32)
    y_prompt = encoder_group(x_prompt, mem_prompt, w, experts, gf)
    y_sample = encoder_group(x_sample, mem_sample, w, experts, gf)
    return (y_prompt, y_sample)
```

```python
import functools
import math

import numpy as np
import jax
import jax.numpy as jnp
from jax import lax
from jax.experimental import pallas as pl
from jax.experimental.pallas import tpu as pltpu

F32 = jnp.float32
BF16 = jnp.bfloat16
I32 = jnp.int32

RMS_EPS = 1e-6
NEG_INF = -1e30

GRID_W = 64
NA_HEADS = 8
NA_HEAD_DIM = 64
NA_KR = 8
NA_KC = 16
SW_HEADS = 16
SW_KV_HEADS = 4
SW_HEAD_DIM = 64
SW_WINDOW = 128
SW_BLOCK = 128
MX_HEADS = 4
T5_BUCKETS = 32
T5_MAX_DIST = 128
N_BRANCHES = 3
N_EXPERTS = 16
EC_CAPACITY = 2

LANES = 128
SUBLANES = 8
V7X_VMEM_BYTES = 64 * 1024 * 1024
VMEM_LIMIT = V7X_VMEM_BYTES * 7 // 8

CHUNK_UNROLL = 8
EXPERT_TS = 512

NT_DIMS = (((1,), (1,)), ((), ()))
TN_DIMS = (((0,), (0,)), ((), ()))


def _params(*sem):
    return pltpu.CompilerParams(dimension_semantics=sem, vmem_limit_bytes=VMEM_LIMIT)


def _rms(x, g):
    return x * lax.rsqrt(jnp.mean(x * x, axis=-1, keepdims=True) + RMS_EPS) * g


def _to_row_tiles(strip):
    return strip.reshape(strip.shape[0] // SUBLANES, SUBLANES, LANES)


def _from_row_tiles(tiles):
    return tiles.reshape(tiles.shape[0] * SUBLANES, LANES)


def _row_tile_shape(rows, d):
    return (rows // SUBLANES, d // LANES, SUBLANES, LANES)


def _norm_proj_kernel(x_ref, g_ref, w_ref, *o_refs, scales):
    h = _rms(x_ref[...], g_ref[...]).astype(BF16)
    off = 0
    for o_ref, sc in zip(o_refs, scales):
        width = o_ref.shape[1]
        for c0 in range(0, width, 512):
            cw = min(512, width - c0)
            r = jnp.dot(h, w_ref[:, off + c0:off + c0 + cw], preferred_element_type=F32)
            if sc != 1.0:
                r = r * sc
            o_ref[:, c0:c0 + cw] = r.astype(o_ref.dtype)
        off += width


def norm_proj(x, g, w, widths, scales, tm):
    n, d = x.shape
    tm = min(tm, n)
    assert n % tm == 0
    return pl.pallas_call(
        functools.partial(_norm_proj_kernel, scales=tuple(scales)),
        grid=(n // tm,),
        in_specs=[pl.BlockSpec((tm, d), lambda i: (i, 0)),
                  pl.BlockSpec((1, d), lambda i: (0, 0)),
                  pl.BlockSpec(w.shape, lambda i: (0, 0))],
        out_specs=[pl.BlockSpec((tm, c), lambda i: (i, 0)) for c in widths],
        out_shape=[jax.ShapeDtypeStruct((n, c), BF16) for c in widths],
        compiler_params=_params("parallel"),
        name="norm_proj",
    )(x, g, w)


def _softmax_pv(s, v, extra_logit=None):
    m = jnp.max(s, axis=-1, keepdims=True)
    if extra_logit is not None:
        m = jnp.maximum(m, extra_logit)
    e = jnp.exp(s - m)
    den = jnp.sum(e, axis=-1, keepdims=True)
    if extra_logit is not None:
        den = den + jnp.exp(extra_logit - m)
    o = jnp.dot(e.astype(BF16), v, preferred_element_type=F32)
    return o / den


def _softmax_pv_half(s, v, extra_logit, use_lo):
    rows, keys = s.shape
    mb = jnp.maximum(jnp.broadcast_to(jnp.max(s, axis=-1, keepdims=True), (rows, LANES)), extra_logit)
    e = jnp.concatenate([jnp.exp(s[:, t * LANES:(t + 1) * LANES] - mb) for t in range(keys // LANES)],
                        axis=1).astype(BF16)
    lo = lax.broadcasted_iota(I32, v.shape, 1) < LANES // 2
    keep = lo if use_lo else jnp.logical_not(lo)
    o = jnp.dot(e, jnp.where(keep, v, jnp.ones_like(v)), preferred_element_type=F32)
    den = pltpu.roll(o + jnp.exp(extra_logit - mb), LANES // 2, 1)
    return o / den


def _na_kernel(q_ref, k_ref, v_ref, bias_ref, o_ref, *, rows, rb):
    j = pl.program_id(1)
    lo = lax.broadcasted_iota(I32, (GRID_W, LANES), 1) < NA_HEAD_DIM
    nkeys = NA_KR * GRID_W

    def body(i, carry):
        r = j * rb + i
        rs = jnp.clip(r - NA_KR // 2, 0, rows - NA_KR)
        off = r - rs
        q = q_ref[0, pl.ds(pl.multiple_of(i * GRID_W, GRID_W), GRID_W), :]
        kk = k_ref[0, pl.ds(pl.multiple_of(rs * GRID_W, GRID_W), nkeys), :]
        vv = v_ref[0, pl.ds(pl.multiple_of(rs * GRID_W, GRID_W), nkeys), :]
        scores = []
        for p in range(NA_HEADS // 2):
            qp = q[:, p * LANES:(p + 1) * LANES]
            kp = kk[:, p * LANES:(p + 1) * LANES]
            zero = jnp.zeros_like(qp)
            q2 = jnp.concatenate([jnp.where(lo, qp, zero), jnp.where(lo, zero, qp)], axis=0)
            s = lax.dot_general(q2, kp, NT_DIMS, preferred_element_type=F32)
            scores.append(s + bias_ref[off, 2 * p:2 * p + 2].reshape(2 * GRID_W, nkeys))
        outs = []
        for p in range(NA_HEADS // 2):
            o2 = _softmax_pv(scores[p], vv[:, p * LANES:(p + 1) * LANES])
            outs.append(jnp.where(lo, o2[:GRID_W], o2[GRID_W:]))
        o_ref[0, pl.ds(pl.multiple_of(i * GRID_W, GRID_W), GRID_W), :] = (
            jnp.concatenate(outs, axis=1).astype(o_ref.dtype))
        return carry

    lax.fori_loop(0, rb, body, 0, unroll=2)


def na_attention(q, k, v, bias, rb=8):
    b, t, c = q.shape
    rows = t // GRID_W
    assert rows >= NA_KR and rows % rb == 0
    return pl.pallas_call(
        functools.partial(_na_kernel, rows=rows, rb=rb),
        grid=(b, rows // rb),
        in_specs=[pl.BlockSpec((1, rb * GRID_W, c), lambda i, j: (i, j, 0)),
                  pl.BlockSpec((1, t, c), lambda i, j: (i, 0, 0)),
                  pl.BlockSpec((1, t, c), lambda i, j: (i, 0, 0)),
                  pl.BlockSpec(bias.shape, lambda i, j: (0, 0, 0, 0))],
        out_specs=pl.BlockSpec((1, rb * GRID_W, c), lambda i, j: (i, j, 0)),
        out_shape=jax.ShapeDtypeStruct((b, t, c), BF16),
        compiler_params=_params("parallel", "arbitrary"),
        name="na_attention",
    )(q, k, v, bias)


def na_bias_table(rpb):
    col = np.arange(GRID_W)
    col_start = np.clip(col - NA_KC // 2, 0, GRID_W - NA_KC)
    in_win = (col[None, :] >= col_start[:, None]) & (col[None, :] < col_start[:, None] + NA_KC)
    dc = np.clip(col[None, :] - col[:, None] + NA_KC - 1, 0, 2 * NA_KC - 2)
    heads = rpb.shape[0]
    by_col = jnp.take(rpb.astype(F32), jnp.asarray(dc.reshape(-1)), axis=2)
    by_col = by_col.reshape(heads, 2 * NA_KR - 1, GRID_W, GRID_W)
    by_col = jnp.where(jnp.asarray(in_win)[None, None], by_col, NEG_INF)
    per_off = [by_col[:, NA_KR - 1 - off:2 * NA_KR - 1 - off].transpose(0, 2, 1, 3) for off in range(NA_KR)]
    return jnp.stack(per_off).reshape(NA_KR, heads, GRID_W, NA_KR * GRID_W)


def _sw_kernel(sink_ref, q_ref, k_ref, v_ref, bias_ref, o_ref, s_scr, *, nb):
    n = pl.program_id(1)
    blk = SW_BLOCK
    group = SW_HEADS // SW_KV_HEADS

    def rows_of(ref, c):
        return ref[0, pl.ds(pl.multiple_of(c * blk, blk), blk), :]

    cl = jnp.maximum(n - 1, 0)
    cr = jnp.minimum(n + 1, nb - 1)
    k3 = jnp.concatenate([rows_of(k_ref, cl), rows_of(k_ref, n), rows_of(k_ref, cr)], axis=0)
    v3 = jnp.concatenate([rows_of(v_ref, cl), rows_of(v_ref, n), rows_of(v_ref, cr)], axis=0)
    pen_l = jnp.where(n > 0, 0.0, NEG_INF).astype(F32)
    pen_r = jnp.where(n < nb - 1, 0.0, NEG_INF).astype(F32)
    key = lax.broadcasted_iota(I32, (1, 3 * blk), 1)
    pen = jnp.where(key < blk, pen_l, jnp.where(key >= 2 * blk, pen_r, 0.0))
    lo = lax.broadcasted_iota(I32, (blk, LANES), 1) < SW_HEAD_DIM

    for c in range(SW_KV_HEADS):
        pair, half = divmod(c, 2)
        kp = k3[:, pair * LANES:(pair + 1) * LANES]
        keep = lo if half == 0 else jnp.logical_not(lo)
        qs = []
        for g in range(group):
            t = pair * group + g
            qt = q_ref[0, :, t * LANES:(t + 1) * LANES]
            qs.append(jnp.where(keep, qt, jnp.zeros_like(qt)))
        qq = jnp.concatenate(qs, axis=0)
        s = lax.dot_general(qq, kp, NT_DIMS, preferred_element_type=F32)
        s_scr[c] = s + bias_ref[c * group:(c + 1) * group].reshape(group * blk, 3 * blk) + pen

    for pair in range(SW_KV_HEADS // 2):
        vp = v3[:, pair * LANES:(pair + 1) * LANES]
        per_half = []
        for half in range(2):
            c = 2 * pair + half
            sink = jnp.concatenate(
                [jnp.full((blk, LANES), sink_ref[c * group + g], F32) for g in range(group)], axis=0)
            per_half.append(_softmax_pv_half(s_scr[c], vp, sink, use_lo=(half == 0)))
        for g in range(group):
            t = pair * group + g
            o = jnp.where(lo, per_half[0][g * blk:(g + 1) * blk], per_half[1][g * blk:(g + 1) * blk])
            o_ref[0, :, t * LANES:(t + 1) * LANES] = o.astype(o_ref.dtype)


def sw_attention(q, k, v, bias, sink):
    b, t, c = q.shape
    nb = t // SW_BLOCK
    kvw = k.shape[2]
    grid_spec = pltpu.PrefetchScalarGridSpec(
        num_scalar_prefetch=1,
        grid=(b, nb),
        in_specs=[pl.BlockSpec((1, SW_BLOCK, c), lambda i, j, s: (i, j, 0)),
                  pl.BlockSpec((1, t, kvw), lambda i, j, s: (i, 0, 0)),
                  pl.BlockSpec((1, t, kvw), lambda i, j, s: (i, 0, 0)),
                  pl.BlockSpec(bias.shape, lambda i, j, s: (0, 0, 0))],
        out_specs=pl.BlockSpec((1, SW_BLOCK, c), lambda i, j, s: (i, j, 0)),
        scratch_shapes=[pltpu.VMEM((SW_KV_HEADS, (SW_HEADS // SW_KV_HEADS) * SW_BLOCK, 3 * SW_BLOCK), F32)],
    )
    return pl.pallas_call(
        functools.partial(_sw_kernel, nb=nb),
        grid_spec=grid_spec,
        out_shape=jax.ShapeDtypeStruct((b, t, c), BF16),
        compiler_params=_params("parallel", "arbitrary"),
        name="sw_attention",
    )(sink, q, k, v, bias)


def _t5_bucket(rel):
    half = T5_BUCKETS // 2
    max_exact = half // 2
    n = jnp.abs(rel)
    nf = jnp.maximum(n, 1).astype(F32)
    large = max_exact + (jnp.log(nf / max_exact) / math.log(T5_MAX_DIST / max_exact)
                         * (half - max_exact)).astype(jnp.int32)
    large = jnp.minimum(large, half - 1)
    return jnp.where(rel > 0, half, 0) + jnp.where(n < max_exact, n, large)


def sw_bias_table(t5_table):
    span = SW_BLOCK + 2 * SW_WINDOW
    rel = np.arange(span)[None, :] - SW_WINDOW - np.arange(SW_BLOCK)[:, None]
    bucket = _t5_bucket(jnp.asarray(rel, dtype=jnp.int32))[None]
    table = t5_table.astype(F32).T[:, :, None, None]
    bias = jnp.zeros((t5_table.shape[1], SW_BLOCK, span), F32)
    for b in range(T5_BUCKETS):
        bias = jnp.where(bucket == b, table[:, b], bias)
    return jnp.where(jnp.asarray(np.abs(rel) <= SW_WINDOW)[None], bias, NEG_INF)


def sw_pair_heads(w, axis):
    group = SW_HEADS // SW_KV_HEADS
    shape = w.shape
    split = shape[:axis] + (SW_KV_HEADS // 2, 2, group, SW_HEAD_DIM) + shape[axis + 1:]
    order = list(range(len(split)))
    order[axis + 1], order[axis + 2] = axis + 2, axis + 1
    return w.reshape(split).transpose(order).reshape(shape)


def _mx_kernel(q_ref, mk_ref, mv_ref, o_ref, *, scale):
    hd = q_ref.shape[2] // MX_HEADS
    heads = [slice(h * hd, (h + 1) * hd) for h in range(MX_HEADS)]
    scores = [lax.dot_general(q_ref[0, :, sl], mk_ref[0, :, sl], NT_DIMS, preferred_element_type=F32) * scale
              for sl in heads]
    for sl, s in zip(heads, scores):
        o_ref[0, :, sl] = _softmax_pv(s, mv_ref[0, :, sl]).astype(o_ref.dtype)


def mx_attention(q, mkv, tq=512):
    b, t, c = q.shape
    m = mkv.shape[1]
    tq = min(tq, t)
    return pl.pallas_call(
        functools.partial(_mx_kernel, scale=float((c // MX_HEADS) ** -0.5)),
        grid=(b, t // tq),
        in_specs=[pl.BlockSpec((1, tq, c), lambda i, j: (i, j, 0)),
                  pl.BlockSpec((1, m, c), lambda i, j: (i, 0, 0)),
                  pl.BlockSpec((1, m, c), lambda i, j: (i, 0, 1))],
        out_specs=pl.BlockSpec((1, tq, c), lambda i, j: (i, j, 0)),
        out_shape=jax.ShapeDtypeStruct((b, t, c), BF16),
        compiler_params=_params("parallel", "arbitrary"),
        name="mx_attention",
    )(q, mkv, mkv)


def _merge_kernel(x_ref, ona_ref, osw_ref, omx_ref, gmix_ref, wg_ref, wna_ref, wsw_ref, wmx_ref, wout_ref,
                  gffn_ref, wrh_ref, wrl_ref, z_ref, aff_ref, h_scr):
    j = pl.program_id(1)
    nstrip = z_ref.shape[1] // 3
    strips = range(nstrip)

    @pl.when(j == 0)
    def _():
        x = x_ref[...]
        h_scr[...] = _rms(x, gmix_ref[...]).astype(BF16)
        for c in strips:
            z_ref[:, c] = _to_row_tiles(x[:, c * LANES:(c + 1) * LANES])

    h = h_scr[...]
    merged = None
    for o_ref, w_ref, b in ((ona_ref, wna_ref, 0), (osw_ref, wsw_ref, 1), (omx_ref, wmx_ref, 2)):
        gate = jax.nn.sigmoid(jnp.dot(h, wg_ref[b], preferred_element_type=F32))
        term = gate * jnp.dot(o_ref[...], w_ref[...], preferred_element_type=F32)
        merged = term if merged is None else merged + term
    part = jnp.dot(merged.astype(BF16), wout_ref[...], preferred_element_type=F32)
    for c in strips:
        z_ref[:, c] += _to_row_tiles(part[:, c * LANES:(c + 1) * LANES])

    @pl.when(j == pl.num_programs(1) - 1)
    def _():
        x2 = jnp.concatenate([_from_row_tiles(z_ref[:, c]) for c in strips], axis=1)
        h2 = _rms(x2, gffn_ref[...])
        for c in strips:
            z_ref[:, nstrip + c] = _to_row_tiles(h2[:, c * LANES:(c + 1) * LANES])
            z_ref[:, 2 * nstrip + c] = jnp.zeros((z_ref.shape[0], SUBLANES, LANES), F32)
        hi = h2.astype(BF16)
        lo = (h2 - hi.astype(F32)).astype(BF16)
        wh = wrh_ref[...]
        logits = (lax.dot_general(wh, hi, NT_DIMS, preferred_element_type=F32)
                  + lax.dot_general(wh, lo, NT_DIMS, preferred_element_type=F32)
                  + lax.dot_general(wrl_ref[...], hi, NT_DIMS, preferred_element_type=F32))
        m = jnp.max(logits, axis=0, keepdims=True)
        e = jnp.exp(logits - m)
        aff = e / jnp.sum(e, axis=0, keepdims=True)
        for c in range(aff_ref.shape[0]):
            aff_ref[c] = aff[:, c * LANES:(c + 1) * LANES]


def merge(x, o_na, o_sw, o_mx, g_mix, w_gate, w_na_o, w_sw_o, w_mx_o, w_out, g_ffn, wr_hi, wr_lo,
          tm=512, tn=256):
    n, d = x.shape
    tm = min(tm, n)
    assert n % tm == 0 and tm % LANES == 0 and d % LANES == 0
    ne = wr_hi.shape[0]
    row = lambda i, j: (i, 0)
    return pl.pallas_call(
        _merge_kernel,
        grid=(n // tm, d // tn),
        in_specs=[pl.BlockSpec((tm, d), row),
                  pl.BlockSpec((tm, o_na.shape[1]), row),
                  pl.BlockSpec((tm, o_sw.shape[1]), row),
                  pl.BlockSpec((tm, o_mx.shape[1]), row),
                  pl.BlockSpec((1, d), lambda i, j: (0, 0)),
                  pl.BlockSpec((N_BRANCHES, d, tn), lambda i, j: (0, 0, j)),
                  pl.BlockSpec((w_na_o.shape[0], tn), lambda i, j: (0, j)),
                  pl.BlockSpec((w_sw_o.shape[0], tn), lambda i, j: (0, j)),
                  pl.BlockSpec((w_mx_o.shape[0], tn), lambda i, j: (0, j)),
                  pl.BlockSpec((tn, d), lambda i, j: (j, 0)),
                  pl.BlockSpec((1, d), lambda i, j: (0, 0)),
                  pl.BlockSpec((ne, d), lambda i, j: (0, 0)),
                  pl.BlockSpec((ne, d), lambda i, j: (0, 0))],
        out_specs=[pl.BlockSpec(_row_tile_shape(tm, 3 * d), lambda i, j: (i, 0, 0, 0)),
                   pl.BlockSpec((tm // LANES, ne, LANES), lambda i, j: (i, 0, 0))],
        out_shape=[jax.ShapeDtypeStruct(_row_tile_shape(n, 3 * d), F32),
                   jax.ShapeDtypeStruct((n // LANES, ne, LANES), F32)],
        scratch_shapes=[pltpu.VMEM((tm, d), BF16)],
        compiler_params=_params("parallel", "arbitrary"),
        name="merge",
    )(x, o_na, o_sw, o_mx, g_mix, w_gate, w_na_o, w_sw_o, w_mx_o, w_out, g_ffn, wr_hi, wr_lo)


NA_WIDTH = NA_HEADS * NA_HEAD_DIM
SW_WIDTH = SW_HEADS * SW_HEAD_DIM
SW_KV_WIDTH = SW_KV_HEADS * SW_HEAD_DIM
QKV_WIDTHS = (NA_WIDTH, NA_WIDTH, NA_WIDTH, SW_WIDTH, SW_KV_WIDTH, SW_KV_WIDTH)


def prep_weights(g_mix, g_mem, w_in, w_mem_kv, na_rpb, t5_table, sw_sink, w_na_o, w_sw_o, w_mx_o, w_out,
                 g_ffn, w_router):
    d = w_in.shape[0]
    mx_width = w_mx_o.shape[0]
    widths = QKV_WIDTHS + (mx_width,)
    offs = np.concatenate([[0], np.cumsum(widths)])
    cols = [w_in[:, offs[i]:offs[i + 1]] for i in range(len(widths))]
    cols[3] = sw_pair_heads(cols[3], axis=1)
    w_gate = w_in[:, offs[-1]:].reshape(d, N_BRANCHES, d).transpose(1, 0, 2)
    wr = w_router.T.astype(F32)
    wr_hi = wr.astype(BF16)
    return dict(
        g_mix=g_mix.reshape(1, d).astype(F32), g_mem=g_mem.reshape(1, d).astype(F32),
        g_ffn=g_ffn.reshape(1, d).astype(F32),
        w_qkv=jnp.concatenate(cols, axis=1).astype(BF16), qkv_widths=widths,
        qkv_scales=(NA_HEAD_DIM ** -0.5, 1.0, 1.0, SW_HEAD_DIM ** -0.5, 1.0, 1.0, 1.0),
        w_mem_kv=w_mem_kv.astype(BF16),
        na_bias=na_bias_table(na_rpb), sw_bias=sw_bias_table(t5_table), sw_sink=sw_sink.astype(F32),
        w_gate=w_gate.astype(BF16), w_na_o=w_na_o.astype(BF16), w_sw_o=sw_pair_heads(w_sw_o, axis=0).astype(BF16),
        w_mx_o=w_mx_o.astype(BF16), w_out=w_out.astype(BF16),
        wr_hi=wr_hi, wr_lo=(wr - wr_hi.astype(F32)).astype(BF16))


def layer_front(x, mem, w, debug=False):
    b, t, d = x.shape
    m = mem.shape[1]
    xf = x.reshape(b * t, d)
    na_q, na_k, na_v, sw_q, sw_k, sw_v, mx_q = norm_proj(xf, w["g_mix"], w["w_qkv"], w["qkv_widths"],
                                                         w["qkv_scales"], tm=512)
    (mkv,) = norm_proj(mem.reshape(b * m, d), w["g_mem"], w["w_mem_kv"], (w["w_mem_kv"].shape[1],), (1.0,),
                       tm=512)
    r3 = lambda a: a.reshape(b, t, a.shape[1])
    o_na = na_attention(r3(na_q), r3(na_k), r3(na_v), w["na_bias"])
    o_sw = sw_attention(r3(sw_q), r3(sw_k), r3(sw_v), w["sw_bias"], w["sw_sink"])
    o_mx = mx_attention(r3(mx_q), mkv.reshape(b, m, mkv.shape[1]))
    f2 = lambda a: a.reshape(b * t, a.shape[2])
    z, aff = merge(xf, f2(o_na), f2(o_sw), f2(o_mx), w["g_mix"], w["w_gate"], w["w_na_o"], w["w_sw_o"],
                   w["w_mx_o"], w["w_out"], w["g_ffn"], w["wr_hi"], w["wr_lo"])
    if debug:
        return o_na, o_sw, o_mx, z, aff
    return z, aff


def _select_kernel(aff_ref, sel_ref, pos_ref, off_ref, *, cap):
    nc, ne, _ = aff_ref.shape
    bits = lax.bitcast_convert_type(aff_ref[...], I32)
    tok = (lax.broadcasted_iota(I32, bits.shape, 0) * LANES + lax.broadcasted_iota(I32, bits.shape, 2))

    def count(flags):
        return jnp.sum(flags, axis=(0, 2), keepdims=True)

    def value_step(i, prefix):
        cand = prefix | lax.shift_left(jnp.int32(1), 30 - i)
        cnt = count(jnp.where(bits >= cand, 1.0, 0.0))
        return jnp.where(cnt >= cap, cand, prefix)

    tau = lax.fori_loop(0, 31, value_step, jnp.zeros((1, ne, 1), I32))
    gt = bits > tau
    eq = bits == tau
    need = cap - count(jnp.where(gt, 1.0, 0.0))

    def index_step(i, last):
        cand = last | lax.shift_left(jnp.int32(1), 15 - i)
        cnt = count(jnp.where(eq, jnp.where(tok < cand, 1.0, 0.0), 0.0))
        return jnp.where(cnt < need, cand, last)

    last = lax.fori_loop(0, 16, index_step, jnp.zeros((1, ne, 1), I32))
    sel_ref[...] = jnp.where(gt, 1.0, jnp.where(eq, jnp.where(tok <= last, 1.0, 0.0), 0.0))

    upper = (lax.broadcasted_iota(I32, (LANES, LANES), 0) <= lax.broadcasted_iota(I32, (LANES, LANES), 1))
    upper = jnp.where(upper, 1.0, 0.0).astype(BF16)

    def chunk_step(c, off):
        s = sel_ref[c]
        cum = jnp.dot(s.astype(BF16), upper, preferred_element_type=F32)
        pos_ref[c] = off + cum - s
        off_ref[c] = jnp.broadcast_to(off, s.shape)
        return off + cum[:, LANES - 1:LANES]

    lax.fori_loop(0, nc, chunk_step, jnp.zeros((ne, 1), F32), unroll=CHUNK_UNROLL)


def select(aff, cap):
    nc, ne, _ = aff.shape
    assert nc * LANES <= 65536
    shape = jax.ShapeDtypeStruct(aff.shape, F32)
    return pl.pallas_call(
        functools.partial(_select_kernel, cap=float(cap)),
        out_shape=[shape, shape, shape],
        compiler_params=pltpu.CompilerParams(vmem_limit_bytes=VMEM_LIMIT),
        name="select",
    )(aff)


TOK_COL, CHUNK_COL, HI_COL = 0, 1, 2


def _compact_kernel(choff_ref, sel_ref, pos_ref, aff_ref, o_ref, tv_scr):
    e = pl.program_id(0)
    nc, ne, _ = aff_ref.shape
    win = 2 * LANES
    col = lax.broadcasted_iota(I32, (LANES, LANES), 1)
    row = lax.broadcasted_iota(I32, (LANES, LANES), 0)

    @pl.when(e == 0)
    def _():
        ecol = lax.broadcasted_iota(I32, (ne, LANES), 1) - lax.broadcasted_iota(I32, (ne, LANES), 0)
        place = [jnp.where(ecol == HI_COL + k * ne, 1.0, 0.0).astype(BF16) for k in range(3)]

        def build(c, carry):
            a = aff_ref[c]
            hi = a.astype(BF16)
            r1 = a - hi.astype(F32)
            mid = r1.astype(BF16)
            lo = (r1 - mid.astype(F32)).astype(BF16)
            rec = (lax.dot_general(hi, place[0], TN_DIMS, preferred_element_type=F32)
                   + lax.dot_general(mid, place[1], TN_DIMS, preferred_element_type=F32)
                   + lax.dot_general(lo, place[2], TN_DIMS, preferred_element_type=F32))
            rec = rec + jnp.where(col == TOK_COL, row, jnp.where(col == CHUNK_COL, c, 0)).astype(F32)
            tv_scr[c] = rec.astype(BF16)
            return carry

        lax.fori_loop(0, nc, build, 0, unroll=CHUNK_UNROLL)

    o_ref[...] = jnp.zeros_like(o_ref)
    slot0 = lax.broadcasted_iota(I32, (win, LANES), 0).astype(F32)

    def body(c, carry):
        off = choff_ref[c * ne + e]
        base = pl.multiple_of((off // LANES) * LANES, LANES)
        s = sel_ref[c, pl.ds(e, 1), :]
        p = pos_ref[c, pl.ds(e, 1), :] - base.astype(F32)
        onehot = jnp.where(s > 0.0, jnp.where(slot0 == p, 1.0, 0.0), 0.0).astype(BF16)
        o_ref[0, pl.ds(base, win), :] += jnp.dot(onehot, tv_scr[c], preferred_element_type=F32)
        return carry

    lax.fori_loop(0, nc, body, 0, unroll=CHUNK_UNROLL)


def compact(sel, pos, choff, aff, cap):
    nc, ne, _ = aff.shape
    assert nc <= 256 and HI_COL + 3 * ne <= LANES and cap % LANES == 0
    rows = cap + 2 * LANES
    full = pl.BlockSpec(aff.shape, lambda e, s: (0, 0, 0))
    rec = pl.pallas_call(
        _compact_kernel,
        grid_spec=pltpu.PrefetchScalarGridSpec(
            num_scalar_prefetch=1, grid=(ne,),
            in_specs=[full, full, full],
            out_specs=pl.BlockSpec((1, rows, LANES), lambda e, s: (e, 0, 0)),
            scratch_shapes=[pltpu.VMEM((nc, LANES, LANES), BF16)]),
        out_shape=jax.ShapeDtypeStruct((ne, rows, LANES), F32),
        compiler_params=_params("arbitrary"),
        name="compact",
    )(choff, sel, pos, aff)
    idx = (rec[:, :cap, CHUNK_COL] * LANES + rec[:, :cap, TOK_COL]).astype(I32)
    return idx, rec


def _expert_kernel(idx_prev, idx_cur, idx_next, rec_ref, zin_ref, wg_ref, wu_ref, wd_ref, z_ref,
                   buf, xs, acc, gsem, ssem, *, ts, nf, s_tiles, ne):
    del zin_ref
    sub = z_ref.shape[1] // 3
    t = pl.program_id(0)
    f = pl.program_id(1)
    last_t = pl.num_programs(0) - 1
    rps = ts // nf
    slot, nslot, pslot = t % 3, (t + 1) % 3, (t + 2) % 3
    odd = (t // s_tiles) % 2
    nodd = (jnp.minimum(t + 1, last_t) // s_tiles) % 2
    podd = (jnp.maximum(t - 1, 0) // s_tiles) % 2

    def rows(ref, row, first, count):
        return ref.at[row >> 3, pl.ds(first, count), pl.ds(row & (SUBLANES - 1), 1), :]

    def gather(token, ra, rs, to_slot, is_odd):
        return pltpu.make_async_copy(rows(z_ref, token, is_odd * sub, 2 * sub),
                                     buf.at[to_slot, ra, :, pl.ds(rs, 1), :], gsem.at[to_slot])

    def scatter(token, ra, rs, from_slot, is_odd):
        return pltpu.make_async_copy(buf.at[from_slot, ra, pl.ds(is_odd * sub, sub), pl.ds(rs, 1), :],
                                     rows(z_ref, token, is_odd * 2 * sub, sub), ssem.at[from_slot])

    def wait_gather(s):
        pltpu.make_async_copy(z_ref.at[pl.ds(0, ts // SUBLANES), pl.ds(0, 2 * sub)], buf.at[s], gsem.at[s]).wait()

    def wait_scatter(s):
        pltpu.make_async_copy(buf.at[s, :, pl.ds(0, sub)], z_ref.at[pl.ds(0, ts // SUBLANES), pl.ds(0, sub)],
                              ssem.at[s]).wait()

    def for_rows(fn):
        def body(r, carry):
            fn(idx_cur[0, 0, r], r >> 3, r & (SUBLANES - 1))
            return carry
        lax.fori_loop(0, ts, body, 0)

    @pl.when(f == 0)
    def _():
        @pl.when(t == 0)
        def _():
            def start(token, ra, rs):
                gather(token, ra, rs, 0, 0).start()
                gather(token, ra, rs, 2, 0).start()
            for_rows(start)
            wait_gather(2)

        acc[...] = jnp.zeros(acc.shape, F32)
        wait_gather(slot)
        h2_first = (1 - odd) * sub
        for c in range(sub):
            xs[:, c * LANES:(c + 1) * LANES] = _from_row_tiles(buf[slot, :, h2_first + c]).astype(BF16)

    for k in range(rps):
        r = f * rps + k
        ra, rs = f * (rps // SUBLANES) + k // SUBLANES, k % SUBLANES
        gather(idx_next[0, 0, r], ra, rs, nslot, nodd).start()
        scatter(idx_prev[0, 0, r], ra, rs, pslot, podd).start()

    x = xs[...]
    a = jnp.dot(x, wg_ref[0], preferred_element_type=F32)
    b = jnp.dot(x, wu_ref[0], preferred_element_type=F32)
    hm = (jax.nn.silu(a) * b).astype(BF16)
    acc[...] += jnp.dot(hm, wd_ref[0], preferred_element_type=F32)

    @pl.when(f == nf - 1)
    def _():
        lane = lax.broadcasted_iota(I32, (1, LANES), 1) - (HI_COL + t // s_tiles)
        own = (lane == 0) | (lane == ne) | (lane == 2 * ne)
        contrib = acc[...] * jnp.sum(jnp.where(own, rec_ref[0], 0.0), axis=1, keepdims=True)
        y_first = odd * sub
        for c in range(sub):
            buf[slot, :, y_first + c] += _to_row_tiles(contrib[:, c * LANES:(c + 1) * LANES])
        wait_scatter(pslot)

        @pl.when(t == last_t)
        def _():
            wait_gather(nslot)
            for_rows(lambda token, ra, rs: scatter(token, ra, rs, slot, odd).start())
            wait_scatter(slot)


def expert_ffn(idx, rec, z, w_gate, w_up, w_down, tf=512):
    ne, cap = idx.shape
    d = z.shape[1] // 3 * LANES
    ff = w_gate.shape[2]
    ts = EXPERT_TS
    tf = min(tf, ff)
    nf = ff // tf
    s_tiles = cap // ts
    assert cap % ts == 0 and s_tiles >= 2 and ts % nf == 0
    nt = ne * s_tiles
    idx3 = idx.reshape(nt, 1, ts)
    smem_tile = lambda shift: pl.BlockSpec(
        (1, 1, ts), lambda t, f: (jnp.clip(t + shift, 0, nt - 1), 0, 0), memory_space=pltpu.SMEM)
    return pl.pallas_call(
        functools.partial(_expert_kernel, ts=ts, nf=nf, s_tiles=s_tiles, ne=ne),
        grid=(nt, nf),
        in_specs=[smem_tile(-1), smem_tile(0), smem_tile(1),
                  pl.BlockSpec((1, ts, LANES), lambda t, f: (t // s_tiles, t % s_tiles, 0)),
                  pl.BlockSpec(memory_space=pl.ANY),
                  pl.BlockSpec((1, d, tf), lambda t, f: (t // s_tiles, 0, f)),
                  pl.BlockSpec((1, d, tf), lambda t, f: (t // s_tiles, 0, f)),
                  pl.BlockSpec((1, tf, d), lambda t, f: (t // s_tiles, f, 0))],
        out_specs=pl.BlockSpec(memory_space=pl.ANY),
        out_shape=jax.ShapeDtypeStruct(z.shape, F32),
        scratch_shapes=[pltpu.VMEM((3,) + _row_tile_shape(ts, 2 * d), F32), pltpu.VMEM((ts, d), BF16),
                        pltpu.VMEM((ts, d), F32), pltpu.SemaphoreType.DMA((3,)), pltpu.SemaphoreType.DMA((3,))],
        input_output_aliases={4: 0},
        compiler_params=_params("arbitrary", "arbitrary"),
        name="expert_ffn",
    )(idx3, idx3, idx3, rec, z, w_gate, w_up, w_down)


def _final_norm_kernel(y0_ref, y1_ref, g_ref, o_ref):
    x = jnp.concatenate([_from_row_tiles(y0_ref[:, c] + y1_ref[:, c]) for c in range(y0_ref.shape[1])], axis=1)
    o_ref[...] = _rms(x, g_ref[...])


def final_norm(z, g, tm=512):
    n, d = z.shape[0] * SUBLANES, z.shape[1] // 3 * LANES
    tm = min(tm, n)
    assert n % tm == 0
    return pl.pallas_call(
        _final_norm_kernel,
        grid=(n // tm,),
        in_specs=[pl.BlockSpec(_row_tile_shape(tm, d), lambda i: (i, 0, 0, 0)),
                  pl.BlockSpec(_row_tile_shape(tm, d), lambda i: (i, 2, 0, 0)),
                  pl.BlockSpec((1, d), lambda i: (0, 0))],
        out_specs=pl.BlockSpec((tm, d), lambda i: (i, 0)),
        out_shape=jax.ShapeDtypeStruct((n, d), F32),
        compiler_params=_params("parallel"),
        name="final_norm",
    )(z, z, g)


def moe_and_norm(z, aff, w_e_gate, w_e_up, w_e_down, g_final):
    n = z.shape[0] * SUBLANES
    cap = EC_CAPACITY * n // N_EXPERTS
    sel, pos, off = select(aff, cap)
    choff = off[:, :, 0].astype(I32).reshape(-1)
    idx, rec = compact(sel, pos, choff, aff, cap)
    z = expert_ffn(idx, rec, z, w_e_gate, w_e_up, w_e_down)
    return final_norm(z, g_final)


def encoder_group(x, mem, w, experts, g_final):
    b, t, d = x.shape
    z, aff = layer_front(x, mem, w)
    return moe_and_norm(z, aff, *experts, g_final).reshape(b, t, d)


def kernel(x_prompt, x_sample, mem_prompt, mem_sample, g_mix, g_mem, w_in, w_mem_kv, na_rpb, t5_table, sw_sink,
           w_na_o, w_sw_o, w_mx_o, w_out, g_ffn, w_router, w_e_gate, w_e_up, w_e_down, g_final):
    assert g_mix.shape[0] == 1, "single-layer trunk"
    w = prep_weights(g_mix[0], g_mem[0], w_in[0], w_mem_kv[0], na_rpb[0], t5_table, sw_sink[0], w_na_o[0],
                     w_sw_o[0], w_mx_o[0], w_out[0], g_ffn[0], w_router[0])
    experts = (w_e_gate[0].astype(BF16), w_e_up[0].astype(BF16), w_e_down[0].astype(BF16))
    gf = g_final.reshape(1, -1).astype(F32)
    y_prompt = encoder_group(x_prompt, mem_prompt, w, experts, gf)
    y_sample = encoder_group(x_sample, mem_sample, w, experts, gf)
    return (y_prompt, y_sample)
```

```python
import functools
import math

import numpy as np
import jax
import jax.numpy as jnp
from jax import lax
from jax.experimental import pallas as pl
from jax.experimental.pallas import tpu as pltpu

F32 = jnp.float32
BF16 = jnp.bfloat16
I32 = jnp.int32

RMS_EPS = 1e-6
NEG_INF = -1e30

GRID_W = 64
NA_HEADS = 8
NA_HEAD_DIM = 64
NA_KR = 8
NA_KC = 16
SW_HEADS = 16
SW_KV_HEADS = 4
SW_HEAD_DIM = 64
SW_WINDOW = 128
SW_BLOCK = 128
MX_HEADS = 4
T5_BUCKETS = 32
T5_MAX_DIST = 128
N_BRANCHES = 3
N_EXPERTS = 16
EC_CAPACITY = 2

LANES = 128
SUBLANES = 8
V7X_VMEM_BYTES = 64 * 1024 * 1024
VMEM_LIMIT = V7X_VMEM_BYTES * 7 // 8

CHUNK_UNROLL = 8
EXPERT_TS = 512

NT_DIMS = (((1,), (1,)), ((), ()))
TN_DIMS = (((0,), (0,)), ((), ()))


VMEM_LIMIT_MERGE = V7X_VMEM_BYTES * 15 // 16


def _params(*sem, vmem_limit=VMEM_LIMIT):
    return pltpu.CompilerParams(dimension_semantics=sem, vmem_limit_bytes=vmem_limit)


def _rms(x, g):
    return x * lax.rsqrt(jnp.mean(x * x, axis=-1, keepdims=True) + RMS_EPS) * g


def _to_row_tiles(strip):
    return strip.reshape(strip.shape[0] // SUBLANES, SUBLANES, LANES)


def _from_row_tiles(tiles):
    return tiles.reshape(tiles.shape[0] * SUBLANES, LANES)


def _row_tile_shape(rows, d):
    return (rows // SUBLANES, d // LANES, SUBLANES, LANES)


def _norm_proj_kernel(x_ref, g_ref, w_ref, *o_refs, scales):
    h = _rms(x_ref[...], g_ref[...]).astype(BF16)
    off = 0
    for o_ref, sc in zip(o_refs, scales):
        width = o_ref.shape[1]
        for c0 in range(0, width, 512):
            cw = min(512, width - c0)
            r = jnp.dot(h, w_ref[:, off + c0:off + c0 + cw], preferred_element_type=F32)
            if sc != 1.0:
                r = r * sc
            o_ref[:, c0:c0 + cw] = r.astype(o_ref.dtype)
        off += width


def norm_proj(x, g, w, widths, scales, tm):
    n, d = x.shape
    tm = min(tm, n)
    assert n % tm == 0
    return pl.pallas_call(
        functools.partial(_norm_proj_kernel, scales=tuple(scales)),
        grid=(n // tm,),
        in_specs=[pl.BlockSpec((tm, d), lambda i: (i, 0)),
                  pl.BlockSpec((1, d), lambda i: (0, 0)),
                  pl.BlockSpec(w.shape, lambda i: (0, 0))],
        out_specs=[pl.BlockSpec((tm, c), lambda i: (i, 0)) for c in widths],
        out_shape=[jax.ShapeDtypeStruct((n, c), BF16) for c in widths],
        compiler_params=_params("parallel"),
        name="norm_proj",
    )(x, g, w)


def _softmax_pv(s, v, extra_logit=None):
    m = jnp.max(s, axis=-1, keepdims=True)
    if extra_logit is not None:
        m = jnp.maximum(m, extra_logit)
    e = jnp.exp(s - m)
    den = jnp.sum(e, axis=-1, keepdims=True)
    if extra_logit is not None:
        den = den + jnp.exp(extra_logit - m)
    o = jnp.dot(e.astype(BF16), v, preferred_element_type=F32)
    return o / den


def _softmax_pv_half(s, v, extra_logit, use_lo):
    rows, keys = s.shape
    mb = jnp.maximum(jnp.broadcast_to(jnp.max(s, axis=-1, keepdims=True), (rows, LANES)), extra_logit)
    e = jnp.concatenate([jnp.exp(s[:, t * LANES:(t + 1) * LANES] - mb) for t in range(keys // LANES)],
                        axis=1).astype(BF16)
    lo = lax.broadcasted_iota(I32, v.shape, 1) < LANES // 2
    keep = lo if use_lo else jnp.logical_not(lo)
    o = jnp.dot(e, jnp.where(keep, v, jnp.ones_like(v)), preferred_element_type=F32)
    den = pltpu.roll(o + jnp.exp(extra_logit - mb), LANES // 2, 1)
    return o / den


def _na_kernel(q_ref, k_ref, v_ref, bias_ref, o_ref, *, rows, rb):
    j = pl.program_id(1)
    lo = lax.broadcasted_iota(I32, (GRID_W, LANES), 1) < NA_HEAD_DIM
    nkeys = NA_KR * GRID_W

    def body(i, carry):
        r = j * rb + i
        rs = jnp.clip(r - NA_KR // 2, 0, rows - NA_KR)
        off = r - rs
        q = q_ref[0, pl.ds(pl.multiple_of(i * GRID_W, GRID_W), GRID_W), :]
        kk = k_ref[0, pl.ds(pl.multiple_of(rs * GRID_W, GRID_W), nkeys), :]
        vv = v_ref[0, pl.ds(pl.multiple_of(rs * GRID_W, GRID_W), nkeys), :]
        scores = []
        for p in range(NA_HEADS // 2):
            qp = q[:, p * LANES:(p + 1) * LANES]
            kp = kk[:, p * LANES:(p + 1) * LANES]
            zero = jnp.zeros_like(qp)
            q2 = jnp.concatenate([jnp.where(lo, qp, zero), jnp.where(lo, zero, qp)], axis=0)
            s = lax.dot_general(q2, kp, NT_DIMS, preferred_element_type=F32)
            scores.append(s + bias_ref[off, 2 * p:2 * p + 2].reshape(2 * GRID_W, nkeys))
        outs = []
        for p in range(NA_HEADS // 2):
            o2 = _softmax_pv(scores[p], vv[:, p * LANES:(p + 1) * LANES])
            outs.append(jnp.where(lo, o2[:GRID_W], o2[GRID_W:]))
        o_ref[0, pl.ds(pl.multiple_of(i * GRID_W, GRID_W), GRID_W), :] = (
            jnp.concatenate(outs, axis=1).astype(o_ref.dtype))
        return carry

    lax.fori_loop(0, rb, body, 0, unroll=2)


def na_attention(q, k, v, bias, rb=8):
    b, t, c = q.shape
    rows = t // GRID_W
    assert rows >= NA_KR and rows % rb == 0
    return pl.pallas_call(
        functools.partial(_na_kernel, rows=rows, rb=rb),
        grid=(b, rows // rb),
        in_specs=[pl.BlockSpec((1, rb * GRID_W, c), lambda i, j: (i, j, 0)),
                  pl.BlockSpec((1, t, c), lambda i, j: (i, 0, 0)),
                  pl.BlockSpec((1, t, c), lambda i, j: (i, 0, 0)),
                  pl.BlockSpec(bias.shape, lambda i, j: (0, 0, 0, 0))],
        out_specs=pl.BlockSpec((1, rb * GRID_W, c), lambda i, j: (i, j, 0)),
        out_shape=jax.ShapeDtypeStruct((b, t, c), BF16),
        compiler_params=_params("parallel", "arbitrary"),
        name="na_attention",
    )(q, k, v, bias)


def na_bias_table(rpb):
    col = np.arange(GRID_W)
    col_start = np.clip(col - NA_KC // 2, 0, GRID_W - NA_KC)
    in_win = (col[None, :] >= col_start[:, None]) & (col[None, :] < col_start[:, None] + NA_KC)
    dc = np.clip(col[None, :] - col[:, None] + NA_KC - 1, 0, 2 * NA_KC - 2)
    heads = rpb.shape[0]
    by_col = jnp.take(rpb.astype(F32), jnp.asarray(dc.reshape(-1)), axis=2)
    by_col = by_col.reshape(heads, 2 * NA_KR - 1, GRID_W, GRID_W)
    by_col = jnp.where(jnp.asarray(in_win)[None, None], by_col, NEG_INF)
    per_off = [by_col[:, NA_KR - 1 - off:2 * NA_KR - 1 - off].transpose(0, 2, 1, 3) for off in range(NA_KR)]
    return jnp.stack(per_off).reshape(NA_KR, heads, GRID_W, NA_KR * GRID_W)


def _sw_kernel(sink_ref, q_ref, k_ref, v_ref, bias_ref, o_ref, s_scr, *, nb):
    n = pl.program_id(1)
    blk = SW_BLOCK
    group = SW_HEADS // SW_KV_HEADS

    def rows_of(ref, c):
        return ref[0, pl.ds(pl.multiple_of(c * blk, blk), blk), :]

    cl = jnp.maximum(n - 1, 0)
    cr = jnp.minimum(n + 1, nb - 1)
    k3 = jnp.concatenate([rows_of(k_ref, cl), rows_of(k_ref, n), rows_of(k_ref, cr)], axis=0)
    v3 = jnp.concatenate([rows_of(v_ref, cl), rows_of(v_ref, n), rows_of(v_ref, cr)], axis=0)
    pen_l = jnp.where(n > 0, 0.0, NEG_INF).astype(F32)
    pen_r = jnp.where(n < nb - 1, 0.0, NEG_INF).astype(F32)
    key = lax.broadcasted_iota(I32, (1, 3 * blk), 1)
    pen = jnp.where(key < blk, pen_l, jnp.where(key >= 2 * blk, pen_r, 0.0))
    lo = lax.broadcasted_iota(I32, (blk, LANES), 1) < SW_HEAD_DIM

    for c in range(SW_KV_HEADS):
        pair, half = divmod(c, 2)
        kp = k3[:, pair * LANES:(pair + 1) * LANES]
        keep = lo if half == 0 else jnp.logical_not(lo)
        qs = []
        for g in range(group):
            t = pair * group + g
            qt = q_ref[0, :, t * LANES:(t + 1) * LANES]
            qs.append(jnp.where(keep, qt, jnp.zeros_like(qt)))
        qq = jnp.concatenate(qs, axis=0)
        s = lax.dot_general(qq, kp, NT_DIMS, preferred_element_type=F32)
        s_scr[c] = s + bias_ref[c * group:(c + 1) * group].reshape(group * blk, 3 * blk) + pen

    for pair in range(SW_KV_HEADS // 2):
        vp = v3[:, pair * LANES:(pair + 1) * LANES]
        per_half = []
        for half in range(2):
            c = 2 * pair + half
            sink = jnp.concatenate(
                [jnp.full((blk, LANES), sink_ref[c * group + g], F32) for g in range(group)], axis=0)
            per_half.append(_softmax_pv_half(s_scr[c], vp, sink, use_lo=(half == 0)))
        for g in range(group):
            t = pair * group + g
            o = jnp.where(lo, per_half[0][g * blk:(g + 1) * blk], per_half[1][g * blk:(g + 1) * blk])
            o_ref[0, :, t * LANES:(t + 1) * LANES] = o.astype(o_ref.dtype)


def sw_attention(q, k, v, bias, sink):
    b, t, c = q.shape
    nb = t // SW_BLOCK
    kvw = k.shape[2]
    grid_spec = pltpu.PrefetchScalarGridSpec(
        num_scalar_prefetch=1,
        grid=(b, nb),
        in_specs=[pl.BlockSpec((1, SW_BLOCK, c), lambda i, j, s: (i, j, 0)),
                  pl.BlockSpec((1, t, kvw), lambda i, j, s: (i, 0, 0)),
                  pl.BlockSpec((1, t, kvw), lambda i, j, s: (i, 0, 0)),
                  pl.BlockSpec(bias.shape, lambda i, j, s: (0, 0, 0))],
        out_specs=pl.BlockSpec((1, SW_BLOCK, c), lambda i, j, s: (i, j, 0)),
        scratch_shapes=[pltpu.VMEM((SW_KV_HEADS, (SW_HEADS // SW_KV_HEADS) * SW_BLOCK, 3 * SW_BLOCK), F32)],
    )
    return pl.pallas_call(
        functools.partial(_sw_kernel, nb=nb),
        grid_spec=grid_spec,
        out_shape=jax.ShapeDtypeStruct((b, t, c), BF16),
        compiler_params=_params("parallel", "arbitrary"),
        name="sw_attention",
    )(sink, q, k, v, bias)


def _t5_bucket(rel):
    half = T5_BUCKETS // 2
    max_exact = half // 2
    n = jnp.abs(rel)
    nf = jnp.maximum(n, 1).astype(F32)
    large = max_exact + (jnp.log(nf / max_exact) / math.log(T5_MAX_DIST / max_exact)
                         * (half - max_exact)).astype(jnp.int32)
    large = jnp.minimum(large, half - 1)
    return jnp.where(rel > 0, half, 0) + jnp.where(n < max_exact, n, large)


def sw_bias_table(t5_table):
    span = SW_BLOCK + 2 * SW_WINDOW
    rel = np.arange(span)[None, :] - SW_WINDOW - np.arange(SW_BLOCK)[:, None]
    bucket = _t5_bucket(jnp.asarray(rel, dtype=jnp.int32))[None]
    table = t5_table.astype(F32).T[:, :, None, None]
    bias = jnp.zeros((t5_table.shape[1], SW_BLOCK, span), F32)
    for b in range(T5_BUCKETS):
        bias = jnp.where(bucket == b, table[:, b], bias)
    return jnp.where(jnp.asarray(np.abs(rel) <= SW_WINDOW)[None], bias, NEG_INF)


def sw_pair_heads(w, axis):
    group = SW_HEADS // SW_KV_HEADS
    shape = w.shape
    split = shape[:axis] + (SW_KV_HEADS // 2, 2, group, SW_HEAD_DIM) + shape[axis + 1:]
    order = list(range(len(split)))
    order[axis + 1], order[axis + 2] = axis + 2, axis + 1
    return w.reshape(split).transpose(order).reshape(shape)


def _mx_kernel(q_ref, mk_ref, mv_ref, o_ref, *, scale):
    hd = q_ref.shape[2] // MX_HEADS
    heads = [slice(h * hd, (h + 1) * hd) for h in range(MX_HEADS)]
    scores = [lax.dot_general(q_ref[0, :, sl], mk_ref[0, :, sl], NT_DIMS, preferred_element_type=F32) * scale
              for sl in heads]
    for sl, s in zip(heads, scores):
        o_ref[0, :, sl] = _softmax_pv(s, mv_ref[0, :, sl]).astype(o_ref.dtype)


def mx_attention(q, mkv, tq=512):
    b, t, c = q.shape
    m = mkv.shape[1]
    tq = min(tq, t)
    return pl.pallas_call(
        functools.partial(_mx_kernel, scale=float((c // MX_HEADS) ** -0.5)),
        grid=(b, t // tq),
        in_specs=[pl.BlockSpec((1, tq, c), lambda i, j: (i, j, 0)),
                  pl.BlockSpec((1, m, c), lambda i, j: (i, 0, 0)),
                  pl.BlockSpec((1, m, c), lambda i, j: (i, 0, 1))],
        out_specs=pl.BlockSpec((1, tq, c), lambda i, j: (i, j, 0)),
        out_shape=jax.ShapeDtypeStruct((b, t, c), BF16),
        compiler_params=_params("parallel", "arbitrary"),
        name="mx_attention",
    )(q, mkv, mkv)


def _merge_kernel(x_ref, ona_ref, osw_ref, omx_ref, gmix_ref, wg_ref, wna_ref, wsw_ref, wmx_ref, wout_ref,
                  gffn_ref, wrh_ref, wrl_ref, z_ref, aff_ref, h_scr, m_scr):
    j = pl.program_id(1)
    nstrip = z_ref.shape[1] // 3
    strips = range(nstrip)

    @pl.when(j == 0)
    def _():
        x = x_ref[...]
        h_scr[...] = _rms(x, gmix_ref[...]).astype(BF16)
        for c in strips:
            z_ref[:, c] = _to_row_tiles(x[:, c * LANES:(c + 1) * LANES])

    h = h_scr[...]
    merged = None
    for o_ref, w_ref, b in ((ona_ref, wna_ref, 0), (osw_ref, wsw_ref, 1), (omx_ref, wmx_ref, 2)):
        gate = jax.nn.sigmoid(jnp.dot(h, wg_ref[b], preferred_element_type=F32))
        term = gate * jnp.dot(o_ref[...], w_ref[...], preferred_element_type=F32)
        merged = term if merged is None else merged + term
    m_scr[j] = merged.astype(BF16)

    @pl.when(j == pl.num_programs(1) - 1)
    def _():
        merged_all = jnp.concatenate([m_scr[jj] for jj in range(m_scr.shape[0])], axis=1)
        proj = jnp.dot(merged_all, wout_ref[...], preferred_element_type=F32)
        for c in strips:
            z_ref[:, c] += _to_row_tiles(proj[:, c * LANES:(c + 1) * LANES])
        x2 = jnp.concatenate([_from_row_tiles(z_ref[:, c]) for c in strips], axis=1)
        h2 = _rms(x2, gffn_ref[...])
        for c in strips:
            z_ref[:, nstrip + c] = _to_row_tiles(h2[:, c * LANES:(c + 1) * LANES])
            z_ref[:, 2 * nstrip + c] = jnp.zeros((z_ref.shape[0], SUBLANES, LANES), F32)
        hi = h2.astype(BF16)
        lo = (h2 - hi.astype(F32)).astype(BF16)
        wh = wrh_ref[...]
        logits = (lax.dot_general(wh, hi, NT_DIMS, preferred_element_type=F32)
                  + lax.dot_general(wh, lo, NT_DIMS, preferred_element_type=F32)
                  + lax.dot_general(wrl_ref[...], hi, NT_DIMS, preferred_element_type=F32))
        m = jnp.max(logits, axis=0, keepdims=True)
        e = jnp.exp(logits - m)
        aff = e / jnp.sum(e, axis=0, keepdims=True)
        for c in range(aff_ref.shape[0]):
            aff_ref[c] = aff[:, c * LANES:(c + 1) * LANES]


def merge(x, o_na, o_sw, o_mx, g_mix, w_gate, w_na_o, w_sw_o, w_mx_o, w_out, g_ffn, wr_hi, wr_lo,
          tm=512, tn=256):
    n, d = x.shape
    tm = min(tm, n)
    assert n % tm == 0 and tm % LANES == 0 and d % LANES == 0
    ne = wr_hi.shape[0]
    row = lambda i, j: (i, 0)
    return pl.pallas_call(
        _merge_kernel,
        grid=(n // tm, d // tn),
        in_specs=[pl.BlockSpec((tm, d), row, pipeline_mode=pl.Buffered(1)),
                  pl.BlockSpec((tm, o_na.shape[1]), row),
                  pl.BlockSpec((tm, o_sw.shape[1]), row),
                  pl.BlockSpec((tm, o_mx.shape[1]), row),
                  pl.BlockSpec((1, d), lambda i, j: (0, 0)),
                  pl.BlockSpec((N_BRANCHES, d, tn), lambda i, j: (0, 0, j)),
                  pl.BlockSpec((w_na_o.shape[0], tn), lambda i, j: (0, j)),
                  pl.BlockSpec((w_sw_o.shape[0], tn), lambda i, j: (0, j)),
                  pl.BlockSpec((w_mx_o.shape[0], tn), lambda i, j: (0, j)),
                  pl.BlockSpec((d, d), lambda i, j: (0, 0), pipeline_mode=pl.Buffered(1)),
                  pl.BlockSpec((1, d), lambda i, j: (0, 0)),
                  pl.BlockSpec((ne, d), lambda i, j: (0, 0)),
                  pl.BlockSpec((ne, d), lambda i, j: (0, 0))],
        out_specs=[pl.BlockSpec(_row_tile_shape(tm, 3 * d), lambda i, j: (i, 0, 0, 0)),
                   pl.BlockSpec((tm // LANES, ne, LANES), lambda i, j: (i, 0, 0))],
        out_shape=[jax.ShapeDtypeStruct(_row_tile_shape(n, 3 * d), F32),
                   jax.ShapeDtypeStruct((n // LANES, ne, LANES), F32)],
        scratch_shapes=[pltpu.VMEM((tm, d), BF16), pltpu.VMEM((d // tn, tm, tn), BF16)],
        compiler_params=_params("parallel", "arbitrary", vmem_limit=VMEM_LIMIT_MERGE),
        name="merge",
    )(x, o_na, o_sw, o_mx, g_mix, w_gate, w_na_o, w_sw_o, w_mx_o, w_out, g_ffn, wr_hi, wr_lo)


NA_WIDTH = NA_HEADS * NA_HEAD_DIM
SW_WIDTH = SW_HEADS * SW_HEAD_DIM
SW_KV_WIDTH = SW_KV_HEADS * SW_HEAD_DIM
QKV_WIDTHS = (NA_WIDTH, NA_WIDTH, NA_WIDTH, SW_WIDTH, SW_KV_WIDTH, SW_KV_WIDTH)


def prep_weights(g_mix, g_mem, w_in, w_mem_kv, na_rpb, t5_table, sw_sink, w_na_o, w_sw_o, w_mx_o, w_out,
                 g_ffn, w_router):
    d = w_in.shape[0]
    mx_width = w_mx_o.shape[0]
    widths = QKV_WIDTHS + (mx_width,)
    offs = np.concatenate([[0], np.cumsum(widths)])
    cols = [w_in[:, offs[i]:offs[i + 1]] for i in range(len(widths))]
    cols[3] = sw_pair_heads(cols[3], axis=1)
    w_gate = w_in[:, offs[-1]:].reshape(d, N_BRANCHES, d).transpose(1, 0, 2)
    wr = w_router.T.astype(F32)
    wr_hi = wr.astype(BF16)
    return dict(
        g_mix=g_mix.reshape(1, d).astype(F32), g_mem=g_mem.reshape(1, d).astype(F32),
        g_ffn=g_ffn.reshape(1, d).astype(F32),
        w_qkv=jnp.concatenate(cols, axis=1).astype(BF16), qkv_widths=widths,
        qkv_scales=(NA_HEAD_DIM ** -0.5, 1.0, 1.0, SW_HEAD_DIM ** -0.5, 1.0, 1.0, 1.0),
        w_mem_kv=w_mem_kv.astype(BF16),
        na_bias=na_bias_table(na_rpb), sw_bias=sw_bias_table(t5_table), sw_sink=sw_sink.astype(F32),
        w_gate=w_gate.astype(BF16), w_na_o=w_na_o.astype(BF16), w_sw_o=sw_pair_heads(w_sw_o, axis=0).astype(BF16),
        w_mx_o=w_mx_o.astype(BF16), w_out=w_out.astype(BF16),
        wr_hi=wr_hi, wr_lo=(wr - wr_hi.astype(F32)).astype(BF16))


def layer_front(x, mem, w, debug=False):
    b, t, d = x.shape
    m = mem.shape[1]
    xf = x.reshape(b * t, d)
    na_q, na_k, na_v, sw_q, sw_k, sw_v, mx_q = norm_proj(xf, w["g_mix"], w["w_qkv"], w["qkv_widths"],
                                                         w["qkv_scales"], tm=512)
    (mkv,) = norm_proj(mem.reshape(b * m, d), w["g_mem"], w["w_mem_kv"], (w["w_mem_kv"].shape[1],), (1.0,),
                       tm=512)
    r3 = lambda a: a.reshape(b, t, a.shape[1])
    o_na = na_attention(r3(na_q), r3(na_k), r3(na_v), w["na_bias"])
    o_sw = sw_attention(r3(sw_q), r3(sw_k), r3(sw_v), w["sw_bias"], w["sw_sink"])
    o_mx = mx_attention(r3(mx_q), mkv.reshape(b, m, mkv.shape[1]))
    f2 = lambda a: a.reshape(b * t, a.shape[2])
    z, aff = merge(xf, f2(o_na), f2(o_sw), f2(o_mx), w["g_mix"], w["w_gate"], w["w_na_o"], w["w_sw_o"],
                   w["w_mx_o"], w["w_out"], w["g_ffn"], w["wr_hi"], w["wr_lo"])
    if debug:
        return o_na, o_sw, o_mx, z, aff
    return z, aff


def _select_kernel(aff_ref, sel_ref, pos_ref, off_ref, *, cap):
    nc, ne, _ = aff_ref.shape
    bits = lax.bitcast_convert_type(aff_ref[...], I32)
    tok = (lax.broadcasted_iota(I32, bits.shape, 0) * LANES + lax.broadcasted_iota(I32, bits.shape, 2))

    def count(flags):
        return jnp.sum(flags, axis=(0, 2), keepdims=True)

    def value_step(i, prefix):
        cand = prefix | lax.shift_left(jnp.int32(1), 30 - i)
        cnt = count(jnp.where(bits >= cand, 1.0, 0.0))
        return jnp.where(cnt >= cap, cand, prefix)

    tau = lax.fori_loop(0, 31, value_step, jnp.zeros((1, ne, 1), I32))
    gt = bits > tau
    eq = bits == tau
    need = cap - count(jnp.where(gt, 1.0, 0.0))

    def index_step(i, last):
        cand = last | lax.shift_left(jnp.int32(1), 15 - i)
        cnt = count(jnp.where(eq, jnp.where(tok < cand, 1.0, 0.0), 0.0))
        return jnp.where(cnt < need, cand, last)

    last = lax.fori_loop(0, 16, index_step, jnp.zeros((1, ne, 1), I32))
    sel_ref[...] = jnp.where(gt, 1.0, jnp.where(eq, jnp.where(tok <= last, 1.0, 0.0), 0.0))

    upper = (lax.broadcasted_iota(I32, (LANES, LANES), 0) <= lax.broadcasted_iota(I32, (LANES, LANES), 1))
    upper = jnp.where(upper, 1.0, 0.0).astype(BF16)

    def chunk_step(c, off):
        s = sel_ref[c]
        cum = jnp.dot(s.astype(BF16), upper, preferred_element_type=F32)
        pos_ref[c] = off + cum - s
        off_ref[c] = jnp.broadcast_to(off, s.shape)
        return off + cum[:, LANES - 1:LANES]

    lax.fori_loop(0, nc, chunk_step, jnp.zeros((ne, 1), F32), unroll=CHUNK_UNROLL)


def select(aff, cap):
    nc, ne, _ = aff.shape
    assert nc * LANES <= 65536
    shape = jax.ShapeDtypeStruct(aff.shape, F32)
    return pl.pallas_call(
        functools.partial(_select_kernel, cap=float(cap)),
        out_shape=[shape, shape, shape],
        compiler_params=pltpu.CompilerParams(vmem_limit_bytes=VMEM_LIMIT),
        name="select",
    )(aff)


TOK_COL, CHUNK_COL, HI_COL = 0, 1, 2


def _compact_kernel(choff_ref, sel_ref, pos_ref, aff_ref, o_ref, tv_scr):
    e = pl.program_id(0)
    nc, ne, _ = aff_ref.shape
    win = 2 * LANES
    col = lax.broadcasted_iota(I32, (LANES, LANES), 1)
    row = lax.broadcasted_iota(I32, (LANES, LANES), 0)

    @pl.when(e == 0)
    def _():
        ecol = lax.broadcasted_iota(I32, (ne, LANES), 1) - lax.broadcasted_iota(I32, (ne, LANES), 0)
        place = [jnp.where(ecol == HI_COL + k * ne, 1.0, 0.0).astype(BF16) for k in range(3)]

        def build(c, carry):
            a = aff_ref[c]
            hi = a.astype(BF16)
            r1 = a - hi.astype(F32)
            mid = r1.astype(BF16)
            lo = (r1 - mid.astype(F32)).astype(BF16)
            rec = (lax.dot_general(hi, place[0], TN_DIMS, preferred_element_type=F32)
                   + lax.dot_general(mid, place[1], TN_DIMS, preferred_element_type=F32)
                   + lax.dot_general(lo, place[2], TN_DIMS, preferred_element_type=F32))
            rec = rec + jnp.where(col == TOK_COL, row, jnp.where(col == CHUNK_COL, c, 0)).astype(F32)
            tv_scr[c] = rec.astype(BF16)
            return carry

        lax.fori_loop(0, nc, build, 0, unroll=CHUNK_UNROLL)

    o_ref[...] = jnp.zeros_like(o_ref)
    slot0 = lax.broadcasted_iota(I32, (win, LANES), 0).astype(F32)

    def body(c, carry):
        off = choff_ref[c * ne + e]
        base = pl.multiple_of((off // LANES) * LANES, LANES)
        s = sel_ref[c, pl.ds(e, 1), :]
        p = pos_ref[c, pl.ds(e, 1), :] - base.astype(F32)
        onehot = jnp.where(s > 0.0, jnp.where(slot0 == p, 1.0, 0.0), 0.0).astype(BF16)
        o_ref[0, pl.ds(base, win), :] += jnp.dot(onehot, tv_scr[c], preferred_element_type=F32)
        return carry

    lax.fori_loop(0, nc, body, 0, unroll=CHUNK_UNROLL)


def compact(sel, pos, choff, aff, cap):
    nc, ne, _ = aff.shape
    assert nc <= 256 and HI_COL + 3 * ne <= LANES and cap % LANES == 0
    rows = cap + 2 * LANES
    full = pl.BlockSpec(aff.shape, lambda e, s: (0, 0, 0))
    rec = pl.pallas_call(
        _compact_kernel,
        grid_spec=pltpu.PrefetchScalarGridSpec(
            num_scalar_prefetch=1, grid=(ne,),
            in_specs=[full, full, full],
            out_specs=pl.BlockSpec((1, rows, LANES), lambda e, s: (e, 0, 0)),
            scratch_shapes=[pltpu.VMEM((nc, LANES, LANES), BF16)]),
        out_shape=jax.ShapeDtypeStruct((ne, rows, LANES), F32),
        compiler_params=_params("arbitrary"),
        name="compact",
    )(choff, sel, pos, aff)
    idx = (rec[:, :cap, CHUNK_COL] * LANES + rec[:, :cap, TOK_COL]).astype(I32)
    return idx, rec


def _expert_kernel(idx_prev, idx_cur, idx_next, rec_ref, zin_ref, wg_ref, wu_ref, wd_ref, z_ref,
                   buf, xs, acc, gsem, ssem, *, ts, nf, s_tiles, ne):
    del zin_ref
    sub = z_ref.shape[1] // 3
    t = pl.program_id(0)
    f = pl.program_id(1)
    last_t = pl.num_programs(0) - 1
    rps = ts // nf
    slot, nslot, pslot = t % 3, (t + 1) % 3, (t + 2) % 3
    odd = (t // s_tiles) % 2
    nodd = (jnp.minimum(t + 1, last_t) // s_tiles) % 2
    podd = (jnp.maximum(t - 1, 0) // s_tiles) % 2

    def rows(ref, row, first, count):
        return ref.at[row >> 3, pl.ds(first, count), pl.ds(row & (SUBLANES - 1), 1), :]

    def gather(token, ra, rs, to_slot, is_odd):
        return pltpu.make_async_copy(rows(z_ref, token, is_odd * sub, 2 * sub),
                                     buf.at[to_slot, ra, :, pl.ds(rs, 1), :], gsem.at[to_slot])

    def scatter(token, ra, rs, from_slot, is_odd):
        return pltpu.make_async_copy(buf.at[from_slot, ra, pl.ds(is_odd * sub, sub), pl.ds(rs, 1), :],
                                     rows(z_ref, token, is_odd * 2 * sub, sub), ssem.at[from_slot])

    def wait_gather(s):
        pltpu.make_async_copy(z_ref.at[pl.ds(0, ts // SUBLANES), pl.ds(0, 2 * sub)], buf.at[s], gsem.at[s]).wait()

    def wait_scatter(s):
        pltpu.make_async_copy(buf.at[s, :, pl.ds(0, sub)], z_ref.at[pl.ds(0, ts // SUBLANES), pl.ds(0, sub)],
                              ssem.at[s]).wait()

    def for_rows(fn):
        def body(r, carry):
            fn(idx_cur[0, 0, r], r >> 3, r & (SUBLANES - 1))
            return carry
        lax.fori_loop(0, ts, body, 0)

    @pl.when(f == 0)
    def _():
        @pl.when(t == 0)
        def _():
            def start(token, ra, rs):
                gather(token, ra, rs, 0, 0).start()
                gather(token, ra, rs, 2, 0).start()
            for_rows(start)
            wait_gather(2)

        acc[...] = jnp.zeros(acc.shape, F32)
        wait_gather(slot)
        h2_first = (1 - odd) * sub
        for c in range(sub):
            xs[:, c * LANES:(c + 1) * LANES] = _from_row_tiles(buf[slot, :, h2_first + c]).astype(BF16)

    for k in range(rps):
        r = f * rps + k
        ra, rs = f * (rps // SUBLANES) + k // SUBLANES, k % SUBLANES
        gather(idx_next[0, 0, r], ra, rs, nslot, nodd).start()
        scatter(idx_prev[0, 0, r], ra, rs, pslot, podd).start()

    x = xs[...]
    a = jnp.dot(x, wg_ref[0], preferred_element_type=F32)
    b = jnp.dot(x, wu_ref[0], preferred_element_type=F32)
    hm = (jax.nn.silu(a) * b).astype(BF16)
    acc[...] += jnp.dot(hm, wd_ref[0], preferred_element_type=F32)

    @pl.when(f == nf - 1)
    def _():
        lane = lax.broadcasted_iota(I32, (1, LANES), 1) - (HI_COL + t // s_tiles)
        own = (lane == 0) | (lane == ne) | (lane == 2 * ne)
        contrib = acc[...] * jnp.sum(jnp.where(own, rec_ref[0], 0.0), axis=1, keepdims=True)
        y_first = odd * sub
        for c in range(sub):
            buf[slot, :, y_first + c] += _to_row_tiles(contrib[:, c * LANES:(c + 1) * LANES])
        wait_scatter(pslot)

        @pl.when(t == last_t)
        def _():
            wait_gather(nslot)
            for_rows(lambda token, ra, rs: scatter(token, ra, rs, slot, odd).start())
            wait_scatter(slot)


def expert_ffn(idx, rec, z, w_gate, w_up, w_down, tf=512):
    ne, cap = idx.shape
    d = z.shape[1] // 3 * LANES
    ff = w_gate.shape[2]
    ts = EXPERT_TS
    tf = min(tf, ff)
    nf = ff // tf
    s_tiles = cap // ts
    assert cap % ts == 0 and s_tiles >= 2 and ts % nf == 0
    nt = ne * s_tiles
    idx3 = idx.reshape(nt, 1, ts)
    smem_tile = lambda shift: pl.BlockSpec(
        (1, 1, ts), lambda t, f: (jnp.clip(t + shift, 0, nt - 1), 0, 0), memory_space=pltpu.SMEM)
    return pl.pallas_call(
        functools.partial(_expert_kernel, ts=ts, nf=nf, s_tiles=s_tiles, ne=ne),
        grid=(nt, nf),
        in_specs=[smem_tile(-1), smem_tile(0), smem_tile(1),
                  pl.BlockSpec((1, ts, LANES), lambda t, f: (t // s_tiles, t % s_tiles, 0)),
                  pl.BlockSpec(memory_space=pl.ANY),
                  pl.BlockSpec((1, d, tf), lambda t, f: (t // s_tiles, 0, f)),
                  pl.BlockSpec((1, d, tf), lambda t, f: (t // s_tiles, 0, f)),
                  pl.BlockSpec((1, tf, d), lambda t, f: (t // s_tiles, f, 0))],
        out_specs=pl.BlockSpec(memory_space=pl.ANY),
        out_shape=jax.ShapeDtypeStruct(z.shape, F32),
        scratch_shapes=[pltpu.VMEM((3,) + _row_tile_shape(ts, 2 * d), F32), pltpu.VMEM((ts, d), BF16),
                        pltpu.VMEM((ts, d), F32), pltpu.SemaphoreType.DMA((3,)), pltpu.SemaphoreType.DMA((3,))],
        input_output_aliases={4: 0},
        compiler_params=_params("arbitrary", "arbitrary"),
        name="expert_ffn",
    )(idx3, idx3, idx3, rec, z, w_gate, w_up, w_down)


def _final_norm_kernel(y0_ref, y1_ref, g_ref, o_ref):
    x = jnp.concatenate([_from_row_tiles(y0_ref[:, c] + y1_ref[:, c]) for c in range(y0_ref.shape[1])], axis=1)
    o_ref[...] = _rms(x, g_ref[...])


def final_norm(z, g, tm=512):
    n, d = z.shape[0] * SUBLANES, z.shape[1] // 3 * LANES
    tm = min(tm, n)
    assert n % tm == 0
    return pl.pallas_call(
        _final_norm_kernel,
        grid=(n // tm,),
        in_specs=[pl.BlockSpec(_row_tile_shape(tm, d), lambda i: (i, 0, 0, 0)),
                  pl.BlockSpec(_row_tile_shape(tm, d), lambda i: (i, 2, 0, 0)),
                  pl.BlockSpec((1, d), lambda i: (0, 0))],
        out_specs=pl.BlockSpec((tm, d), lambda i: (i, 0)),
        out_shape=jax.ShapeDtypeStruct((n, d), F32),
        compiler_params=_params("parallel"),
        name="final_norm",
    )(z, z, g)


def moe_and_norm(z, aff, w_e_gate, w_e_up, w_e_down, g_final):
    n = z.shape[0] * SUBLANES
    cap = EC_CAPACITY * n // N_EXPERTS
    sel, pos, off = select(aff, cap)
    choff = off[:, :, 0].astype(I32).reshape(-1)
    idx, rec = compact(sel, pos, choff, aff, cap)
    z = expert_ffn(idx, rec, z, w_e_gate, w_e_up, w_e_down)
    return final_norm(z, g_final)


def encoder_group(x, mem, w, experts, g_final):
    b, t, d = x.shape
    z, aff = layer_front(x, mem, w)
    return moe_and_norm(z, aff, *experts, g_final).reshape(b, t, d)


def kernel(x_prompt, x_sample, mem_prompt, mem_sample, g_mix, g_mem, w_in, w_mem_kv, na_rpb, t5_table, sw_sink,
           w_na_o, w_sw_o, w_mx_o, w_out, g_ffn, w_router, w_e_gate, w_e_up, w_e_down, g_final):
    assert g_mix.shape[0] == 1, "single-layer trunk"
    w = prep_weights(g_mix[0], g_mem[0], w_in[0], w_mem_kv[0], na_rpb[0], t5_table, sw_sink[0], w_na_o[0],
                     w_sw_o[0], w_mx_o[0], w_out[0], g_ffn[0], w_router[0])
    experts = (w_e_gate[0].astype(BF16), w_e_up[0].astype(BF16), w_e_down[0].astype(BF16))
    gf = g_final.reshape(1, -1).astype(F32)
    y_prompt = encoder_group(x_prompt, mem_prompt, w, experts, gf)
    y_sample = encoder_group(x_sample, mem_sample, w, experts, gf)
    return (y_prompt, y_sample)
```

```python
import functools
import math

import numpy as np
import jax
import jax.numpy as jnp
from jax import lax
from jax.experimental import pallas as pl
from jax.experimental.pallas import tpu as pltpu

F32 = jnp.float32
BF16 = jnp.bfloat16
I32 = jnp.int32

RMS_EPS = 1e-6
NEG_INF = -1e30

GRID_W = 64
NA_HEADS = 8
NA_HEAD_DIM = 64
NA_KR = 8
NA_KC = 16
SW_HEADS = 16
SW_KV_HEADS = 4
SW_HEAD_DIM = 64
SW_WINDOW = 128
SW_BLOCK = 128
MX_HEADS = 4
T5_BUCKETS = 32
T5_MAX_DIST = 128
N_BRANCHES = 3
N_EXPERTS = 16
EC_CAPACITY = 2

LANES = 128
SUBLANES = 8
V7X_VMEM_BYTES = 64 * 1024 * 1024
VMEM_LIMIT = V7X_VMEM_BYTES * 7 // 8

CHUNK_UNROLL = 8
EXPERT_TS = 512
EXPERT_TF = 512
MERGE_TN = 256

NT_DIMS = (((1,), (1,)), ((), ()))
TN_DIMS = (((0,), (0,)), ((), ()))


VMEM_LIMIT_MERGE = V7X_VMEM_BYTES * 15 // 16


def _params(*sem, vmem_limit=VMEM_LIMIT):
    return pltpu.CompilerParams(dimension_semantics=sem, vmem_limit_bytes=vmem_limit)


def _rms(x, g):
    return x * lax.rsqrt(jnp.mean(x * x, axis=-1, keepdims=True) + RMS_EPS) * g


def _to_row_tiles(strip):
    return strip.reshape(strip.shape[0] // SUBLANES, SUBLANES, LANES)


def _from_row_tiles(tiles):
    return tiles.reshape(tiles.shape[0] * SUBLANES, LANES)


def _row_tile_shape(rows, d):
    return (rows // SUBLANES, d // LANES, SUBLANES, LANES)


def _col_tiles(w, tn):
    *lead, r, c = w.shape
    return jnp.moveaxis(w.reshape(*lead, r, c // tn, tn), -2, 0)


def _norm_proj_kernel(x_ref, g_ref, w_ref, *o_refs, scales):
    h = _rms(x_ref[...], g_ref[...]).astype(BF16)
    off = 0
    for o_ref, sc in zip(o_refs, scales):
        width = o_ref.shape[1]
        for c0 in range(0, width, 512):
            cw = min(512, width - c0)
            r = jnp.dot(h, w_ref[:, off + c0:off + c0 + cw], preferred_element_type=F32)
            if sc != 1.0:
                r = r * sc
            o_ref[:, c0:c0 + cw] = r.astype(o_ref.dtype)
        off += width


def norm_proj(x, g, w, widths, scales, tm):
    n, d = x.shape
    tm = min(tm, n)
    assert n % tm == 0
    return pl.pallas_call(
        functools.partial(_norm_proj_kernel, scales=tuple(scales)),
        grid=(n // tm,),
        in_specs=[pl.BlockSpec((tm, d), lambda i: (i, 0)),
                  pl.BlockSpec((1, d), lambda i: (0, 0)),
                  pl.BlockSpec(w.shape, lambda i: (0, 0))],
        out_specs=[pl.BlockSpec((tm, c), lambda i: (i, 0)) for c in widths],
        out_shape=[jax.ShapeDtypeStruct((n, c), BF16) for c in widths],
        compiler_params=_params("parallel"),
        name="norm_proj",
    )(x, g, w)


def _softmax_pv(s, v, extra_logit=None):
    m = jnp.max(s, axis=-1, keepdims=True)
    if extra_logit is not None:
        m = jnp.maximum(m, extra_logit)
    e = jnp.exp(s - m)
    den = jnp.sum(e, axis=-1, keepdims=True)
    if extra_logit is not None:
        den = den + jnp.exp(extra_logit - m)
    o = jnp.dot(e.astype(BF16), v, preferred_element_type=F32)
    return o / den


def _softmax_pv_half(s, v, extra_logit, use_lo):
    rows, keys = s.shape
    mb = jnp.maximum(jnp.broadcast_to(jnp.max(s, axis=-1, keepdims=True), (rows, LANES)), extra_logit)
    e = jnp.concatenate([jnp.exp(s[:, t * LANES:(t + 1) * LANES] - mb) for t in range(keys // LANES)],
                        axis=1).astype(BF16)
    lo = lax.broadcasted_iota(I32, v.shape, 1) < LANES // 2
    keep = lo if use_lo else jnp.logical_not(lo)
    o = jnp.dot(e, jnp.where(keep, v, jnp.ones_like(v)), preferred_element_type=F32)
    den = pltpu.roll(o + jnp.exp(extra_logit - mb), LANES // 2, 1)
    return o / den


def _na_kernel(q_ref, k_ref, v_ref, bias_ref, o_ref, *, rows, rb):
    j = pl.program_id(1)
    lo = lax.broadcasted_iota(I32, (GRID_W, LANES), 1) < NA_HEAD_DIM
    nkeys = NA_KR * GRID_W

    def body(i, carry):
        r = j * rb + i
        rs = jnp.clip(r - NA_KR // 2, 0, rows - NA_KR)
        off = r - rs
        q = q_ref[0, pl.ds(pl.multiple_of(i * GRID_W, GRID_W), GRID_W), :]
        kk = k_ref[0, pl.ds(pl.multiple_of(rs * GRID_W, GRID_W), nkeys), :]
        vv = v_ref[0, pl.ds(pl.multiple_of(rs * GRID_W, GRID_W), nkeys), :]
        scores = []
        for p in range(NA_HEADS // 2):
            qp = q[:, p * LANES:(p + 1) * LANES]
            kp = kk[:, p * LANES:(p + 1) * LANES]
            zero = jnp.zeros_like(qp)
            q2 = jnp.concatenate([jnp.where(lo, qp, zero), jnp.where(lo, zero, qp)], axis=0)
            s = lax.dot_general(q2, kp, NT_DIMS, preferred_element_type=F32)
            scores.append(s + bias_ref[off, 2 * p:2 * p + 2].reshape(2 * GRID_W, nkeys))
        outs = []
        for p in range(NA_HEADS // 2):
            o2 = _softmax_pv(scores[p], vv[:, p * LANES:(p + 1) * LANES])
            outs.append(jnp.where(lo, o2[:GRID_W], o2[GRID_W:]))
        o_ref[0, pl.ds(pl.multiple_of(i * GRID_W, GRID_W), GRID_W), :] = (
            jnp.concatenate(outs, axis=1).astype(o_ref.dtype))
        return carry

    lax.fori_loop(0, rb, body, 0, unroll=2)


def na_attention(q, k, v, bias, rb=8):
    b, t, c = q.shape
    rows = t // GRID_W
    assert rows >= NA_KR and rows % rb == 0
    return pl.pallas_call(
        functools.partial(_na_kernel, rows=rows, rb=rb),
        grid=(b, rows // rb),
        in_specs=[pl.BlockSpec((1, rb * GRID_W, c), lambda i, j: (i, j, 0)),
                  pl.BlockSpec((1, t, c), lambda i, j: (i, 0, 0)),
                  pl.BlockSpec((1, t, c), lambda i, j: (i, 0, 0)),
                  pl.BlockSpec(bias.shape, lambda i, j: (0, 0, 0, 0))],
        out_specs=pl.BlockSpec((1, rb * GRID_W, c), lambda i, j: (i, j, 0)),
        out_shape=jax.ShapeDtypeStruct((b, t, c), BF16),
        compiler_params=_params("parallel", "arbitrary"),
        name="na_attention",
    )(q, k, v, bias)


def na_bias_table(rpb):
    col = np.arange(GRID_W)
    col_start = np.clip(col - NA_KC // 2, 0, GRID_W - NA_KC)
    in_win = (col[None, :] >= col_start[:, None]) & (col[None, :] < col_start[:, None] + NA_KC)
    dc = np.clip(col[None, :] - col[:, None] + NA_KC - 1, 0, 2 * NA_KC - 2)
    heads = rpb.shape[0]
    by_col = jnp.take(rpb.astype(F32), jnp.asarray(dc.reshape(-1)), axis=2)
    by_col = by_col.reshape(heads, 2 * NA_KR - 1, GRID_W, GRID_W)
    by_col = jnp.where(jnp.asarray(in_win)[None, None], by_col, NEG_INF)
    per_off = [by_col[:, NA_KR - 1 - off:2 * NA_KR - 1 - off].transpose(0, 2, 1, 3) for off in range(NA_KR)]
    return jnp.stack(per_off).reshape(NA_KR, heads, GRID_W, NA_KR * GRID_W)


def _sw_kernel(sink_ref, q_ref, k_ref, v_ref, bias_ref, o_ref, s_scr, *, nb):
    n = pl.program_id(1)
    blk = SW_BLOCK
    group = SW_HEADS // SW_KV_HEADS

    def rows_of(ref, c):
        return ref[0, pl.ds(pl.multiple_of(c * blk, blk), blk), :]

    cl = jnp.maximum(n - 1, 0)
    cr = jnp.minimum(n + 1, nb - 1)
    k3 = jnp.concatenate([rows_of(k_ref, cl), rows_of(k_ref, n), rows_of(k_ref, cr)], axis=0)
    v3 = jnp.concatenate([rows_of(v_ref, cl), rows_of(v_ref, n), rows_of(v_ref, cr)], axis=0)
    pen_l = jnp.where(n > 0, 0.0, NEG_INF).astype(F32)
    pen_r = jnp.where(n < nb - 1, 0.0, NEG_INF).astype(F32)
    key = lax.broadcasted_iota(I32, (1, 3 * blk), 1)
    pen = jnp.where(key < blk, pen_l, jnp.where(key >= 2 * blk, pen_r, 0.0))
    lo = lax.broadcasted_iota(I32, (blk, LANES), 1) < SW_HEAD_DIM

    for c in range(SW_KV_HEADS):
        pair, half = divmod(c, 2)
        kp = k3[:, pair * LANES:(pair + 1) * LANES]
        keep = lo if half == 0 else jnp.logical_not(lo)
        qs = []
        for g in range(group):
            t = pair * group + g
            qt = q_ref[0, :, t * LANES:(t + 1) * LANES]
            qs.append(jnp.where(keep, qt, jnp.zeros_like(qt)))
        qq = jnp.concatenate(qs, axis=0)
        s = lax.dot_general(qq, kp, NT_DIMS, preferred_element_type=F32)
        s_scr[c] = s + bias_ref[c * group:(c + 1) * group].reshape(group * blk, 3 * blk) + pen

    for pair in range(SW_KV_HEADS // 2):
        vp = v3[:, pair * LANES:(pair + 1) * LANES]
        per_half = []
        for half in range(2):
            c = 2 * pair + half
            sink = jnp.concatenate(
                [jnp.full((blk, LANES), sink_ref[c * group + g], F32) for g in range(group)], axis=0)
            per_half.append(_softmax_pv_half(s_scr[c], vp, sink, use_lo=(half == 0)))
        for g in range(group):
            t = pair * group + g
            o = jnp.where(lo, per_half[0][g * blk:(g + 1) * blk], per_half[1][g * blk:(g + 1) * blk])
            o_ref[0, :, t * LANES:(t + 1) * LANES] = o.astype(o_ref.dtype)


def sw_attention(q, k, v, bias, sink):
    b, t, c = q.shape
    nb = t // SW_BLOCK
    kvw = k.shape[2]
    grid_spec = pltpu.PrefetchScalarGridSpec(
        num_scalar_prefetch=1,
        grid=(b, nb),
        in_specs=[pl.BlockSpec((1, SW_BLOCK, c), lambda i, j, s: (i, j, 0)),
                  pl.BlockSpec((1, t, kvw), lambda i, j, s: (i, 0, 0)),
                  pl.BlockSpec((1, t, kvw), lambda i, j, s: (i, 0, 0)),
                  pl.BlockSpec(bias.shape, lambda i, j, s: (0, 0, 0))],
        out_specs=pl.BlockSpec((1, SW_BLOCK, c), lambda i, j, s: (i, j, 0)),
        scratch_shapes=[pltpu.VMEM((SW_KV_HEADS, (SW_HEADS // SW_KV_HEADS) * SW_BLOCK, 3 * SW_BLOCK), F32)],
    )
    return pl.pallas_call(
        functools.partial(_sw_kernel, nb=nb),
        grid_spec=grid_spec,
        out_shape=jax.ShapeDtypeStruct((b, t, c), BF16),
        compiler_params=_params("parallel", "arbitrary"),
        name="sw_attention",
    )(sink, q, k, v, bias)


def _t5_bucket(rel):
    half = T5_BUCKETS // 2
    max_exact = half // 2
    n = jnp.abs(rel)
    nf = jnp.maximum(n, 1).astype(F32)
    large = max_exact + (jnp.log(nf / max_exact) / math.log(T5_MAX_DIST / max_exact)
                         * (half - max_exact)).astype(jnp.int32)
    large = jnp.minimum(large, half - 1)
    return jnp.where(rel > 0, half, 0) + jnp.where(n < max_exact, n, large)


def sw_bias_table(t5_table):
    span = SW_BLOCK + 2 * SW_WINDOW
    rel = np.arange(span)[None, :] - SW_WINDOW - np.arange(SW_BLOCK)[:, None]
    bucket = _t5_bucket(jnp.asarray(rel, dtype=jnp.int32))[None]
    table = t5_table.astype(F32).T[:, :, None, None]
    bias = jnp.zeros((t5_table.shape[1], SW_BLOCK, span), F32)
    for b in range(T5_BUCKETS):
        bias = jnp.where(bucket == b, table[:, b], bias)
    return jnp.where(jnp.asarray(np.abs(rel) <= SW_WINDOW)[None], bias, NEG_INF)


def sw_pair_heads(w, axis):
    group = SW_HEADS // SW_KV_HEADS
    shape = w.shape
    split = shape[:axis] + (SW_KV_HEADS // 2, 2, group, SW_HEAD_DIM) + shape[axis + 1:]
    order = list(range(len(split)))
    order[axis + 1], order[axis + 2] = axis + 2, axis + 1
    return w.reshape(split).transpose(order).reshape(shape)


def _mx_kernel(q_ref, mk_ref, mv_ref, o_ref, *, scale):
    hd = q_ref.shape[2] // MX_HEADS
    heads = [slice(h * hd, (h + 1) * hd) for h in range(MX_HEADS)]
    scores = [lax.dot_general(q_ref[0, :, sl], mk_ref[0, :, sl], NT_DIMS, preferred_element_type=F32) * scale
              for sl in heads]
    for sl, s in zip(heads, scores):
        o_ref[0, :, sl] = _softmax_pv(s, mv_ref[0, :, sl]).astype(o_ref.dtype)


def mx_attention(q, mkv, tq=512):
    b, t, c = q.shape
    m = mkv.shape[1]
    tq = min(tq, t)
    return pl.pallas_call(
        functools.partial(_mx_kernel, scale=float((c // MX_HEADS) ** -0.5)),
        grid=(b, t // tq),
        in_specs=[pl.BlockSpec((1, tq, c), lambda i, j: (i, j, 0)),
                  pl.BlockSpec((1, m, c), lambda i, j: (i, 0, 0)),
                  pl.BlockSpec((1, m, c), lambda i, j: (i, 0, 1))],
        out_specs=pl.BlockSpec((1, tq, c), lambda i, j: (i, j, 0)),
        out_shape=jax.ShapeDtypeStruct((b, t, c), BF16),
        compiler_params=_params("parallel", "arbitrary"),
        name="mx_attention",
    )(q, mkv, mkv)


def _merge_kernel(x_ref, ona_ref, osw_ref, omx_ref, gmix_ref, wg_ref, wna_ref, wsw_ref, wmx_ref, wout_ref,
                  gffn_ref, wrh_ref, wrl_ref, z_ref, aff_ref, h_scr, m_scr):
    j = pl.program_id(1)
    nstrip = z_ref.shape[1] // 3
    strips = range(nstrip)

    @pl.when(j == 0)
    def _():
        x = x_ref[...]
        h_scr[...] = _rms(x, gmix_ref[...]).astype(BF16)
        for c in strips:
            z_ref[:, c] = _to_row_tiles(x[:, c * LANES:(c + 1) * LANES])

    h = h_scr[...]
    merged = None
    for o_ref, w_ref, b in ((ona_ref, wna_ref, 0), (osw_ref, wsw_ref, 1), (omx_ref, wmx_ref, 2)):
        gate = jax.nn.sigmoid(jnp.dot(h, wg_ref[0, b], preferred_element_type=F32))
        term = gate * jnp.dot(o_ref[...], w_ref[0], preferred_element_type=F32)
        merged = term if merged is None else merged + term
    m_scr[j] = merged.astype(BF16)

    @pl.when(j == pl.num_programs(1) - 1)
    def _():
        merged_all = jnp.concatenate([m_scr[jj] for jj in range(m_scr.shape[0])], axis=1)
        proj = jnp.dot(merged_all, wout_ref[...], preferred_element_type=F32)
        for c in strips:
            z_ref[:, c] += _to_row_tiles(proj[:, c * LANES:(c + 1) * LANES])
        x2 = jnp.concatenate([_from_row_tiles(z_ref[:, c]) for c in strips], axis=1)
        h2 = _rms(x2, gffn_ref[...])
        for c in strips:
            z_ref[:, nstrip + c] = _to_row_tiles(h2[:, c * LANES:(c + 1) * LANES])
            z_ref[:, 2 * nstrip + c] = jnp.zeros((z_ref.shape[0], SUBLANES, LANES), F32)
        hi = h2.astype(BF16)
        lo = (h2 - hi.astype(F32)).astype(BF16)
        wh = wrh_ref[...]
        logits = (lax.dot_general(wh, hi, NT_DIMS, preferred_element_type=F32)
                  + lax.dot_general(wh, lo, NT_DIMS, preferred_element_type=F32)
                  + lax.dot_general(wrl_ref[...], hi, NT_DIMS, preferred_element_type=F32))
        m = jnp.max(logits, axis=0, keepdims=True)
        e = jnp.exp(logits - m)
        aff = e / jnp.sum(e, axis=0, keepdims=True)
        for c in range(aff_ref.shape[0]):
            aff_ref[c] = aff[:, c * LANES:(c + 1) * LANES]


def merge(x, o_na, o_sw, o_mx, g_mix, w_gate, w_na_o, w_sw_o, w_mx_o, w_out, g_ffn, wr_hi, wr_lo,
          tm=512):
    n, d = x.shape
    tn = w_gate.shape[3]
    tm = min(tm, n)
    assert n % tm == 0 and tm % LANES == 0 and d % LANES == 0
    ne = wr_hi.shape[0]
    row = lambda i, j: (i, 0)
    return pl.pallas_call(
        _merge_kernel,
        grid=(n // tm, d // tn),
        in_specs=[pl.BlockSpec((tm, d), row, pipeline_mode=pl.Buffered(1)),
                  pl.BlockSpec((tm, o_na.shape[1]), row),
                  pl.BlockSpec((tm, o_sw.shape[1]), row),
                  pl.BlockSpec((tm, o_mx.shape[1]), row),
                  pl.BlockSpec((1, d), lambda i, j: (0, 0)),
                  pl.BlockSpec((1,) + w_gate.shape[1:], lambda i, j: (j, 0, 0, 0)),
                  pl.BlockSpec((1,) + w_na_o.shape[1:], lambda i, j: (j, 0, 0)),
                  pl.BlockSpec((1,) + w_sw_o.shape[1:], lambda i, j: (j, 0, 0)),
                  pl.BlockSpec((1,) + w_mx_o.shape[1:], lambda i, j: (j, 0, 0)),
                  pl.BlockSpec((d, d), lambda i, j: (0, 0), pipeline_mode=pl.Buffered(1)),
                  pl.BlockSpec((1, d), lambda i, j: (0, 0)),
                  pl.BlockSpec((ne, d), lambda i, j: (0, 0)),
                  pl.BlockSpec((ne, d), lambda i, j: (0, 0))],
        out_specs=[pl.BlockSpec(_row_tile_shape(tm, 3 * d), lambda i, j: (i, 0, 0, 0)),
                   pl.BlockSpec((tm // LANES, ne, LANES), lambda i, j: (i, 0, 0))],
        out_shape=[jax.ShapeDtypeStruct(_row_tile_shape(n, 3 * d), F32),
                   jax.ShapeDtypeStruct((n // LANES, ne, LANES), F32)],
        scratch_shapes=[pltpu.VMEM((tm, d), BF16), pltpu.VMEM((d // tn, tm, tn), BF16)],
        compiler_params=_params("parallel", "arbitrary", vmem_limit=VMEM_LIMIT_MERGE),
        name="merge",
    )(x, o_na, o_sw, o_mx, g_mix, w_gate, w_na_o, w_sw_o, w_mx_o, w_out, g_ffn, wr_hi, wr_lo)


NA_WIDTH = NA_HEADS * NA_HEAD_DIM
SW_WIDTH = SW_HEADS * SW_HEAD_DIM
SW_KV_WIDTH = SW_KV_HEADS * SW_HEAD_DIM
QKV_WIDTHS = (NA_WIDTH, NA_WIDTH, NA_WIDTH, SW_WIDTH, SW_KV_WIDTH, SW_KV_WIDTH)


def prep_weights(g_mix, g_mem, w_in, w_mem_kv, na_rpb, t5_table, sw_sink, w_na_o, w_sw_o, w_mx_o, w_out,
                 g_ffn, w_router):
    d = w_in.shape[0]
    mx_width = w_mx_o.shape[0]
    widths = QKV_WIDTHS + (mx_width,)
    offs = np.concatenate([[0], np.cumsum(widths)])
    cols = [w_in[:, offs[i]:offs[i + 1]] for i in range(len(widths))]
    cols[3] = sw_pair_heads(cols[3], axis=1)
    w_gate = w_in[:, offs[-1]:].reshape(d, N_BRANCHES, d).transpose(1, 0, 2)
    wr = w_router.T.astype(F32)
    wr_hi = wr.astype(BF16)
    return dict(
        g_mix=g_mix.reshape(1, d).astype(F32), g_mem=g_mem.reshape(1, d).astype(F32),
        g_ffn=g_ffn.reshape(1, d).astype(F32),
        w_qkv=jnp.concatenate(cols, axis=1).astype(BF16), qkv_widths=widths,
        qkv_scales=(NA_HEAD_DIM ** -0.5, 1.0, 1.0, SW_HEAD_DIM ** -0.5, 1.0, 1.0, 1.0),
        w_mem_kv=w_mem_kv.astype(BF16),
        na_bias=na_bias_table(na_rpb), sw_bias=sw_bias_table(t5_table), sw_sink=sw_sink.astype(F32),
        w_gate=_col_tiles(w_gate.astype(BF16), MERGE_TN), w_na_o=_col_tiles(w_na_o.astype(BF16), MERGE_TN),
        w_sw_o=_col_tiles(sw_pair_heads(w_sw_o, axis=0).astype(BF16), MERGE_TN),
        w_mx_o=_col_tiles(w_mx_o.astype(BF16), MERGE_TN), w_out=w_out.astype(BF16),
        wr_hi=wr_hi, wr_lo=(wr - wr_hi.astype(F32)).astype(BF16))


def layer_front(x, mem, w, debug=False):
    b, t, d = x.shape
    m = mem.shape[1]
    xf = x.reshape(b * t, d)
    na_q, na_k, na_v, sw_q, sw_k, sw_v, mx_q = norm_proj(xf, w["g_mix"], w["w_qkv"], w["qkv_widths"],
                                                         w["qkv_scales"], tm=512)
    (mkv,) = norm_proj(mem.reshape(b * m, d), w["g_mem"], w["w_mem_kv"], (w["w_mem_kv"].shape[1],), (1.0,),
                       tm=512)
    r3 = lambda a: a.reshape(b, t, a.shape[1])
    o_na = na_attention(r3(na_q), r3(na_k), r3(na_v), w["na_bias"])
    o_sw = sw_attention(r3(sw_q), r3(sw_k), r3(sw_v), w["sw_bias"], w["sw_sink"])
    o_mx = mx_attention(r3(mx_q), mkv.reshape(b, m, mkv.shape[1]))
    f2 = lambda a: a.reshape(b * t, a.shape[2])
    z, aff = merge(xf, f2(o_na), f2(o_sw), f2(o_mx), w["g_mix"], w["w_gate"], w["w_na_o"], w["w_sw_o"],
                   w["w_mx_o"], w["w_out"], w["g_ffn"], w["wr_hi"], w["wr_lo"])
    if debug:
        return o_na, o_sw, o_mx, z, aff
    return z, aff


def _select_kernel(aff_ref, sel_ref, pos_ref, off_ref, *, cap):
    nc, ne, _ = aff_ref.shape
    bits = lax.bitcast_convert_type(aff_ref[...], I32)
    tok = (lax.broadcasted_iota(I32, bits.shape, 0) * LANES + lax.broadcasted_iota(I32, bits.shape, 2))

    def count(flags):
        return jnp.sum(flags, axis=(0, 2), keepdims=True)

    def value_step(i, prefix):
        cand = prefix | lax.shift_left(jnp.int32(1), 30 - i)
        cnt = count(jnp.where(bits >= cand, 1.0, 0.0))
        return jnp.where(cnt >= cap, cand, prefix)

    tau = lax.fori_loop(0, 31, value_step, jnp.zeros((1, ne, 1), I32))
    gt = bits > tau
    eq = bits == tau
    need = cap - count(jnp.where(gt, 1.0, 0.0))

    def index_step(i, last):
        cand = last | lax.shift_left(jnp.int32(1), 15 - i)
        cnt = count(jnp.where(eq, jnp.where(tok < cand, 1.0, 0.0), 0.0))
        return jnp.where(cnt < need, cand, last)

    last = lax.fori_loop(0, 16, index_step, jnp.zeros((1, ne, 1), I32))
    sel_ref[...] = jnp.where(gt, 1.0, jnp.where(eq, jnp.where(tok <= last, 1.0, 0.0), 0.0))

    upper = (lax.broadcasted_iota(I32, (LANES, LANES), 0) <= lax.broadcasted_iota(I32, (LANES, LANES), 1))
    upper = jnp.where(upper, 1.0, 0.0).astype(BF16)

    def chunk_step(c, off):
        s = sel_ref[c]
        cum = jnp.dot(s.astype(BF16), upper, preferred_element_type=F32)
        pos_ref[c] = off + cum - s
        off_ref[c] = jnp.broadcast_to(off, s.shape)
        return off + cum[:, LANES - 1:LANES]

    lax.fori_loop(0, nc, chunk_step, jnp.zeros((ne, 1), F32), unroll=CHUNK_UNROLL)


def select(aff, cap):
    nc, ne, _ = aff.shape
    assert nc * LANES <= 65536
    shape = jax.ShapeDtypeStruct(aff.shape, F32)
    return pl.pallas_call(
        functools.partial(_select_kernel, cap=float(cap)),
        out_shape=[shape, shape, shape],
        compiler_params=pltpu.CompilerParams(vmem_limit_bytes=VMEM_LIMIT),
        name="select",
    )(aff)


TOK_COL, CHUNK_COL, HI_COL = 0, 1, 2


def _compact_kernel(choff_ref, sel_ref, pos_ref, aff_ref, o_ref, tv_scr):
    e = pl.program_id(0)
    nc, ne, _ = aff_ref.shape
    win = 2 * LANES
    col = lax.broadcasted_iota(I32, (LANES, LANES), 1)
    row = lax.broadcasted_iota(I32, (LANES, LANES), 0)

    @pl.when(e == 0)
    def _():
        ecol = lax.broadcasted_iota(I32, (ne, LANES), 1) - lax.broadcasted_iota(I32, (ne, LANES), 0)
        place = [jnp.where(ecol == HI_COL + k * ne, 1.0, 0.0).astype(BF16) for k in range(3)]

        def build(c, carry):
            a = aff_ref[c]
            hi = a.astype(BF16)
            r1 = a - hi.astype(F32)
            mid = r1.astype(BF16)
            lo = (r1 - mid.astype(F32)).astype(BF16)
            rec = (lax.dot_general(hi, place[0], TN_DIMS, preferred_element_type=F32)
                   + lax.dot_general(mid, place[1], TN_DIMS, preferred_element_type=F32)
                   + lax.dot_general(lo, place[2], TN_DIMS, preferred_element_type=F32))
            rec = rec + jnp.where(col == TOK_COL, row, jnp.where(col == CHUNK_COL, c, 0)).astype(F32)
            tv_scr[c] = rec.astype(BF16)
            return carry

        lax.fori_loop(0, nc, build, 0, unroll=CHUNK_UNROLL)

    o_ref[...] = jnp.zeros_like(o_ref)
    slot0 = lax.broadcasted_iota(I32, (win, LANES), 0).astype(F32)

    def body(c, carry):
        off = choff_ref[c * ne + e]
        base = pl.multiple_of((off // LANES) * LANES, LANES)
        s = sel_ref[c, pl.ds(e, 1), :]
        p = pos_ref[c, pl.ds(e, 1), :] - base.astype(F32)
        onehot = jnp.where(s > 0.0, jnp.where(slot0 == p, 1.0, 0.0), 0.0).astype(BF16)
        o_ref[0, pl.ds(base, win), :] += jnp.dot(onehot, tv_scr[c], preferred_element_type=F32)
        return carry

    lax.fori_loop(0, nc, body, 0, unroll=CHUNK_UNROLL)


def compact(sel, pos, choff, aff, cap):
    nc, ne, _ = aff.shape
    assert nc <= 256 and HI_COL + 3 * ne <= LANES and cap % LANES == 0
    rows = cap + 2 * LANES
    full = pl.BlockSpec(aff.shape, lambda e, s: (0, 0, 0))
    rec = pl.pallas_call(
        _compact_kernel,
        grid_spec=pltpu.PrefetchScalarGridSpec(
            num_scalar_prefetch=1, grid=(ne,),
            in_specs=[full, full, full],
            out_specs=pl.BlockSpec((1, rows, LANES), lambda e, s: (e, 0, 0)),
            scratch_shapes=[pltpu.VMEM((nc, LANES, LANES), BF16)]),
        out_shape=jax.ShapeDtypeStruct((ne, rows, LANES), F32),
        compiler_params=_params("arbitrary"),
        name="compact",
    )(choff, sel, pos, aff)
    idx = (rec[:, :cap, CHUNK_COL] * LANES + rec[:, :cap, TOK_COL]).astype(I32)
    return idx, rec


def _expert_kernel(idx_prev, idx_cur, idx_next, rec_ref, zin_ref, wg_ref, wu_ref, wd_ref, z_ref,
                   buf, xs, acc, gsem, ssem, *, ts, nf, s_tiles, ne):
    del zin_ref
    sub = z_ref.shape[1] // 3
    t = pl.program_id(0)
    f = pl.program_id(1)
    last_t = pl.num_programs(0) - 1
    rps = ts // nf
    slot, nslot, pslot = t % 3, (t + 1) % 3, (t + 2) % 3
    odd = (t // s_tiles) % 2
    nodd = (jnp.minimum(t + 1, last_t) // s_tiles) % 2
    podd = (jnp.maximum(t - 1, 0) // s_tiles) % 2

    def rows(ref, row, first, count):
        return ref.at[row >> 3, pl.ds(first, count), pl.ds(row & (SUBLANES - 1), 1), :]

    def gather(token, ra, rs, to_slot, is_odd):
        return pltpu.make_async_copy(rows(z_ref, token, is_odd * sub, 2 * sub),
                                     buf.at[to_slot, ra, :, pl.ds(rs, 1), :], gsem.at[to_slot])

    def scatter(token, ra, rs, from_slot, is_odd):
        return pltpu.make_async_copy(buf.at[from_slot, ra, pl.ds(is_odd * sub, sub), pl.ds(rs, 1), :],
                                     rows(z_ref, token, is_odd * 2 * sub, sub), ssem.at[from_slot])

    def wait_gather(s):
        pltpu.make_async_copy(z_ref.at[pl.ds(0, ts // SUBLANES), pl.ds(0, 2 * sub)], buf.at[s], gsem.at[s]).wait()

    def wait_scatter(s):
        pltpu.make_async_copy(buf.at[s, :, pl.ds(0, sub)], z_ref.at[pl.ds(0, ts // SUBLANES), pl.ds(0, sub)],
                              ssem.at[s]).wait()

    def for_rows(fn):
        def body(r, carry):
            fn(idx_cur[0, 0, r], r >> 3, r & (SUBLANES - 1))
            return carry
        lax.fori_loop(0, ts, body, 0)

    @pl.when(f == 0)
    def _():
        @pl.when(t == 0)
        def _():
            def start(token, ra, rs):
                gather(token, ra, rs, 0, 0).start()
                gather(token, ra, rs, 2, 0).start()
            for_rows(start)
            wait_gather(2)

        acc[...] = jnp.zeros(acc.shape, F32)
        wait_gather(slot)
        h2_first = (1 - odd) * sub
        for c in range(sub):
            xs[:, c * LANES:(c + 1) * LANES] = _from_row_tiles(buf[slot, :, h2_first + c]).astype(BF16)

    for k in range(rps):
        r = f * rps + k
        ra, rs = f * (rps // SUBLANES) + k // SUBLANES, k % SUBLANES
        gather(idx_next[0, 0, r], ra, rs, nslot, nodd).start()
        scatter(idx_prev[0, 0, r], ra, rs, pslot, podd).start()

    x = xs[...]
    a = jnp.dot(x, wg_ref[0, 0], preferred_element_type=F32)
    b = jnp.dot(x, wu_ref[0, 0], preferred_element_type=F32)
    hm = (jax.nn.silu(a) * b).astype(BF16)
    acc[...] += jnp.dot(hm, wd_ref[0], preferred_element_type=F32)

    @pl.when(f == nf - 1)
    def _():
        lane = lax.broadcasted_iota(I32, (1, LANES), 1) - (HI_COL + t // s_tiles)
        own = (lane == 0) | (lane == ne) | (lane == 2 * ne)
        contrib = acc[...] * jnp.sum(jnp.where(own, rec_ref[0], 0.0), axis=1, keepdims=True)
        y_first = odd * sub
        for c in range(sub):
            buf[slot, :, y_first + c] += _to_row_tiles(contrib[:, c * LANES:(c + 1) * LANES])
        wait_scatter(pslot)

        @pl.when(t == last_t)
        def _():
            wait_gather(nslot)
            for_rows(lambda token, ra, rs: scatter(token, ra, rs, slot, odd).start())
            wait_scatter(slot)


def expert_ffn(idx, rec, z, w_gate, w_up, w_down):
    ne, cap = idx.shape
    d = z.shape[1] // 3 * LANES
    nf, _, _, tf = w_gate.shape
    ts = EXPERT_TS
    s_tiles = cap // ts
    assert cap % ts == 0 and s_tiles >= 2 and ts % nf == 0
    nt = ne * s_tiles
    idx3 = idx.reshape(nt, 1, ts)
    smem_tile = lambda shift: pl.BlockSpec(
        (1, 1, ts), lambda t, f: (jnp.clip(t + shift, 0, nt - 1), 0, 0), memory_space=pltpu.SMEM)
    return pl.pallas_call(
        functools.partial(_expert_kernel, ts=ts, nf=nf, s_tiles=s_tiles, ne=ne),
        grid=(nt, nf),
        in_specs=[smem_tile(-1), smem_tile(0), smem_tile(1),
                  pl.BlockSpec((1, ts, LANES), lambda t, f: (t // s_tiles, t % s_tiles, 0)),
                  pl.BlockSpec(memory_space=pl.ANY),
                  pl.BlockSpec((1, 1, d, tf), lambda t, f: (f, t // s_tiles, 0, 0)),
                  pl.BlockSpec((1, 1, d, tf), lambda t, f: (f, t // s_tiles, 0, 0)),
                  pl.BlockSpec((1, tf, d), lambda t, f: (t // s_tiles, f, 0))],
        out_specs=pl.BlockSpec(memory_space=pl.ANY),
        out_shape=jax.ShapeDtypeStruct(z.shape, F32),
        scratch_shapes=[pltpu.VMEM((3,) + _row_tile_shape(ts, 2 * d), F32), pltpu.VMEM((ts, d), BF16),
                        pltpu.VMEM((ts, d), F32), pltpu.SemaphoreType.DMA((3,)), pltpu.SemaphoreType.DMA((3,))],
        input_output_aliases={4: 0},
        compiler_params=_params("arbitrary", "arbitrary"),
        name="expert_ffn",
    )(idx3, idx3, idx3, rec, z, w_gate, w_up, w_down)


def _final_norm_kernel(y0_ref, y1_ref, g_ref, o_ref):
    x = jnp.concatenate([_from_row_tiles(y0_ref[:, c] + y1_ref[:, c]) for c in range(y0_ref.shape[1])], axis=1)
    o_ref[...] = _rms(x, g_ref[...])


def final_norm(z, g, tm=512):
    n, d = z.shape[0] * SUBLANES, z.shape[1] // 3 * LANES
    tm = min(tm, n)
    assert n % tm == 0
    return pl.pallas_call(
        _final_norm_kernel,
        grid=(n // tm,),
        in_specs=[pl.BlockSpec(_row_tile_shape(tm, d), lambda i: (i, 0, 0, 0)),
                  pl.BlockSpec(_row_tile_shape(tm, d), lambda i: (i, 2, 0, 0)),
                  pl.BlockSpec((1, d), lambda i: (0, 0))],
        out_specs=pl.BlockSpec((tm, d), lambda i: (i, 0)),
        out_shape=jax.ShapeDtypeStruct((n, d), F32),
        compiler_params=_params("parallel"),
        name="final_norm",
    )(z, z, g)


def moe_and_norm(z, aff, w_e_gate, w_e_up, w_e_down, g_final):
    n = z.shape[0] * SUBLANES
    cap = EC_CAPACITY * n // N_EXPERTS
    sel, pos, off = select(aff, cap)
    choff = off[:, :, 0].astype(I32).reshape(-1)
    idx, rec = compact(sel, pos, choff, aff, cap)
    z = expert_ffn(idx, rec, z, w_e_gate, w_e_up, w_e_down)
    return final_norm(z, g_final)


def encoder_group(x, mem, w, experts, g_final):
    b, t, d = x.shape
    z, aff = layer_front(x, mem, w)
    return moe_and_norm(z, aff, *experts, g_final).reshape(b, t, d)


def kernel(x_prompt, x_sample, mem_prompt, mem_sample, g_mix, g_mem, w_in, w_mem_kv, na_rpb, t5_table, sw_sink,
           w_na_o, w_sw_o, w_mx_o, w_out, g_ffn, w_router, w_e_gate, w_e_up, w_e_down, g_final):
    assert g_mix.shape[0] == 1, "single-layer trunk"
    w = prep_weights(g_mix[0], g_mem[0], w_in[0], w_mem_kv[0], na_rpb[0], t5_table, sw_sink[0], w_na_o[0],
                     w_sw_o[0], w_mx_o[0], w_out[0], g_ffn[0], w_router[0])
    tf = min(EXPERT_TF, w_e_gate.shape[3])
    experts = (_col_tiles(w_e_gate[0].astype(BF16), tf), _col_tiles(w_e_up[0].astype(BF16), tf),
               w_e_down[0].astype(BF16))
    gf = g_final.reshape(1, -1).astype(F32)
    y_prompt = encoder_group(x_prompt, mem_prompt, w, experts, gf)
    y_sample = encoder_group(x_sample, mem_sample, w, experts, gf)
    return (y_prompt, y_sample)
```

```python
import functools
import math

import numpy as np
import jax
import jax.numpy as jnp
from jax import lax
from jax.experimental import pallas as pl
from jax.experimental.pallas import tpu as pltpu

F32 = jnp.float32
BF16 = jnp.bfloat16
I32 = jnp.int32

RMS_EPS = 1e-6
NEG_INF = -1e30

GRID_W = 64
NA_HEADS = 8
NA_HEAD_DIM = 64
NA_KR = 8
NA_KC = 16
SW_HEADS = 16
SW_KV_HEADS = 4
SW_HEAD_DIM = 64
SW_WINDOW = 128
SW_BLOCK = 128
MX_HEADS = 4
T5_BUCKETS = 32
T5_MAX_DIST = 128
N_BRANCHES = 3
N_EXPERTS = 16
EC_CAPACITY = 2

LANES = 128
SUBLANES = 8
V7X_VMEM_BYTES = 64 * 1024 * 1024
VMEM_LIMIT = V7X_VMEM_BYTES * 7 // 8

CHUNK_UNROLL = 8
EXPERT_TS = 512
MERGE_TN = 256

NT_DIMS = (((1,), (1,)), ((), ()))
TN_DIMS = (((0,), (0,)), ((), ()))


def _params(*sem):
    return pltpu.CompilerParams(dimension_semantics=sem, vmem_limit_bytes=VMEM_LIMIT)


def _rms(x, g):
    return x * lax.rsqrt(jnp.mean(x * x, axis=-1, keepdims=True) + RMS_EPS) * g


def _to_row_tiles(strip):
    return strip.reshape(strip.shape[0] // SUBLANES, SUBLANES, LANES)


def _from_row_tiles(tiles):
    return tiles.reshape(tiles.shape[0] * SUBLANES, LANES)


def _row_tile_shape(rows, d):
    return (rows // SUBLANES, d // LANES, SUBLANES, LANES)


def _norm_proj_kernel(x_ref, g_ref, w_ref, *o_refs, scales):
    h = _rms(x_ref[...], g_ref[...]).astype(BF16)
    off = 0
    for o_ref, sc in zip(o_refs, scales):
        width = o_ref.shape[1]
        for c0 in range(0, width, 512):
            cw = min(512, width - c0)
            r = jnp.dot(h, w_ref[:, off + c0:off + c0 + cw], preferred_element_type=F32)
            if sc != 1.0:
                r = r * sc
            o_ref[:, c0:c0 + cw] = r.astype(o_ref.dtype)
        off += width


def norm_proj(x, g, w, widths, scales, tm):
    n, d = x.shape
    tm = min(tm, n)
    assert n % tm == 0
    return pl.pallas_call(
        functools.partial(_norm_proj_kernel, scales=tuple(scales)),
        grid=(n // tm,),
        in_specs=[pl.BlockSpec((tm, d), lambda i: (i, 0)),
                  pl.BlockSpec((1, d), lambda i: (0, 0)),
                  pl.BlockSpec(w.shape, lambda i: (0, 0))],
        out_specs=[pl.BlockSpec((tm, c), lambda i: (i, 0)) for c in widths],
        out_shape=[jax.ShapeDtypeStruct((n, c), BF16) for c in widths],
        compiler_params=_params("parallel"),
        name="norm_proj",
    )(x, g, w)


def _softmax_pv(s, v, extra_logit=None):
    m = jnp.max(s, axis=-1, keepdims=True)
    if extra_logit is not None:
        m = jnp.maximum(m, extra_logit)
    e = jnp.exp(s - m)
    den = jnp.sum(e, axis=-1, keepdims=True)
    if extra_logit is not None:
        den = den + jnp.exp(extra_logit - m)
    o = jnp.dot(e.astype(BF16), v, preferred_element_type=F32)
    return o / den


def _softmax_pv_half(s, v, extra_logit, use_lo):
    rows, keys = s.shape
    mb = jnp.maximum(jnp.broadcast_to(jnp.max(s, axis=-1, keepdims=True), (rows, LANES)), extra_logit)
    e = jnp.concatenate([jnp.exp(s[:, t * LANES:(t + 1) * LANES] - mb) for t in range(keys // LANES)],
                        axis=1).astype(BF16)
    lo = lax.broadcasted_iota(I32, v.shape, 1) < LANES // 2
    keep = lo if use_lo else jnp.logical_not(lo)
    o = jnp.dot(e, jnp.where(keep, v, jnp.ones_like(v)), preferred_element_type=F32)
    den = pltpu.roll(o + jnp.exp(extra_logit - mb), LANES // 2, 1)
    return o / den


def _na_kernel(q_ref, k_ref, v_ref, bias_ref, o_ref, *, rows, rb):
    j = pl.program_id(1)
    lo = lax.broadcasted_iota(I32, (GRID_W, LANES), 1) < NA_HEAD_DIM
    nkeys = NA_KR * GRID_W

    def body(i, carry):
        r = j * rb + i
        rs = jnp.clip(r - NA_KR // 2, 0, rows - NA_KR)
        off = r - rs
        q = q_ref[0, pl.ds(pl.multiple_of(i * GRID_W, GRID_W), GRID_W), :]
        kk = k_ref[0, pl.ds(pl.multiple_of(rs * GRID_W, GRID_W), nkeys), :]
        vv = v_ref[0, pl.ds(pl.multiple_of(rs * GRID_W, GRID_W), nkeys), :]
        scores = []
        for p in range(NA_HEADS // 2):
            qp = q[:, p * LANES:(p + 1) * LANES]
            kp = kk[:, p * LANES:(p + 1) * LANES]
            zero = jnp.zeros_like(qp)
            q2 = jnp.concatenate([jnp.where(lo, qp, zero), jnp.where(lo, zero, qp)], axis=0)
            s = lax.dot_general(q2, kp, NT_DIMS, preferred_element_type=F32)
            scores.append(s + bias_ref[off, 2 * p:2 * p + 2].reshape(2 * GRID_W, nkeys))
        outs = []
        for p in range(NA_HEADS // 2):
            o2 = _softmax_pv(scores[p], vv[:, p * LANES:(p + 1) * LANES])
            outs.append(jnp.where(lo, o2[:GRID_W], o2[GRID_W:]))
        o_ref[0, pl.ds(pl.multiple_of(i * GRID_W, GRID_W), GRID_W), :] = (
            jnp.concatenate(outs, axis=1).astype(o_ref.dtype))
        return carry

    lax.fori_loop(0, rb, body, 0, unroll=2)


def na_attention(q, k, v, bias, rb=8):
    b, t, c = q.shape
    rows = t // GRID_W
    assert rows >= NA_KR and rows % rb == 0
    return pl.pallas_call(
        functools.partial(_na_kernel, rows=rows, rb=rb),
        grid=(b, rows // rb),
        in_specs=[pl.BlockSpec((1, rb * GRID_W, c), lambda i, j: (i, j, 0)),
                  pl.BlockSpec((1, t, c), lambda i, j: (i, 0, 0)),
                  pl.BlockSpec((1, t, c), lambda i, j: (i, 0, 0)),
                  pl.BlockSpec(bias.shape, lambda i, j: (0, 0, 0, 0))],
        out_specs=pl.BlockSpec((1, rb * GRID_W, c), lambda i, j: (i, j, 0)),
        out_shape=jax.ShapeDtypeStruct((b, t, c), BF16),
        compiler_params=_params("parallel", "arbitrary"),
        name="na_attention",
    )(q, k, v, bias)


def na_bias_table(rpb):
    col = np.arange(GRID_W)
    col_start = np.clip(col - NA_KC // 2, 0, GRID_W - NA_KC)
    in_win = (col[None, :] >= col_start[:, None]) & (col[None, :] < col_start[:, None] + NA_KC)
    dc = np.clip(col[None, :] - col[:, None] + NA_KC - 1, 0, 2 * NA_KC - 2)
    heads = rpb.shape[0]
    by_col = jnp.take(rpb.astype(F32), jnp.asarray(dc.reshape(-1)), axis=2)
    by_col = by_col.reshape(heads, 2 * NA_KR - 1, GRID_W, GRID_W)
    by_col = jnp.where(jnp.asarray(in_win)[None, None], by_col, NEG_INF)
    per_off = [by_col[:, NA_KR - 1 - off:2 * NA_KR - 1 - off].transpose(0, 2, 1, 3) for off in range(NA_KR)]
    return jnp.stack(per_off).reshape(NA_KR, heads, GRID_W, NA_KR * GRID_W)


def _sw_kernel(sink_ref, q_ref, k_ref, v_ref, bias_ref, o_ref, s_scr, *, nb):
    n = pl.program_id(1)
    blk = SW_BLOCK
    group = SW_HEADS // SW_KV_HEADS

    def rows_of(ref, c):
        return ref[0, pl.ds(pl.multiple_of(c * blk, blk), blk), :]

    cl = jnp.maximum(n - 1, 0)
    cr = jnp.minimum(n + 1, nb - 1)
    k3 = jnp.concatenate([rows_of(k_ref, cl), rows_of(k_ref, n), rows_of(k_ref, cr)], axis=0)
    v3 = jnp.concatenate([rows_of(v_ref, cl), rows_of(v_ref, n), rows_of(v_ref, cr)], axis=0)
    pen_l = jnp.where(n > 0, 0.0, NEG_INF).astype(F32)
    pen_r = jnp.where(n < nb - 1, 0.0, NEG_INF).astype(F32)
    key = lax.broadcasted_iota(I32, (1, 3 * blk), 1)
    pen = jnp.where(key < blk, pen_l, jnp.where(key >= 2 * blk, pen_r, 0.0))
    lo = lax.broadcasted_iota(I32, (blk, LANES), 1) < SW_HEAD_DIM

    for c in range(SW_KV_HEADS):
        pair, half = divmod(c, 2)
        kp = k3[:, pair * LANES:(pair + 1) * LANES]
        keep = lo if half == 0 else jnp.logical_not(lo)
        qs = []
        for g in range(group):
            t = pair * group + g
            qt = q_ref[0, :, t * LANES:(t + 1) * LANES]
            qs.append(jnp.where(keep, qt, jnp.zeros_like(qt)))
        qq = jnp.concatenate(qs, axis=0)
        s = lax.dot_general(qq, kp, NT_DIMS, preferred_element_type=F32)
        s_scr[c] = s + bias_ref[c * group:(c + 1) * group].reshape(group * blk, 3 * blk) + pen

    for pair in range(SW_KV_HEADS // 2):
        vp = v3[:, pair * LANES:(pair + 1) * LANES]
        per_half = []
        for half in range(2):
            c = 2 * pair + half
            sink = jnp.concatenate(
                [jnp.full((blk, LANES), sink_ref[c * group + g], F32) for g in range(group)], axis=0)
            per_half.append(_softmax_pv_half(s_scr[c], vp, sink, use_lo=(half == 0)))
        for g in range(group):
            t = pair * group + g
            o = jnp.where(lo, per_half[0][g * blk:(g + 1) * blk], per_half[1][g * blk:(g + 1) * blk])
            o_ref[0, :, t * LANES:(t + 1) * LANES] = o.astype(o_ref.dtype)


def sw_attention(q, k, v, bias, sink):
    b, t, c = q.shape
    nb = t // SW_BLOCK
    kvw = k.shape[2]
    grid_spec = pltpu.PrefetchScalarGridSpec(
        num_scalar_prefetch=1,
        grid=(b, nb),
        in_specs=[pl.BlockSpec((1, SW_BLOCK, c), lambda i, j, s: (i, j, 0)),
                  pl.BlockSpec((1, t, kvw), lambda i, j, s: (i, 0, 0)),
                  pl.BlockSpec((1, t, kvw), lambda i, j, s: (i, 0, 0)),
                  pl.BlockSpec(bias.shape, lambda i, j, s: (0, 0, 0))],
        out_specs=pl.BlockSpec((1, SW_BLOCK, c), lambda i, j, s: (i, j, 0)),
        scratch_shapes=[pltpu.VMEM((SW_KV_HEADS, (SW_HEADS // SW_KV_HEADS) * SW_BLOCK, 3 * SW_BLOCK), F32)],
    )
    return pl.pallas_call(
        functools.partial(_sw_kernel, nb=nb),
        grid_spec=grid_spec,
        out_shape=jax.ShapeDtypeStruct((b, t, c), BF16),
        compiler_params=_params("parallel", "arbitrary"),
        name="sw_attention",
    )(sink, q, k, v, bias)


def _t5_bucket(rel):
    half = T5_BUCKETS // 2
    max_exact = half // 2
    n = jnp.abs(rel)
    nf = jnp.maximum(n, 1).astype(F32)
    large = max_exact + (jnp.log(nf / max_exact) / math.log(T5_MAX_DIST / max_exact)
                         * (half - max_exact)).astype(jnp.int32)
    large = jnp.minimum(large, half - 1)
    return jnp.where(rel > 0, half, 0) + jnp.where(n < max_exact, n, large)


def sw_bias_table(t5_table):
    span = SW_BLOCK + 2 * SW_WINDOW
    rel = np.arange(span)[None, :] - SW_WINDOW - np.arange(SW_BLOCK)[:, None]
    bucket = _t5_bucket(jnp.asarray(rel, dtype=jnp.int32))[None]
    table = t5_table.astype(F32).T[:, :, None, None]
    bias = jnp.zeros((t5_table.shape[1], SW_BLOCK, span), F32)
    for b in range(T5_BUCKETS):
        bias = jnp.where(bucket == b, table[:, b], bias)
    return jnp.where(jnp.asarray(np.abs(rel) <= SW_WINDOW)[None], bias, NEG_INF)


def sw_pair_heads(w, axis):
    group = SW_HEADS // SW_KV_HEADS
    shape = w.shape
    split = shape[:axis] + (SW_KV_HEADS // 2, 2, group, SW_HEAD_DIM) + shape[axis + 1:]
    order = list(range(len(split)))
    order[axis + 1], order[axis + 2] = axis + 2, axis + 1
    return w.reshape(split).transpose(order).reshape(shape)


def _mx_kernel(q_ref, mk_ref, mv_ref, o_ref, *, scale):
    hd = q_ref.shape[2] // MX_HEADS
    heads = [slice(h * hd, (h + 1) * hd) for h in range(MX_HEADS)]
    scores = [lax.dot_general(q_ref[0, :, sl], mk_ref[0, :, sl], NT_DIMS, preferred_element_type=F32) * scale
              for sl in heads]
    for sl, s in zip(heads, scores):
        o_ref[0, :, sl] = _softmax_pv(s, mv_ref[0, :, sl]).astype(o_ref.dtype)


def mx_attention(q, mkv, tq=512):
    b, t, c = q.shape
    m = mkv.shape[1]
    tq = min(tq, t)
    return pl.pallas_call(
        functools.partial(_mx_kernel, scale=float((c // MX_HEADS) ** -0.5)),
        grid=(b, t // tq),
        in_specs=[pl.BlockSpec((1, tq, c), lambda i, j: (i, j, 0)),
                  pl.BlockSpec((1, m, c), lambda i, j: (i, 0, 0)),
                  pl.BlockSpec((1, m, c), lambda i, j: (i, 0, 1))],
        out_specs=pl.BlockSpec((1, tq, c), lambda i, j: (i, j, 0)),
        out_shape=jax.ShapeDtypeStruct((b, t, c), BF16),
        compiler_params=_params("parallel", "arbitrary"),
        name="mx_attention",
    )(q, mkv, mkv)


def _merge_kernel(x_ref, ona_ref, osw_ref, omx_ref, gmix_ref, wg_ref, wna_ref, wsw_ref, wmx_ref, wout_ref,
                  gffn_ref, wrh_ref, wrl_ref, z_ref, aff_ref, h_scr):
    j = pl.program_id(1)
    nstrip = z_ref.shape[1] // 3
    strips = range(nstrip)

    @pl.when(j == 0)
    def _():
        x = x_ref[...]
        h_scr[...] = _rms(x, gmix_ref[...]).astype(BF16)
        for c in strips:
            z_ref[:, c] = _to_row_tiles(x[:, c * LANES:(c + 1) * LANES])

    gates = jax.nn.sigmoid(jnp.dot(h_scr[...], wg_ref[...], preferred_element_type=F32))
    tn = wna_ref.shape[1]
    merged = None
    for o_ref, w_ref, b in ((ona_ref, wna_ref, 0), (osw_ref, wsw_ref, 1), (omx_ref, wmx_ref, 2)):
        term = gates[:, b * tn:(b + 1) * tn] * jnp.dot(o_ref[...], w_ref[...], preferred_element_type=F32)
        merged = term if merged is None else merged + term
    part = jnp.dot(merged.astype(BF16), wout_ref[...], preferred_element_type=F32)
    for c in strips:
        z_ref[:, c] += _to_row_tiles(part[:, c * LANES:(c + 1) * LANES])

    @pl.when(j == pl.num_programs(1) - 1)
    def _():
        x2 = jnp.concatenate([_from_row_tiles(z_ref[:, c]) for c in strips], axis=1)
        h2 = _rms(x2, gffn_ref[...])
        for c in strips:
            z_ref[:, nstrip + c] = _to_row_tiles(h2[:, c * LANES:(c + 1) * LANES])
            z_ref[:, 2 * nstrip + c] = jnp.zeros((z_ref.shape[0], SUBLANES, LANES), F32)
        hi = h2.astype(BF16)
        lo = (h2 - hi.astype(F32)).astype(BF16)
        wh = wrh_ref[...]
        logits = (lax.dot_general(wh, hi, NT_DIMS, preferred_element_type=F32)
                  + lax.dot_general(wh, lo, NT_DIMS, preferred_element_type=F32)
                  + lax.dot_general(wrl_ref[...], hi, NT_DIMS, preferred_element_type=F32))
        m = jnp.max(logits, axis=0, keepdims=True)
        e = jnp.exp(logits - m)
        aff = e / jnp.sum(e, axis=0, keepdims=True)
        for c in range(aff_ref.shape[0]):
            aff_ref[c] = aff[:, c * LANES:(c + 1) * LANES]


def merge(x, o_na, o_sw, o_mx, g_mix, w_gate, w_na_o, w_sw_o, w_mx_o, w_out, g_ffn, wr_hi, wr_lo,
          tm=512, tn=MERGE_TN):
    n, d = x.shape
    tm = min(tm, n)
    assert n % tm == 0 and tm % LANES == 0 and d % LANES == 0
    ne = wr_hi.shape[0]
    row = lambda i, j: (i, 0)
    return pl.pallas_call(
        _merge_kernel,
        grid=(n // tm, d // tn),
        in_specs=[pl.BlockSpec((tm, d), row),
                  pl.BlockSpec((tm, o_na.shape[1]), row),
                  pl.BlockSpec((tm, o_sw.shape[1]), row),
                  pl.BlockSpec((tm, o_mx.shape[1]), row),
                  pl.BlockSpec((1, d), lambda i, j: (0, 0)),
                  pl.BlockSpec((d, N_BRANCHES * tn), lambda i, j: (0, j)),
                  pl.BlockSpec((w_na_o.shape[0], tn), lambda i, j: (0, j)),
                  pl.BlockSpec((w_sw_o.shape[0], tn), lambda i, j: (0, j)),
                  pl.BlockSpec((w_mx_o.shape[0], tn), lambda i, j: (0, j)),
                  pl.BlockSpec((tn, d), lambda i, j: (j, 0)),
                  pl.BlockSpec((1, d), lambda i, j: (0, 0)),
                  pl.BlockSpec((ne, d), lambda i, j: (0, 0)),
                  pl.BlockSpec((ne, d), lambda i, j: (0, 0))],
        out_specs=[pl.BlockSpec(_row_tile_shape(tm, 3 * d), lambda i, j: (i, 0, 0, 0)),
                   pl.BlockSpec((tm // LANES, ne, LANES), lambda i, j: (i, 0, 0))],
        out_shape=[jax.ShapeDtypeStruct(_row_tile_shape(n, 3 * d), F32),
                   jax.ShapeDtypeStruct((n // LANES, ne, LANES), F32)],
        scratch_shapes=[pltpu.VMEM((tm, d), BF16)],
        compiler_params=_params("parallel", "arbitrary"),
        name="merge",
    )(x, o_na, o_sw, o_mx, g_mix, w_gate, w_na_o, w_sw_o, w_mx_o, w_out, g_ffn, wr_hi, wr_lo)


NA_WIDTH = NA_HEADS * NA_HEAD_DIM
SW_WIDTH = SW_HEADS * SW_HEAD_DIM
SW_KV_WIDTH = SW_KV_HEADS * SW_HEAD_DIM
QKV_WIDTHS = (NA_WIDTH, NA_WIDTH, NA_WIDTH, SW_WIDTH, SW_KV_WIDTH, SW_KV_WIDTH)


def prep_weights(g_mix, g_mem, w_in, w_mem_kv, na_rpb, t5_table, sw_sink, w_na_o, w_sw_o, w_mx_o, w_out,
                 g_ffn, w_router):
    d = w_in.shape[0]
    mx_width = w_mx_o.shape[0]
    widths = QKV_WIDTHS + (mx_width,)
    offs = np.concatenate([[0], np.cumsum(widths)])
    cols = [w_in[:, offs[i]:offs[i + 1]] for i in range(len(widths))]
    cols[3] = sw_pair_heads(cols[3], axis=1)
    w_gate = (w_in[:, offs[-1]:].reshape(d, N_BRANCHES, d // MERGE_TN, MERGE_TN).transpose(0, 2, 1, 3)
              .reshape(d, N_BRANCHES * d))
    wr = w_router.T.astype(F32)
    wr_hi = wr.astype(BF16)
    return dict(
        g_mix=g_mix.reshape(1, d).astype(F32), g_mem=g_mem.reshape(1, d).astype(F32),
        g_ffn=g_ffn.reshape(1, d).astype(F32),
        w_qkv=jnp.concatenate(cols, axis=1).astype(BF16), qkv_widths=widths,
        qkv_scales=(NA_HEAD_DIM ** -0.5, 1.0, 1.0, SW_HEAD_DIM ** -0.5, 1.0, 1.0, 1.0),
        w_mem_kv=w_mem_kv.astype(BF16),
        na_bias=na_bias_table(na_rpb), sw_bias=sw_bias_table(t5_table), sw_sink=sw_sink.astype(F32),
        w_gate=w_gate.astype(BF16), w_na_o=w_na_o.astype(BF16), w_sw_o=sw_pair_heads(w_sw_o, axis=0).astype(BF16),
        w_mx_o=w_mx_o.astype(BF16), w_out=w_out.astype(BF16),
        wr_hi=wr_hi, wr_lo=(wr - wr_hi.astype(F32)).astype(BF16))


def layer_front(x, mem, w, debug=False):
    b, t, d = x.shape
    m = mem.shape[1]
    xf = x.reshape(b * t, d)
    na_q, na_k, na_v, sw_q, sw_k, sw_v, mx_q = norm_proj(xf, w["g_mix"], w["w_qkv"], w["qkv_widths"],
                                                         w["qkv_scales"], tm=512)
    (mkv,) = norm_proj(mem.reshape(b * m, d), w["g_mem"], w["w_mem_kv"], (w["w_mem_kv"].shape[1],), (1.0,),
                       tm=512)
    r3 = lambda a: a.reshape(b, t, a.shape[1])
    o_na = na_attention(r3(na_q), r3(na_k), r3(na_v), w["na_bias"])
    o_sw = sw_attention(r3(sw_q), r3(sw_k), r3(sw_v), w["sw_bias"], w["sw_sink"])
    o_mx = mx_attention(r3(mx_q), mkv.reshape(b, m, mkv.shape[1]))
    f2 = lambda a: a.reshape(b * t, a.shape[2])
    z, aff = merge(xf, f2(o_na), f2(o_sw), f2(o_mx), w["g_mix"], w["w_gate"], w["w_na_o"], w["w_sw_o"],
                   w["w_mx_o"], w["w_out"], w["g_ffn"], w["wr_hi"], w["wr_lo"])
    if debug:
        return o_na, o_sw, o_mx, z, aff
    return z, aff


def _select_kernel(aff_ref, sel_ref, pos_ref, off_ref, *, cap):
    nc, ne, _ = aff_ref.shape
    bits = lax.bitcast_convert_type(aff_ref[...], I32)
    tok = (lax.broadcasted_iota(I32, bits.shape, 0) * LANES + lax.broadcasted_iota(I32, bits.shape, 2))

    def count(flags):
        return jnp.sum(flags, axis=(0, 2), keepdims=True)

    def value_step(i, prefix):
        cand = prefix | lax.shift_left(jnp.int32(1), 30 - i)
        cnt = count(jnp.where(bits >= cand, 1.0, 0.0))
        return jnp.where(cnt >= cap, cand, prefix)

    tau = lax.fori_loop(0, 31, value_step, jnp.zeros((1, ne, 1), I32))
    gt = bits > tau
    eq = bits == tau
    need = cap - count(jnp.where(gt, 1.0, 0.0))

    def index_step(i, last):
        cand = last | lax.shift_left(jnp.int32(1), 15 - i)
        cnt = count(jnp.where(eq, jnp.where(tok < cand, 1.0, 0.0), 0.0))
        return jnp.where(cnt < need, cand, last)

    last = lax.fori_loop(0, 16, index_step, jnp.zeros((1, ne, 1), I32))
    sel_ref[...] = jnp.where(gt, 1.0, jnp.where(eq, jnp.where(tok <= last, 1.0, 0.0), 0.0))

    upper = (lax.broadcasted_iota(I32, (LANES, LANES), 0) <= lax.broadcasted_iota(I32, (LANES, LANES), 1))
    upper = jnp.where(upper, 1.0, 0.0).astype(BF16)

    def chunk_step(c, off):
        s = sel_ref[c]
        cum = jnp.dot(s.astype(BF16), upper, preferred_element_type=F32)
        pos_ref[c] = off + cum - s
        off_ref[c] = jnp.broadcast_to(off, s.shape)
        return off + cum[:, LANES - 1:LANES]

    lax.fori_loop(0, nc, chunk_step, jnp.zeros((ne, 1), F32), unroll=CHUNK_UNROLL)


def select(aff, cap):
    nc, ne, _ = aff.shape
    assert nc * LANES <= 65536
    shape = jax.ShapeDtypeStruct(aff.shape, F32)
    return pl.pallas_call(
        functools.partial(_select_kernel, cap=float(cap)),
        out_shape=[shape, shape, shape],
        compiler_params=pltpu.CompilerParams(vmem_limit_bytes=VMEM_LIMIT),
        name="select",
    )(aff)


TOK_COL, CHUNK_COL, HI_COL = 0, 1, 2


def _compact_kernel(choff_ref, sel_ref, pos_ref, aff_ref, o_ref, tv_scr):
    e = pl.program_id(0)
    nc, ne, _ = aff_ref.shape
    win = 2 * LANES
    col = lax.broadcasted_iota(I32, (LANES, LANES), 1)
    row = lax.broadcasted_iota(I32, (LANES, LANES), 0)

    @pl.when(e == 0)
    def _():
        ecol = lax.broadcasted_iota(I32, (ne, LANES), 1) - lax.broadcasted_iota(I32, (ne, LANES), 0)
        place = [jnp.where(ecol == HI_COL + k * ne, 1.0, 0.0).astype(BF16) for k in range(3)]

        def build(c, carry):
            a = aff_ref[c]
            hi = a.astype(BF16)
            r1 = a - hi.astype(F32)
            mid = r1.astype(BF16)
            lo = (r1 - mid.astype(F32)).astype(BF16)
            rec = (lax.dot_general(hi, place[0], TN_DIMS, preferred_element_type=F32)
                   + lax.dot_general(mid, place[1], TN_DIMS, preferred_element_type=F32)
                   + lax.dot_general(lo, place[2], TN_DIMS, preferred_element_type=F32))
            rec = rec + jnp.where(col == TOK_COL, row, jnp.where(col == CHUNK_COL, c, 0)).astype(F32)
            tv_scr[c] = rec.astype(BF16)
            return carry

        lax.fori_loop(0, nc, build, 0, unroll=CHUNK_UNROLL)

    o_ref[...] = jnp.zeros_like(o_ref)
    slot0 = lax.broadcasted_iota(I32, (win, LANES), 0).astype(F32)

    def body(c, carry):
        off = choff_ref[c * ne + e]
        base = pl.multiple_of((off // LANES) * LANES, LANES)
        s = sel_ref[c, pl.ds(e, 1), :]
        p = pos_ref[c, pl.ds(e, 1), :] - base.astype(F32)
        onehot = jnp.where(s > 0.0, jnp.where(slot0 == p, 1.0, 0.0), 0.0).astype(BF16)
        o_ref[0, pl.ds(base, win), :] += jnp.dot(onehot, tv_scr[c], preferred_element_type=F32)
        return carry

    lax.fori_loop(0, nc, body, 0, unroll=CHUNK_UNROLL)


def compact(sel, pos, choff, aff, cap):
    nc, ne, _ = aff.shape
    assert nc <= 256 and HI_COL + 3 * ne <= LANES and cap % LANES == 0
    rows = cap + 2 * LANES
    full = pl.BlockSpec(aff.shape, lambda e, s: (0, 0, 0))
    rec = pl.pallas_call(
        _compact_kernel,
        grid_spec=pltpu.PrefetchScalarGridSpec(
            num_scalar_prefetch=1, grid=(ne,),
            in_specs=[full, full, full],
            out_specs=pl.BlockSpec((1, rows, LANES), lambda e, s: (e, 0, 0)),
            scratch_shapes=[pltpu.VMEM((nc, LANES, LANES), BF16)]),
        out_shape=jax.ShapeDtypeStruct((ne, rows, LANES), F32),
        compiler_params=_params("arbitrary"),
        name="compact",
    )(choff, sel, pos, aff)
    idx = (rec[:, :cap, CHUNK_COL] * LANES + rec[:, :cap, TOK_COL]).astype(I32)
    return idx, rec


def _expert_kernel(idx_prev, idx_cur, idx_next, rec_ref, zin_ref, wg_ref, wu_ref, wd_ref, z_ref,
                   buf, xs, acc, gsem, ssem, *, ts, nf, s_tiles, ne):
    del zin_ref
    sub = z_ref.shape[1] // 3
    t = pl.program_id(0)
    f = pl.program_id(1)
    last_t = pl.num_programs(0) - 1
    rps = ts // nf
    slot, nslot, pslot = t % 3, (t + 1) % 3, (t + 2) % 3
    odd = (t // s_tiles) % 2
    nodd = (jnp.minimum(t + 1, last_t) // s_tiles) % 2
    podd = (jnp.maximum(t - 1, 0) // s_tiles) % 2

    def rows(ref, row, first, count):
        return ref.at[row >> 3, pl.ds(first, count), pl.ds(row & (SUBLANES - 1), 1), :]

    def gather(token, ra, rs, to_slot, is_odd):
        return pltpu.make_async_copy(rows(z_ref, token, is_odd * sub, 2 * sub),
                                     buf.at[to_slot, ra, :, pl.ds(rs, 1), :], gsem.at[to_slot])

    def scatter(token, ra, rs, from_slot, is_odd):
        return pltpu.make_async_copy(buf.at[from_slot, ra, pl.ds(is_odd * sub, sub), pl.ds(rs, 1), :],
                                     rows(z_ref, token, is_odd * 2 * sub, sub), ssem.at[from_slot])

    def wait_gather(s):
        pltpu.make_async_copy(z_ref.at[pl.ds(0, ts // SUBLANES), pl.ds(0, 2 * sub)], buf.at[s], gsem.at[s]).wait()

    def wait_scatter(s):
        pltpu.make_async_copy(buf.at[s, :, pl.ds(0, sub)], z_ref.at[pl.ds(0, ts // SUBLANES), pl.ds(0, sub)],
                              ssem.at[s]).wait()

    def for_rows(fn):
        def body(r, carry):
            fn(idx_cur[0, 0, r], r >> 3, r & (SUBLANES - 1))
            return carry
        lax.fori_loop(0, ts, body, 0)

    @pl.when(f == 0)
    def _():
        @pl.when(t == 0)
        def _():
            def start(token, ra, rs):
                gather(token, ra, rs, 0, 0).start()
                gather(token, ra, rs, 2, 0).start()
            for_rows(start)
            wait_gather(2)

        acc[...] = jnp.zeros(acc.shape, F32)
        wait_gather(slot)
        h2_first = (1 - odd) * sub
        for c in range(sub):
            xs[:, c * LANES:(c + 1) * LANES] = _from_row_tiles(buf[slot, :, h2_first + c]).astype(BF16)

    for k in range(rps):
        r = f * rps + k
        ra, rs = f * (rps // SUBLANES) + k // SUBLANES, k % SUBLANES
        gather(idx_next[0, 0, r], ra, rs, nslot, nodd).start()
        scatter(idx_prev[0, 0, r], ra, rs, pslot, podd).start()

    x = xs[...]
    a = jnp.dot(x, wg_ref[0], preferred_element_type=F32)
    b = jnp.dot(x, wu_ref[0], preferred_element_type=F32)
    hm = (jax.nn.silu(a) * b).astype(BF16)
    acc[...] += jnp.dot(hm, wd_ref[0], preferred_element_type=F32)

    @pl.when(f == nf - 1)
    def _():
        lane = lax.broadcasted_iota(I32, (1, LANES), 1) - (HI_COL + t // s_tiles)
        own = (lane == 0) | (lane == ne) | (lane == 2 * ne)
        contrib = acc[...] * jnp.sum(jnp.where(own, rec_ref[0], 0.0), axis=1, keepdims=True)
        y_first = odd * sub
        for c in range(sub):
            buf[slot, :, y_first + c] += _to_row_tiles(contrib[:, c * LANES:(c + 1) * LANES])
        wait_scatter(pslot)

        @pl.when(t == last_t)
        def _():
            wait_gather(nslot)
            for_rows(lambda token, ra, rs: scatter(token, ra, rs, slot, odd).start())
            wait_scatter(slot)


def expert_ffn(idx, rec, z, w_gate, w_up, w_down, tf=512):
    ne, cap = idx.shape
    d = z.shape[1] // 3 * LANES
    ff = w_gate.shape[2]
    ts = EXPERT_TS
    tf = min(tf, ff)
    nf = ff // tf
    s_tiles = cap // ts
    assert cap % ts == 0 and s_tiles >= 2 and ts % nf == 0
    nt = ne * s_tiles
    idx3 = idx.reshape(nt, 1, ts)
    smem_tile = lambda shift: pl.BlockSpec(
        (1, 1, ts), lambda t, f: (jnp.clip(t + shift, 0, nt - 1), 0, 0), memory_space=pltpu.SMEM)
    return pl.pallas_call(
        functools.partial(_expert_kernel, ts=ts, nf=nf, s_tiles=s_tiles, ne=ne),
        grid=(nt, nf),
        in_specs=[smem_tile(-1), smem_tile(0), smem_tile(1),
                  pl.BlockSpec((1, ts, LANES), lambda t, f: (t // s_tiles, t % s_tiles, 0)),
                  pl.BlockSpec(memory_space=pl.ANY),
                  pl.BlockSpec((1, d, tf), lambda t, f: (t // s_tiles, 0, f)),
                  pl.BlockSpec((1, d, tf), lambda t, f: (t // s_tiles, 0, f)),
                  pl.BlockSpec((1, tf, d), lambda t, f: (t // s_tiles, f, 0))],
        out_specs=pl.BlockSpec(memory_space=pl.ANY),
        out_shape=jax.ShapeDtypeStruct(z.shape, F32),
        scratch_shapes=[pltpu.VMEM((3,) + _row_tile_shape(ts, 2 * d), F32), pltpu.VMEM((ts, d), BF16),
                        pltpu.VMEM((ts, d), F32), pltpu.SemaphoreType.DMA((3,)), pltpu.SemaphoreType.DMA((3,))],
        input_output_aliases={4: 0},
        compiler_params=_params("arbitrary", "arbitrary"),
        name="expert_ffn",
    )(idx3, idx3, idx3, rec, z, w_gate, w_up, w_down)


def _final_norm_kernel(y0_ref, y1_ref, g_ref, o_ref):
    x = jnp.concatenate([_from_row_tiles(y0_ref[:, c] + y1_ref[:, c]) for c in range(y0_ref.shape[1])], axis=1)
    o_ref[...] = _rms(x, g_ref[...])


def final_norm(z, g, tm=512):
    n, d = z.shape[0] * SUBLANES, z.shape[1] // 3 * LANES
    tm = min(tm, n)
    assert n % tm == 0
    return pl.pallas_call(
        _final_norm_kernel,
        grid=(n // tm,),
        in_specs=[pl.BlockSpec(_row_tile_shape(tm, d), lambda i: (i, 0, 0, 0)),
                  pl.BlockSpec(_row_tile_shape(tm, d), lambda i: (i, 2, 0, 0)),
                  pl.BlockSpec((1, d), lambda i: (0, 0))],
        out_specs=pl.BlockSpec((tm, d), lambda i: (i, 0)),
        out_shape=jax.ShapeDtypeStruct((n, d), F32),
        compiler_params=_params("parallel"),
        name="final_norm",
    )(z, z, g)


def moe_and_norm(z, aff, w_e_gate, w_e_up, w_e_down, g_final):
    n = z.shape[0] * SUBLANES
    cap = EC_CAPACITY * n // N_EXPERTS
    sel, pos, off = select(aff, cap)
    choff = off[:, :, 0].astype(I32).reshape(-1)
    idx, rec = compact(sel, pos, choff, aff, cap)
    z = expert_ffn(idx, rec, z, w_e_gate, w_e_up, w_e_down)
    return final_norm(z, g_final)


def encoder_group(x, mem, w, experts, g_final):
    b, t, d = x.shape
    z, aff = layer_front(x, mem, w)
    return moe_and_norm(z, aff, *experts, g_final).reshape(b, t, d)


def kernel(x_prompt, x_sample, mem_prompt, mem_sample, g_mix, g_mem, w_in, w_mem_kv, na_rpb, t5_table, sw_sink,
           w_na_o, w_sw_o, w_mx_o, w_out, g_ffn, w_router, w_e_gate, w_e_up, w_e_down, g_final):
    assert g_mix.shape[0] == 1, "single-layer trunk"
    w = prep_weights(g_mix[0], g_mem[0], w_in[0], w_mem_kv[0], na_rpb[0], t5_table, sw_sink[0], w_na_o[0],
                     w_sw_o[0], w_mx_o[0], w_out[0], g_ffn[0], w_router[0])
    experts = (w_e_gate[0].astype(BF16), w_e_up[0].astype(BF16), w_e_down[0].astype(BF16))
    gf = g_final.reshape(1, -1).astype(F32)
    y_prompt = encoder_group(x_prompt, mem_prompt, w, experts, gf)
    y_sample = encoder_group(x_sample, mem_sample, w, experts, gf)
    return (y_prompt, y_sample)
```

```python
import functools
import math

import numpy as np
import jax
import jax.numpy as jnp
from jax import lax
from jax.experimental import pallas as pl
from jax.experimental.pallas import tpu as pltpu

F32 = jnp.float32
BF16 = jnp.bfloat16
I32 = jnp.int32

RMS_EPS = 1e-6
NEG_INF = -1e30

GRID_W = 64
NA_HEADS = 8
NA_HEAD_DIM = 64
NA_KR = 8
NA_KC = 16
SW_HEADS = 16
SW_KV_HEADS = 4
SW_HEAD_DIM = 64
SW_WINDOW = 128
SW_BLOCK = 128
MX_HEADS = 4
T5_BUCKETS = 32
T5_MAX_DIST = 128
N_BRANCHES = 3
N_EXPERTS = 16
EC_CAPACITY = 2

LANES = 128
SUBLANES = 8
V7X_VMEM_BYTES = 64 * 1024 * 1024
VMEM_LIMIT = V7X_VMEM_BYTES * 7 // 8

CHUNK_UNROLL = 8
EXPERT_TS = 512
MERGE_TN = 256

NT_DIMS = (((1,), (1,)), ((), ()))
TN_DIMS = (((0,), (0,)), ((), ()))


def _params(*sem):
    return pltpu.CompilerParams(dimension_semantics=sem, vmem_limit_bytes=VMEM_LIMIT)


def _rms(x, g):
    return x * lax.rsqrt(jnp.mean(x * x, axis=-1, keepdims=True) + RMS_EPS) * g


def _to_row_tiles(strip):
    return strip.reshape(strip.shape[0] // SUBLANES, SUBLANES, LANES)


def _from_row_tiles(tiles):
    return tiles.reshape(tiles.shape[0] * SUBLANES, LANES)


def _row_tile_shape(rows, d):
    return (rows // SUBLANES, d // LANES, SUBLANES, LANES)


def _norm_proj_kernel(x_ref, g_ref, w_ref, *o_refs, scales):
    h = _rms(x_ref[...], g_ref[...]).astype(BF16)
    off = 0
    for o_ref, sc in zip(o_refs, scales):
        width = o_ref.shape[1]
        for c0 in range(0, width, 512):
            cw = min(512, width - c0)
            r = jnp.dot(h, w_ref[:, off + c0:off + c0 + cw], preferred_element_type=F32)
            if sc != 1.0:
                r = r * sc
            o_ref[:, c0:c0 + cw] = r.astype(o_ref.dtype)
        off += width


def norm_proj(x, g, w, widths, scales, tm):
    n, d = x.shape
    tm = min(tm, n)
    assert n % tm == 0
    return pl.pallas_call(
        functools.partial(_norm_proj_kernel, scales=tuple(scales)),
        grid=(n // tm,),
        in_specs=[pl.BlockSpec((tm, d), lambda i: (i, 0)),
                  pl.BlockSpec((1, d), lambda i: (0, 0)),
                  pl.BlockSpec(w.shape, lambda i: (0, 0))],
        out_specs=[pl.BlockSpec((tm, c), lambda i: (i, 0)) for c in widths],
        out_shape=[jax.ShapeDtypeStruct((n, c), BF16) for c in widths],
        compiler_params=_params("parallel"),
        name="norm_proj",
    )(x, g, w)


def _softmax_pv(s, v):
    m = jnp.max(s, axis=-1, keepdims=True)
    e = jnp.exp(s - m)
    den = jnp.sum(e, axis=-1, keepdims=True)
    o = jnp.dot(e.astype(BF16), v, preferred_element_type=F32)
    return o / den


def _softmax_pv_half(s, v, extra_logit, use_lo):
    rows, keys = s.shape
    mb = jnp.maximum(jnp.broadcast_to(jnp.max(s, axis=-1, keepdims=True), (rows, LANES)), extra_logit)
    e = jnp.concatenate([jnp.exp(s[:, t * LANES:(t + 1) * LANES] - mb) for t in range(keys // LANES)],
                        axis=1).astype(BF16)
    lo = lax.broadcasted_iota(I32, v.shape, 1) < LANES // 2
    keep = lo if use_lo else jnp.logical_not(lo)
    o = jnp.dot(e, jnp.where(keep, v, jnp.ones_like(v)), preferred_element_type=F32)
    den = pltpu.roll(o + jnp.exp(extra_logit - mb), LANES // 2, 1)
    return o / den


def _na_kernel(q_ref, k_ref, v_ref, bias_ref, o_ref, *, rows, rb):
    j = pl.program_id(1)
    lo = lax.broadcasted_iota(I32, (GRID_W, LANES), 1) < NA_HEAD_DIM
    nkeys = NA_KR * GRID_W

    def body(i, carry):
        r = j * rb + i
        rs = jnp.clip(r - NA_KR // 2, 0, rows - NA_KR)
        off = r - rs
        q = q_ref[0, pl.ds(pl.multiple_of(i * GRID_W, GRID_W), GRID_W), :]
        kk = k_ref[0, pl.ds(pl.multiple_of(rs * GRID_W, GRID_W), nkeys), :]
        vv = v_ref[0, pl.ds(pl.multiple_of(rs * GRID_W, GRID_W), nkeys), :]
        scores = []
        for p in range(NA_HEADS // 2):
            qp = q[:, p * LANES:(p + 1) * LANES]
            kp = kk[:, p * LANES:(p + 1) * LANES]
            zero = jnp.zeros_like(qp)
            q2 = jnp.concatenate([jnp.where(lo, qp, zero), jnp.where(lo, zero, qp)], axis=0)
            s = lax.dot_general(q2, kp, NT_DIMS, preferred_element_type=F32)
            scores.append(s + bias_ref[off, 2 * p:2 * p + 2].reshape(2 * GRID_W, nkeys))
        outs = []
        for p in range(NA_HEADS // 2):
            o2 = _softmax_pv(scores[p], vv[:, p * LANES:(p + 1) * LANES])
            outs.append(jnp.where(lo, o2[:GRID_W], o2[GRID_W:]))
        o_ref[0, pl.ds(pl.multiple_of(i * GRID_W, GRID_W), GRID_W), :] = (
            jnp.concatenate(outs, axis=1).astype(o_ref.dtype))
        return carry

    lax.fori_loop(0, rb, body, 0, unroll=2)


def na_attention(q, k, v, bias, rb=8):
    b, t, c = q.shape
    rows = t // GRID_W
    assert rows >= NA_KR and rows % rb == 0
    return pl.pallas_call(
        functools.partial(_na_kernel, rows=rows, rb=rb),
        grid=(b, rows // rb),
        in_specs=[pl.BlockSpec((1, rb * GRID_W, c), lambda i, j: (i, j, 0)),
                  pl.BlockSpec((1, t, c), lambda i, j: (i, 0, 0)),
                  pl.BlockSpec((1, t, c), lambda i, j: (i, 0, 0)),
                  pl.BlockSpec(bias.shape, lambda i, j: (0, 0, 0, 0))],
        out_specs=pl.BlockSpec((1, rb * GRID_W, c), lambda i, j: (i, j, 0)),
        out_shape=jax.ShapeDtypeStruct((b, t, c), BF16),
        compiler_params=_params("parallel", "arbitrary"),
        name="na_attention",
    )(q, k, v, bias)


def na_bias_table(rpb):
    col = np.arange(GRID_W)
    col_start = np.clip(col - NA_KC // 2, 0, GRID_W - NA_KC)
    in_win = (col[None, :] >= col_start[:, None]) & (col[None, :] < col_start[:, None] + NA_KC)
    dc = np.clip(col[None, :] - col[:, None] + NA_KC - 1, 0, 2 * NA_KC - 2)
    heads = rpb.shape[0]
    by_col = jnp.take(rpb.astype(F32), jnp.asarray(dc.reshape(-1)), axis=2)
    by_col = by_col.reshape(heads, 2 * NA_KR - 1, GRID_W, GRID_W)
    by_col = jnp.where(jnp.asarray(in_win)[None, None], by_col, NEG_INF)
    per_off = [by_col[:, NA_KR - 1 - off:2 * NA_KR - 1 - off].transpose(0, 2, 1, 3) for off in range(NA_KR)]
    return jnp.stack(per_off).reshape(NA_KR, heads, GRID_W, NA_KR * GRID_W)


def _sw_kernel(sink_ref, q_ref, k_ref, v_ref, bias_ref, o_ref, s_scr, *, nb):
    n = pl.program_id(1)
    blk = SW_BLOCK
    group = SW_HEADS // SW_KV_HEADS

    def rows_of(ref, c):
        return ref[0, pl.ds(pl.multiple_of(c * blk, blk), blk), :]

    cl = jnp.maximum(n - 1, 0)
    cr = jnp.minimum(n + 1, nb - 1)
    k3 = jnp.concatenate([rows_of(k_ref, cl), rows_of(k_ref, n), rows_of(k_ref, cr)], axis=0)
    v3 = jnp.concatenate([rows_of(v_ref, cl), rows_of(v_ref, n), rows_of(v_ref, cr)], axis=0)
    pen_l = jnp.where(n > 0, 0.0, NEG_INF).astype(F32)
    pen_r = jnp.where(n < nb - 1, 0.0, NEG_INF).astype(F32)
    key = lax.broadcasted_iota(I32, (1, 3 * blk), 1)
    pen = jnp.where(key < blk, pen_l, jnp.where(key >= 2 * blk, pen_r, 0.0))
    lo = lax.broadcasted_iota(I32, (blk, LANES), 1) < SW_HEAD_DIM

    for c in range(SW_KV_HEADS):
        pair, half = divmod(c, 2)
        kp = k3[:, pair * LANES:(pair + 1) * LANES]
        keep = lo if half == 0 else jnp.logical_not(lo)
        qs = []
        for g in range(group):
            t = pair * group + g
            qt = q_ref[0, :, t * LANES:(t + 1) * LANES]
            qs.append(jnp.where(keep, qt, jnp.zeros_like(qt)))
        qq = jnp.concatenate(qs, axis=0)
        s = lax.dot_general(qq, kp, NT_DIMS, preferred_element_type=F32)
        s_scr[c] = s + bias_ref[c * group:(c + 1) * group].reshape(group * blk, 3 * blk) + pen

    for pair in range(SW_KV_HEADS // 2):
        vp = v3[:, pair * LANES:(pair + 1) * LANES]
        per_half = []
        for half in range(2):
            c = 2 * pair + half
            sink = jnp.concatenate(
                [jnp.full((blk, LANES), sink_ref[c * group + g], F32) for g in range(group)], axis=0)
            per_half.append(_softmax_pv_half(s_scr[c], vp, sink, use_lo=(half == 0)))
        for g in range(group):
            t = pair * group + g
            o = jnp.where(lo, per_half[0][g * blk:(g + 1) * blk], per_half[1][g * blk:(g + 1) * blk])
            o_ref[0, :, t * LANES:(t + 1) * LANES] = o.astype(o_ref.dtype)


def sw_attention(q, k, v, bias, sink):
    b, t, c = q.shape
    nb = t // SW_BLOCK
    kvw = k.shape[2]
    grid_spec = pltpu.PrefetchScalarGridSpec(
        num_scalar_prefetch=1,
        grid=(b, nb),
        in_specs=[pl.BlockSpec((1, SW_BLOCK, c), lambda i, j, s: (i, j, 0)),
                  pl.BlockSpec((1, t, kvw), lambda i, j, s: (i, 0, 0)),
                  pl.BlockSpec((1, t, kvw), lambda i, j, s: (i, 0, 0)),
                  pl.BlockSpec(bias.shape, lambda i, j, s: (0, 0, 0))],
        out_specs=pl.BlockSpec((1, SW_BLOCK, c), lambda i, j, s: (i, j, 0)),
        scratch_shapes=[pltpu.VMEM((SW_KV_HEADS, (SW_HEADS // SW_KV_HEADS) * SW_BLOCK, 3 * SW_BLOCK), F32)],
    )
    return pl.pallas_call(
        functools.partial(_sw_kernel, nb=nb),
        grid_spec=grid_spec,
        out_shape=jax.ShapeDtypeStruct((b, t, c), BF16),
        compiler_params=_params("parallel", "arbitrary"),
        name="sw_attention",
    )(sink, q, k, v, bias)


def _t5_bucket(rel):
    half = T5_BUCKETS // 2
    max_exact = half // 2
    n = jnp.abs(rel)
    nf = jnp.maximum(n, 1).astype(F32)
    large = max_exact + (jnp.log(nf / max_exact) / math.log(T5_MAX_DIST / max_exact)
                         * (half - max_exact)).astype(jnp.int32)
    large = jnp.minimum(large, half - 1)
    return jnp.where(rel > 0, half, 0) + jnp.where(n < max_exact, n, large)


def sw_bias_table(t5_table):
    span = SW_BLOCK + 2 * SW_WINDOW
    rel = np.arange(span)[None, :] - SW_WINDOW - np.arange(SW_BLOCK)[:, None]
    bucket = _t5_bucket(jnp.asarray(rel, dtype=jnp.int32))[None]
    table = t5_table.astype(F32).T[:, :, None, None]
    bias = jnp.zeros((t5_table.shape[1], SW_BLOCK, span), F32)
    for b in range(T5_BUCKETS):
        bias = jnp.where(bucket == b, table[:, b], bias)
    return jnp.where(jnp.asarray(np.abs(rel) <= SW_WINDOW)[None], bias, NEG_INF)


def sw_pair_heads(w, axis):
    group = SW_HEADS // SW_KV_HEADS
    shape = w.shape
    split = shape[:axis] + (SW_KV_HEADS // 2, 2, group, SW_HEAD_DIM) + shape[axis + 1:]
    order = list(range(len(split)))
    order[axis + 1], order[axis + 2] = axis + 2, axis + 1
    return w.reshape(split).transpose(order).reshape(shape)


def _mx_kernel(q_ref, mk_ref, mv_ref, o_ref, *, scale):
    hd = q_ref.shape[2] // MX_HEADS
    heads = [slice(h * hd, (h + 1) * hd) for h in range(MX_HEADS)]
    scores = [lax.dot_general(q_ref[0, :, sl], mk_ref[0, :, sl], NT_DIMS, preferred_element_type=F32) * scale
              for sl in heads]
    for sl, s in zip(heads, scores):
        o_ref[0, :, sl] = _softmax_pv(s, mv_ref[0, :, sl]).astype(o_ref.dtype)


def mx_attention(q, mkv, tq=512):
    b, t, c = q.shape
    m = mkv.shape[1]
    tq = min(tq, t)
    return pl.pallas_call(
        functools.partial(_mx_kernel, scale=float((c // MX_HEADS) ** -0.5)),
        grid=(b, t // tq),
        in_specs=[pl.BlockSpec((1, tq, c), lambda i, j: (i, j, 0)),
                  pl.BlockSpec((1, m, c), lambda i, j: (i, 0, 0)),
                  pl.BlockSpec((1, m, c), lambda i, j: (i, 0, 1))],
        out_specs=pl.BlockSpec((1, tq, c), lambda i, j: (i, j, 0)),
        out_shape=jax.ShapeDtypeStruct((b, t, c), BF16),
        compiler_params=_params("parallel", "arbitrary"),
        name="mx_attention",
    )(q, mkv, mkv)


def _merge_kernel(x_ref, ona_ref, osw_ref, omx_ref, gmix_ref, wg_ref, wna_ref, wsw_ref, wmx_ref, wout_ref,
                  gffn_ref, wrh_ref, wrl_ref, z_ref, aff_ref, h_scr):
    j = pl.program_id(1)
    nstrip = z_ref.shape[1] // 3
    strips = range(nstrip)

    @pl.when(j == 0)
    def _():
        x = x_ref[...]
        h_scr[...] = _rms(x, gmix_ref[...]).astype(BF16)
        for c in strips:
            z_ref[:, c] = _to_row_tiles(x[:, c * LANES:(c + 1) * LANES])

    gates = jax.nn.sigmoid(jnp.dot(h_scr[...], wg_ref[...], preferred_element_type=F32))
    tn = wna_ref.shape[1]
    merged = None
    for o_ref, w_ref, b in ((ona_ref, wna_ref, 0), (osw_ref, wsw_ref, 1), (omx_ref, wmx_ref, 2)):
        term = gates[:, b * tn:(b + 1) * tn] * jnp.dot(o_ref[...], w_ref[...], preferred_element_type=F32)
        merged = term if merged is None else merged + term
    part = jnp.dot(merged.astype(BF16), wout_ref[...], preferred_element_type=F32)
    for c in strips:
        z_ref[:, c] += _to_row_tiles(part[:, c * LANES:(c + 1) * LANES])

    @pl.when(j == pl.num_programs(1) - 1)
    def _():
        x2 = jnp.concatenate([_from_row_tiles(z_ref[:, c]) for c in strips], axis=1)
        h2 = _rms(x2, gffn_ref[...])
        for c in strips:
            z_ref[:, nstrip + c] = _to_row_tiles(h2[:, c * LANES:(c + 1) * LANES])
            z_ref[:, 2 * nstrip + c] = jnp.zeros((z_ref.shape[0], SUBLANES, LANES), F32)
        hi = h2.astype(BF16)
        lo = (h2 - hi.astype(F32)).astype(BF16)
        wh = wrh_ref[...]
        logits = (lax.dot_general(wh, hi, NT_DIMS, preferred_element_type=F32)
                  + lax.dot_general(wh, lo, NT_DIMS, preferred_element_type=F32)
                  + lax.dot_general(wrl_ref[...], hi, NT_DIMS, preferred_element_type=F32))
        m = jnp.max(logits, axis=0, keepdims=True)
        e = jnp.exp(logits - m)
        aff = e / jnp.sum(e, axis=0, keepdims=True)
        for c in range(aff_ref.shape[0]):
            aff_ref[c] = aff[:, c * LANES:(c + 1) * LANES]


def merge(x, o_na, o_sw, o_mx, g_mix, w_gate, w_na_o, w_sw_o, w_mx_o, w_out, g_ffn, wr_hi, wr_lo,
          tm=512, tn=MERGE_TN):
    n, d = x.shape
    tm = min(tm, n)
    assert n % tm == 0 and tm % LANES == 0 and d % LANES == 0
    ne = wr_hi.shape[0]
    row = lambda i, j: (i, 0)
    return pl.pallas_call(
        _merge_kernel,
        grid=(n // tm, d // tn),
        in_specs=[pl.BlockSpec((tm, d), row),
                  pl.BlockSpec((tm, o_na.shape[1]), row),
                  pl.BlockSpec((tm, o_sw.shape[1]), row),
                  pl.BlockSpec((tm, o_mx.shape[1]), row),
                  pl.BlockSpec((1, d), lambda i, j: (0, 0)),
                  pl.BlockSpec((d, N_BRANCHES * tn), lambda i, j: (0, j)),
                  pl.BlockSpec((w_na_o.shape[0], tn), lambda i, j: (0, j)),
                  pl.BlockSpec((w_sw_o.shape[0], tn), lambda i, j: (0, j)),
                  pl.BlockSpec((w_mx_o.shape[0], tn), lambda i, j: (0, j)),
                  pl.BlockSpec((tn, d), lambda i, j: (j, 0)),
                  pl.BlockSpec((1, d), lambda i, j: (0, 0)),
                  pl.BlockSpec((ne, d), lambda i, j: (0, 0)),
                  pl.BlockSpec((ne, d), lambda i, j: (0, 0))],
        out_specs=[pl.BlockSpec(_row_tile_shape(tm, 3 * d), lambda i, j: (i, 0, 0, 0)),
                   pl.BlockSpec((tm // LANES, ne, LANES), lambda i, j: (i, 0, 0))],
        out_shape=[jax.ShapeDtypeStruct(_row_tile_shape(n, 3 * d), F32),
                   jax.ShapeDtypeStruct((n // LANES, ne, LANES), F32)],
        scratch_shapes=[pltpu.VMEM((tm, d), BF16)],
        compiler_params=_params("parallel", "arbitrary"),
        name="merge",
    )(x, o_na, o_sw, o_mx, g_mix, w_gate, w_na_o, w_sw_o, w_mx_o, w_out, g_ffn, wr_hi, wr_lo)


NA_WIDTH = NA_HEADS * NA_HEAD_DIM
SW_WIDTH = SW_HEADS * SW_HEAD_DIM
SW_KV_WIDTH = SW_KV_HEADS * SW_HEAD_DIM
QKV_WIDTHS = (NA_WIDTH, NA_WIDTH, NA_WIDTH, SW_WIDTH, SW_KV_WIDTH, SW_KV_WIDTH)


def prep_weights(g_mix, g_mem, w_in, w_mem_kv, na_rpb, t5_table, sw_sink, w_na_o, w_sw_o, w_mx_o, w_out,
                 g_ffn, w_router):
    d = w_in.shape[0]
    mx_width = w_mx_o.shape[0]
    widths = QKV_WIDTHS + (mx_width,)
    offs = np.concatenate([[0], np.cumsum(widths)])
    cols = [w_in[:, offs[i]:offs[i + 1]] for i in range(len(widths))]
    cols[3] = sw_pair_heads(cols[3], axis=1)
    w_gate = (w_in[:, offs[-1]:].reshape(d, N_BRANCHES, d // MERGE_TN, MERGE_TN).transpose(0, 2, 1, 3)
              .reshape(d, N_BRANCHES * d))
    wr = w_router.T.astype(F32)
    wr_hi = wr.astype(BF16)
    return dict(
        g_mix=g_mix.reshape(1, d).astype(F32), g_mem=g_mem.reshape(1, d).astype(F32),
        g_ffn=g_ffn.reshape(1, d).astype(F32),
        w_qkv=jnp.concatenate(cols, axis=1).astype(BF16), qkv_widths=widths,
        qkv_scales=(NA_HEAD_DIM ** -0.5, 1.0, 1.0, SW_HEAD_DIM ** -0.5, 1.0, 1.0, 1.0),
        w_mem_kv=w_mem_kv.astype(BF16),
        na_bias=na_bias_table(na_rpb), sw_bias=sw_bias_table(t5_table), sw_sink=sw_sink.astype(F32),
        w_gate=w_gate.astype(BF16), w_na_o=w_na_o.astype(BF16), w_sw_o=sw_pair_heads(w_sw_o, axis=0).astype(BF16),
        w_mx_o=w_mx_o.astype(BF16), w_out=w_out.astype(BF16),
        wr_hi=wr_hi, wr_lo=(wr - wr_hi.astype(F32)).astype(BF16))


def layer_front(x, mem, w):
    b, t, d = x.shape
    m = mem.shape[1]
    xf = x.reshape(b * t, d)
    na_q, na_k, na_v, sw_q, sw_k, sw_v, mx_q = norm_proj(xf, w["g_mix"], w["w_qkv"], w["qkv_widths"],
                                                         w["qkv_scales"], tm=512)
    (mkv,) = norm_proj(mem.reshape(b * m, d), w["g_mem"], w["w_mem_kv"], (w["w_mem_kv"].shape[1],), (1.0,),
                       tm=512)
    r3 = lambda a: a.reshape(b, t, a.shape[1])
    o_na = na_attention(r3(na_q), r3(na_k), r3(na_v), w["na_bias"])
    o_sw = sw_attention(r3(sw_q), r3(sw_k), r3(sw_v), w["sw_bias"], w["sw_sink"])
    o_mx = mx_attention(r3(mx_q), mkv.reshape(b, m, mkv.shape[1]))
    f2 = lambda a: a.reshape(b * t, a.shape[2])
    return merge(xf, f2(o_na), f2(o_sw), f2(o_mx), w["g_mix"], w["w_gate"], w["w_na_o"], w["w_sw_o"],
                 w["w_mx_o"], w["w_out"], w["g_ffn"], w["wr_hi"], w["wr_lo"])


def _select_kernel(aff_ref, sel_ref, pos_ref, off_ref, *, cap):
    nc, ne, _ = aff_ref.shape
    bits = lax.bitcast_convert_type(aff_ref[...], I32)
    tok = (lax.broadcasted_iota(I32, bits.shape, 0) * LANES + lax.broadcasted_iota(I32, bits.shape, 2))

    def count(flags):
        return jnp.sum(flags, axis=(0, 2), keepdims=True)

    def value_step(i, prefix):
        cand = prefix | lax.shift_left(jnp.int32(1), 30 - i)
        cnt = count(jnp.where(bits >= cand, 1.0, 0.0))
        return jnp.where(cnt >= cap, cand, prefix)

    tau = lax.fori_loop(0, 31, value_step, jnp.zeros((1, ne, 1), I32))
    gt = bits > tau
    eq = bits == tau
    need = cap - count(jnp.where(gt, 1.0, 0.0))

    def index_step(i, last):
        cand = last | lax.shift_left(jnp.int32(1), 15 - i)
        cnt = count(jnp.where(eq, jnp.where(tok < cand, 1.0, 0.0), 0.0))
        return jnp.where(cnt < need, cand, last)

    last = lax.fori_loop(0, 16, index_step, jnp.zeros((1, ne, 1), I32))
    sel_ref[...] = jnp.where(gt, 1.0, jnp.where(eq, jnp.where(tok <= last, 1.0, 0.0), 0.0))

    upper = (lax.broadcasted_iota(I32, (LANES, LANES), 0) <= lax.broadcasted_iota(I32, (LANES, LANES), 1))
    upper = jnp.where(upper, 1.0, 0.0).astype(BF16)

    def chunk_step(c, off):
        s = sel_ref[c]
        cum = jnp.dot(s.astype(BF16), upper, preferred_element_type=F32)
        pos_ref[c] = off + cum - s
        off_ref[c] = jnp.broadcast_to(off, s.shape)
        return off + cum[:, LANES - 1:LANES]

    lax.fori_loop(0, nc, chunk_step, jnp.zeros((ne, 1), F32), unroll=CHUNK_UNROLL)


def select(aff, cap):
    nc, ne, _ = aff.shape
    assert nc * LANES <= 65536
    shape = jax.ShapeDtypeStruct(aff.shape, F32)
    return pl.pallas_call(
        functools.partial(_select_kernel, cap=float(cap)),
        out_shape=[shape, shape, shape],
        compiler_params=pltpu.CompilerParams(vmem_limit_bytes=VMEM_LIMIT),
        name="select",
    )(aff)


TOK_COL, CHUNK_COL, HI_COL = 0, 1, 2


def _compact_kernel(choff_ref, sel_ref, pos_ref, aff_ref, o_ref, tv_scr):
    e = pl.program_id(0)
    nc, ne, _ = aff_ref.shape
    win = 2 * LANES
    col = lax.broadcasted_iota(I32, (LANES, LANES), 1)
    row = lax.broadcasted_iota(I32, (LANES, LANES), 0)

    @pl.when(e == 0)
    def _():
        ecol = lax.broadcasted_iota(I32, (ne, LANES), 1) - lax.broadcasted_iota(I32, (ne, LANES), 0)
        place = [jnp.where(ecol == HI_COL + k * ne, 1.0, 0.0).astype(BF16) for k in range(3)]

        def build(c, carry):
            a = aff_ref[c]
            hi = a.astype(BF16)
            r1 = a - hi.astype(F32)
            mid = r1.astype(BF16)
            lo = (r1 - mid.astype(F32)).astype(BF16)
            rec = (lax.dot_general(hi, place[0], TN_DIMS, preferred_element_type=F32)
                   + lax.dot_general(mid, place[1], TN_DIMS, preferred_element_type=F32)
                   + lax.dot_general(lo, place[2], TN_DIMS, preferred_element_type=F32))
            rec = rec + jnp.where(col == TOK_COL, row, jnp.where(col == CHUNK_COL, c, 0)).astype(F32)
            tv_scr[c] = rec.astype(BF16)
            return carry

        lax.fori_loop(0, nc, build, 0, unroll=CHUNK_UNROLL)

    o_ref[...] = jnp.zeros_like(o_ref)
    slot0 = lax.broadcasted_iota(I32, (win, LANES), 0).astype(F32)

    def body(c, carry):
        off = choff_ref[c * ne + e]
        base = pl.multiple_of((off // LANES) * LANES, LANES)
        s = sel_ref[c, pl.ds(e, 1), :]
        p = pos_ref[c, pl.ds(e, 1), :] - base.astype(F32)
        onehot = jnp.where(s > 0.0, jnp.where(slot0 == p, 1.0, 0.0), 0.0).astype(BF16)
        o_ref[0, pl.ds(base, win), :] += jnp.dot(onehot, tv_scr[c], preferred_element_type=F32)
        return carry

    lax.fori_loop(0, nc, body, 0, unroll=CHUNK_UNROLL)


def compact(sel, pos, choff, aff, cap):
    nc, ne, _ = aff.shape
    assert nc <= 256 and HI_COL + 3 * ne <= LANES and cap % LANES == 0
    rows = cap + 2 * LANES
    full = pl.BlockSpec(aff.shape, lambda e, s: (0, 0, 0))
    rec = pl.pallas_call(
        _compact_kernel,
        grid_spec=pltpu.PrefetchScalarGridSpec(
            num_scalar_prefetch=1, grid=(ne,),
            in_specs=[full, full, full],
            out_specs=pl.BlockSpec((1, rows, LANES), lambda e, s: (e, 0, 0)),
            scratch_shapes=[pltpu.VMEM((nc, LANES, LANES), BF16)]),
        out_shape=jax.ShapeDtypeStruct((ne, rows, LANES), F32),
        compiler_params=_params("arbitrary"),
        name="compact",
    )(choff, sel, pos, aff)
    idx = (rec[:, :cap, CHUNK_COL] * LANES + rec[:, :cap, TOK_COL]).astype(I32)
    return idx, rec


def _expert_kernel(idx_prev, idx_cur, idx_next, rec_ref, zin_ref, wg_ref, wu_ref, wd_ref, z_ref,
                   buf, xs, acc, gsem, ssem, *, ts, nf, s_tiles, ne):
    del zin_ref
    sub = z_ref.shape[1] // 3
    t = pl.program_id(0)
    f = pl.program_id(1)
    last_t = pl.num_programs(0) - 1
    rps = ts // nf
    slot, nslot, pslot = t % 3, (t + 1) % 3, (t + 2) % 3
    odd = (t // s_tiles) % 2
    nodd = (jnp.minimum(t + 1, last_t) // s_tiles) % 2
    podd = (jnp.maximum(t - 1, 0) // s_tiles) % 2

    def rows(ref, row, first, count):
        return ref.at[row >> 3, pl.ds(first, count), pl.ds(row & (SUBLANES - 1), 1), :]

    def gather(token, ra, rs, to_slot, is_odd):
        return pltpu.make_async_copy(rows(z_ref, token, is_odd * sub, 2 * sub),
                                     buf.at[to_slot, ra, :, pl.ds(rs, 1), :], gsem.at[to_slot])

    def scatter(token, ra, rs, from_slot, is_odd):
        return pltpu.make_async_copy(buf.at[from_slot, ra, pl.ds(is_odd * sub, sub), pl.ds(rs, 1), :],
                                     rows(z_ref, token, is_odd * 2 * sub, sub), ssem.at[from_slot])

    def wait_gather(s):
        pltpu.make_async_copy(z_ref.at[pl.ds(0, ts // SUBLANES), pl.ds(0, 2 * sub)], buf.at[s], gsem.at[s]).wait()

    def wait_scatter(s):
        pltpu.make_async_copy(buf.at[s, :, pl.ds(0, sub)], z_ref.at[pl.ds(0, ts // SUBLANES), pl.ds(0, sub)],
                              ssem.at[s]).wait()

    def for_rows(fn):
        def body(r, carry):
            fn(idx_cur[0, 0, r], r >> 3, r & (SUBLANES - 1))
            return carry
        lax.fori_loop(0, ts, body, 0)

    @pl.when(f == 0)
    def _():
        @pl.when(t == 0)
        def _():
            def start(token, ra, rs):
                gather(token, ra, rs, 0, 0).start()
                gather(token, ra, rs, 2, 0).start()
            for_rows(start)
            wait_gather(2)

        acc[...] = jnp.zeros(acc.shape, F32)
        wait_gather(slot)
        h2_first = (1 - odd) * sub
        for c in range(sub):
            xs[:, c * LANES:(c + 1) * LANES] = _from_row_tiles(buf[slot, :, h2_first + c]).astype(BF16)

    for k in range(rps):
        r = f * rps + k
        ra, rs = f * (rps // SUBLANES) + k // SUBLANES, k % SUBLANES
        gather(idx_next[0, 0, r], ra, rs, nslot, nodd).start()
        scatter(idx_prev[0, 0, r], ra, rs, pslot, podd).start()

    x = xs[...]
    a = jnp.dot(x, wg_ref[0], preferred_element_type=F32)
    b = jnp.dot(x, wu_ref[0], preferred_element_type=F32)
    hm = (jax.nn.silu(a) * b).astype(BF16)
    acc[...] += jnp.dot(hm, wd_ref[0], preferred_element_type=F32)

    @pl.when(f == nf - 1)
    def _():
        lane = lax.broadcasted_iota(I32, (1, LANES), 1) - (HI_COL + t // s_tiles)
        own = (lane == 0) | (lane == ne) | (lane == 2 * ne)
        contrib = acc[...] * jnp.sum(jnp.where(own, rec_ref[0], 0.0), axis=1, keepdims=True)
        y_first = odd * sub
        for c in range(sub):
            buf[slot, :, y_first + c] += _to_row_tiles(contrib[:, c * LANES:(c + 1) * LANES])
        wait_scatter(pslot)

        @pl.when(t == last_t)
        def _():
            wait_gather(nslot)
            for_rows(lambda token, ra, rs: scatter(token, ra, rs, slot, odd).start())
            wait_scatter(slot)


def expert_ffn(idx, rec, z, w_gate, w_up, w_down, tf=512):
    ne, cap = idx.shape
    d = z.shape[1] // 3 * LANES
    ff = w_gate.shape[2]
    ts = EXPERT_TS
    tf = min(tf, ff)
    nf = ff // tf
    s_tiles = cap // ts
    assert cap % ts == 0 and s_tiles >= 2 and ts % nf == 0
    nt = ne * s_tiles
    idx3 = idx.reshape(nt, 1, ts)
    smem_tile = lambda shift: pl.BlockSpec(
        (1, 1, ts), lambda t, f: (jnp.clip(t + shift, 0, nt - 1), 0, 0), memory_space=pltpu.SMEM)
    return pl.pallas_call(
        functools.partial(_expert_kernel, ts=ts, nf=nf, s_tiles=s_tiles, ne=ne),
        grid=(nt, nf),
        in_specs=[smem_tile(-1), smem_tile(0), smem_tile(1),
                  pl.BlockSpec((1, ts, LANES), lambda t, f: (t // s_tiles, t % s_tiles, 0)),
                  pl.BlockSpec(memory_space=pl.ANY),
                  pl.BlockSpec((1, d, tf), lambda t, f: (t // s_tiles, 0, f)),
                  pl.BlockSpec((1, d, tf), lambda t, f: (t // s_tiles, 0, f)),
                  pl.BlockSpec((1, tf, d), lambda t, f: (t // s_tiles, f, 0))],
        out_specs=pl.BlockSpec(memory_space=pl.ANY),
        out_shape=jax.ShapeDtypeStruct(z.shape, F32),
        scratch_shapes=[pltpu.VMEM((3,) + _row_tile_shape(ts, 2 * d), F32), pltpu.VMEM((ts, d), BF16),
                        pltpu.VMEM((ts, d), F32), pltpu.SemaphoreType.DMA((3,)), pltpu.SemaphoreType.DMA((3,))],
        input_output_aliases={4: 0},
        compiler_params=_params("arbitrary", "arbitrary"),
        name="expert_ffn",
    )(idx3, idx3, idx3, rec, z, w_gate, w_up, w_down)


def _final_norm_kernel(y0_ref, y1_ref, g_ref, o_ref):
    x = jnp.concatenate([_from_row_tiles(y0_ref[:, c] + y1_ref[:, c]) for c in range(y0_ref.shape[1])], axis=1)
    o_ref[...] = _rms(x, g_ref[...])


def final_norm(z, g, tm=512):
    n, d = z.shape[0] * SUBLANES, z.shape[1] // 3 * LANES
    tm = min(tm, n)
    assert n % tm == 0
    return pl.pallas_call(
        _final_norm_kernel,
        grid=(n // tm,),
        in_specs=[pl.BlockSpec(_row_tile_shape(tm, d), lambda i: (i, 0, 0, 0)),
                  pl.BlockSpec(_row_tile_shape(tm, d), lambda i: (i, 2, 0, 0)),
                  pl.BlockSpec((1, d), lambda i: (0, 0))],
        out_specs=pl.BlockSpec((tm, d), lambda i: (i, 0)),
        out_shape=jax.ShapeDtypeStruct((n, d), F32),
        compiler_params=_params("parallel"),
        name="final_norm",
    )(z, z, g)


def moe_and_norm(z, aff, w_e_gate, w_e_up, w_e_down, g_final):
    n = z.shape[0] * SUBLANES
    cap = EC_CAPACITY * n // N_EXPERTS
    sel, pos, off = select(aff, cap)
    choff = off[:, :, 0].astype(I32).reshape(-1)
    idx, rec = compact(sel, pos, choff, aff, cap)
    z = expert_ffn(idx, rec, z, w_e_gate, w_e_up, w_e_down)
    return final_norm(z, g_final)


def encoder_group(x, mem, w, experts, g_final):
    b, t, d = x.shape
    z, aff = layer_front(x, mem, w)
    return moe_and_norm(z, aff, *experts, g_final).reshape(b, t, d)


def kernel(x_prompt, x_sample, mem_prompt, mem_sample, g_mix, g_mem, w_in, w_mem_kv, na_rpb, t5_table, sw_sink,
           w_na_o, w_sw_o, w_mx_o, w_out, g_ffn, w_router, w_e_gate, w_e_up, w_e_down, g_final):
    assert g_mix.shape[0] == 1, "single-layer trunk"
    w = prep_weights(g_mix[0], g_mem[0], w_in[0], w_mem_kv[0], na_rpb[0], t5_table, sw_sink[0], w_na_o[0],
                     w_sw_o[0], w_mx_o[0], w_out[0], g_ffn[0], w_router[0])
    experts = (w_e_gate[0].astype(BF16), w_e_up[0].astype(BF16), w_e_down[0].astype(BF16))
    gf = g_final.reshape(1, -1).astype(F32)
    y_prompt = encoder_group(x_prompt, mem_prompt, w, experts, gf)
    y_sample = encoder_group(x_sample, mem_sample, w, experts, gf)
    return (y_prompt, y_sample)
```

```python
import functools
import math

import numpy as np
import jax
import jax.numpy as jnp
from jax import lax
from jax.experimental import pallas as pl
from jax.experimental.pallas import tpu as pltpu

F32 = jnp.float32
BF16 = jnp.bfloat16
I32 = jnp.int32

RMS_EPS = 1e-6
NEG_INF = -1e30

GRID_W = 64
NA_HEADS = 8
NA_HEAD_DIM = 64
NA_KR = 8
NA_KC = 16
SW_HEADS = 16
SW_KV_HEADS = 4
SW_HEAD_DIM = 64
SW_WINDOW = 128
SW_BLOCK = 128
MX_HEADS = 4
T5_BUCKETS = 32
T5_MAX_DIST = 128
N_BRANCHES = 3
N_EXPERTS = 16
EC_CAPACITY = 2

LANES = 128
SUBLANES = 8
V7X_VMEM_BYTES = 64 * 1024 * 1024
VMEM_LIMIT = V7X_VMEM_BYTES * 7 // 8

CHUNK_UNROLL = 8
EXPERT_TS = 512
MERGE_TN = 256

NT_DIMS = (((1,), (1,)), ((), ()))
TN_DIMS = (((0,), (0,)), ((), ()))


def _params(*sem):
    return pltpu.CompilerParams(dimension_semantics=sem, vmem_limit_bytes=VMEM_LIMIT)


def _rms(x, g):
    return x * lax.rsqrt(jnp.mean(x * x, axis=-1, keepdims=True) + RMS_EPS) * g


def _to_row_tiles(strip):
    return strip.reshape(strip.shape[0] // SUBLANES, SUBLANES, LANES)


def _from_row_tiles(tiles):
    return tiles.reshape(tiles.shape[0] * SUBLANES, LANES)


def _row_tile_shape(rows, d):
    return (rows // SUBLANES, d // LANES, SUBLANES, LANES)


def _norm_proj_kernel(x_ref, g_ref, w_ref, *o_refs, scales):
    h = _rms(x_ref[...], g_ref[...]).astype(BF16)
    off = 0
    for o_ref, sc in zip(o_refs, scales):
        width = o_ref.shape[1]
        for c0 in range(0, width, 512):
            cw = min(512, width - c0)
            r = jnp.dot(h, w_ref[:, off + c0:off + c0 + cw], preferred_element_type=F32)
            if sc != 1.0:
                r = r * sc
            o_ref[:, c0:c0 + cw] = r.astype(o_ref.dtype)
        off += width


def norm_proj(x, g, w, widths, scales, tm):
    n, d = x.shape
    tm = min(tm, n)
    assert n % tm == 0
    return pl.pallas_call(
        functools.partial(_norm_proj_kernel, scales=tuple(scales)),
        grid=(n // tm,),
        in_specs=[pl.BlockSpec((tm, d), lambda i: (i, 0)),
                  pl.BlockSpec((1, d), lambda i: (0, 0)),
                  pl.BlockSpec(w.shape, lambda i: (0, 0))],
        out_specs=[pl.BlockSpec((tm, c), lambda i: (i, 0)) for c in widths],
        out_shape=[jax.ShapeDtypeStruct((n, c), BF16) for c in widths],
        compiler_params=_params("parallel"),
        name="norm_proj",
    )(x, g, w)


def _softmax_pv(s, v):
    m = jnp.max(s, axis=-1, keepdims=True)
    e = jnp.exp(s - m)
    den = jnp.sum(e, axis=-1, keepdims=True)
    o = jnp.dot(e.astype(BF16), v, preferred_element_type=F32)
    return o / den


def _softmax_pv_half(s, v, extra_logit, use_lo):
    rows, keys = s.shape
    mb = jnp.maximum(jnp.broadcast_to(jnp.max(s, axis=-1, keepdims=True), (rows, LANES)), extra_logit)
    e = jnp.concatenate([jnp.exp(s[:, t * LANES:(t + 1) * LANES] - mb) for t in range(keys // LANES)],
                        axis=1).astype(BF16)
    lo = lax.broadcasted_iota(I32, v.shape, 1) < LANES // 2
    keep = lo if use_lo else jnp.logical_not(lo)
    o = jnp.dot(e, jnp.where(keep, v, jnp.ones_like(v)), preferred_element_type=F32)
    den = pltpu.roll(o + jnp.exp(extra_logit - mb), LANES // 2, 1)
    return o / den


def _na_kernel(q_ref, k_ref, v_ref, bias_ref, o_ref, *, rows, rb):
    j = pl.program_id(1)
    lo = lax.broadcasted_iota(I32, (GRID_W, LANES), 1) < NA_HEAD_DIM
    nkeys = NA_KR * GRID_W

    def body(i, carry):
        r = j * rb + i
        rs = jnp.clip(r - NA_KR // 2, 0, rows - NA_KR)
        off = r - rs
        q = q_ref[0, pl.ds(pl.multiple_of(i * GRID_W, GRID_W), GRID_W), :]
        kk = k_ref[0, pl.ds(pl.multiple_of(rs * GRID_W, GRID_W), nkeys), :]
        vv = v_ref[0, pl.ds(pl.multiple_of(rs * GRID_W, GRID_W), nkeys), :]
        scores = []
        for p in range(NA_HEADS // 2):
            qp = q[:, p * LANES:(p + 1) * LANES]
            kp = kk[:, p * LANES:(p + 1) * LANES]
            zero = jnp.zeros_like(qp)
            q2 = jnp.concatenate([jnp.where(lo, qp, zero), jnp.where(lo, zero, qp)], axis=0)
            s = lax.dot_general(q2, kp, NT_DIMS, preferred_element_type=F32)
            scores.append(s + bias_ref[off, 2 * p:2 * p + 2].reshape(2 * GRID_W, nkeys))
        outs = []
        for p in range(NA_HEADS // 2):
            o2 = _softmax_pv(scores[p], vv[:, p * LANES:(p + 1) * LANES])
            outs.append(jnp.where(lo, o2[:GRID_W], o2[GRID_W:]))
        o_ref[0, pl.ds(pl.multiple_of(i * GRID_W, GRID_W), GRID_W), :] = (
            jnp.concatenate(outs, axis=1).astype(o_ref.dtype))
        return carry

    lax.fori_loop(0, rb, body, 0, unroll=2)


def na_attention(q, k, v, bias, rb=8):
    b, t, c = q.shape
    rows = t // GRID_W
    assert rows >= NA_KR and rows % rb == 0
    return pl.pallas_call(
        functools.partial(_na_kernel, rows=rows, rb=rb),
        grid=(b, rows // rb),
        in_specs=[pl.BlockSpec((1, rb * GRID_W, c), lambda i, j: (i, j, 0)),
                  pl.BlockSpec((1, t, c), lambda i, j: (i, 0, 0)),
                  pl.BlockSpec((1, t, c), lambda i, j: (i, 0, 0)),
                  pl.BlockSpec(bias.shape, lambda i, j: (0, 0, 0, 0))],
        out_specs=pl.BlockSpec((1, rb * GRID_W, c), lambda i, j: (i, j, 0)),
        out_shape=jax.ShapeDtypeStruct((b, t, c), BF16),
        compiler_params=_params("parallel", "arbitrary"),
        name="na_attention",
    )(q, k, v, bias)


def na_bias_table(rpb):
    col = np.arange(GRID_W)
    col_start = np.clip(col - NA_KC // 2, 0, GRID_W - NA_KC)
    in_win = (col[None, :] >= col_start[:, None]) & (col[None, :] < col_start[:, None] + NA_KC)
    dc = np.clip(col[None, :] - col[:, None] + NA_KC - 1, 0, 2 * NA_KC - 2)
    heads = rpb.shape[0]
    by_col = jnp.take(rpb.astype(F32), jnp.asarray(dc.reshape(-1)), axis=2)
    by_col = by_col.reshape(heads, 2 * NA_KR - 1, GRID_W, GRID_W)
    by_col = jnp.where(jnp.asarray(in_win)[None, None], by_col, NEG_INF)
    per_off = [by_col[:, NA_KR - 1 - off:2 * NA_KR - 1 - off].transpose(0, 2, 1, 3) for off in range(NA_KR)]
    return jnp.stack(per_off).reshape(NA_KR, heads, GRID_W, NA_KR * GRID_W)


def _sw_kernel(sink_ref, q_ref, k_ref, v_ref, bias_ref, o_ref, s_scr, *, nb):
    n = pl.program_id(1)
    blk = SW_BLOCK
    group = SW_HEADS // SW_KV_HEADS

    def rows_of(ref, c):
        return ref[0, pl.ds(pl.multiple_of(c * blk, blk), blk), :]

    cl = jnp.maximum(n - 1, 0)
    cr = jnp.minimum(n + 1, nb - 1)
    k3 = jnp.concatenate([rows_of(k_ref, cl), rows_of(k_ref, n), rows_of(k_ref, cr)], axis=0)
    v3 = jnp.concatenate([rows_of(v_ref, cl), rows_of(v_ref, n), rows_of(v_ref, cr)], axis=0)
    pen_l = jnp.where(n > 0, 0.0, NEG_INF).astype(F32)
    pen_r = jnp.where(n < nb - 1, 0.0, NEG_INF).astype(F32)
    key = lax.broadcasted_iota(I32, (1, 3 * blk), 1)
    pen = jnp.where(key < blk, pen_l, jnp.where(key >= 2 * blk, pen_r, 0.0))
    lo = lax.broadcasted_iota(I32, (blk, LANES), 1) < SW_HEAD_DIM

    for c in range(SW_KV_HEADS):
        pair, half = divmod(c, 2)
        kp = k3[:, pair * LANES:(pair + 1) * LANES]
        keep = lo if half == 0 else jnp.logical_not(lo)
        qs = []
        for g in range(group):
            t = pair * group + g
            qt = q_ref[0, :, t * LANES:(t + 1) * LANES]
            qs.append(jnp.where(keep, qt, jnp.zeros_like(qt)))
        qq = jnp.concatenate(qs, axis=0)
        s = lax.dot_general(qq, kp, NT_DIMS, preferred_element_type=F32)
        s_scr[c] = s + bias_ref[c * group:(c + 1) * group].reshape(group * blk, 3 * blk) + pen

    for pair in range(SW_KV_HEADS // 2):
        vp = v3[:, pair * LANES:(pair + 1) * LANES]
        per_half = []
        for half in range(2):
            c = 2 * pair + half
            sink = jnp.concatenate(
                [jnp.full((blk, LANES), sink_ref[c * group + g], F32) for g in range(group)], axis=0)
            per_half.append(_softmax_pv_half(s_scr[c], vp, sink, use_lo=(half == 0)))
        for g in range(group):
            t = pair * group + g
            o = jnp.where(lo, per_half[0][g * blk:(g + 1) * blk], per_half[1][g * blk:(g + 1) * blk])
            o_ref[0, :, t * LANES:(t + 1) * LANES] = o.astype(o_ref.dtype)


def sw_attention(q, k, v, bias, sink):
    b, t, c = q.shape
    nb = t // SW_BLOCK
    kvw = k.shape[2]
    grid_spec = pltpu.PrefetchScalarGridSpec(
        num_scalar_prefetch=1,
        grid=(b, nb),
        in_specs=[pl.BlockSpec((1, SW_BLOCK, c), lambda i, j, s: (i, j, 0)),
                  pl.BlockSpec((1, t, kvw), lambda i, j, s: (i, 0, 0)),
                  pl.BlockSpec((1, t, kvw), lambda i, j, s: (i, 0, 0)),
                  pl.BlockSpec(bias.shape, lambda i, j, s: (0, 0, 0))],
        out_specs=pl.BlockSpec((1, SW_BLOCK, c), lambda i, j, s: (i, j, 0)),
        scratch_shapes=[pltpu.VMEM((SW_KV_HEADS, (SW_HEADS // SW_KV_HEADS) * SW_BLOCK, 3 * SW_BLOCK), F32)],
    )
    return pl.pallas_call(
        functools.partial(_sw_kernel, nb=nb),
        grid_spec=grid_spec,
        out_shape=jax.ShapeDtypeStruct((b, t, c), BF16),
        compiler_params=_params("parallel", "arbitrary"),
        name="sw_attention",
    )(sink, q, k, v, bias)


def _t5_bucket(rel):
    half = T5_BUCKETS // 2
    max_exact = half // 2
    n = jnp.abs(rel)
    nf = jnp.maximum(n, 1).astype(F32)
    large = max_exact + (jnp.log(nf / max_exact) / math.log(T5_MAX_DIST / max_exact)
                         * (half - max_exact)).astype(jnp.int32)
    large = jnp.minimum(large, half - 1)
    return jnp.where(rel > 0, half, 0) + jnp.where(n < max_exact, n, large)


def sw_bias_table(t5_table):
    span = SW_BLOCK + 2 * SW_WINDOW
    rel = np.arange(span)[None, :] - SW_WINDOW - np.arange(SW_BLOCK)[:, None]
    bucket = _t5_bucket(jnp.asarray(rel, dtype=jnp.int32))[None]
    table = t5_table.astype(F32).T[:, :, None, None]
    bias = jnp.zeros((t5_table.shape[1], SW_BLOCK, span), F32)
    for b in range(T5_BUCKETS):
        bias = jnp.where(bucket == b, table[:, b], bias)
    return jnp.where(jnp.asarray(np.abs(rel) <= SW_WINDOW)[None], bias, NEG_INF)


def sw_pair_heads(w, axis):
    group = SW_HEADS // SW_KV_HEADS
    shape = w.shape
    split = shape[:axis] + (SW_KV_HEADS // 2, 2, group, SW_HEAD_DIM) + shape[axis + 1:]
    order = list(range(len(split)))
    order[axis + 1], order[axis + 2] = axis + 2, axis + 1
    return w.reshape(split).transpose(order).reshape(shape)


def _mx_kernel(q_ref, mk_ref, mv_ref, o_ref, *, scale):
    hd = q_ref.shape[2] // MX_HEADS
    heads = [slice(h * hd, (h + 1) * hd) for h in range(MX_HEADS)]
    scores = [lax.dot_general(q_ref[0, :, sl], mk_ref[0, :, sl], NT_DIMS, preferred_element_type=F32) * scale
              for sl in heads]
    for sl, s in zip(heads, scores):
        o_ref[0, :, sl] = _softmax_pv(s, mv_ref[0, :, sl]).astype(o_ref.dtype)


def mx_attention(q, mkv, tq=512):
    b, t, c = q.shape
    m = mkv.shape[1]
    tq = min(tq, t)
    return pl.pallas_call(
        functools.partial(_mx_kernel, scale=float((c // MX_HEADS) ** -0.5)),
        grid=(b, t // tq),
        in_specs=[pl.BlockSpec((1, tq, c), lambda i, j: (i, j, 0)),
                  pl.BlockSpec((1, m, c), lambda i, j: (i, 0, 0)),
                  pl.BlockSpec((1, m, c), lambda i, j: (i, 0, 1))],
        out_specs=pl.BlockSpec((1, tq, c), lambda i, j: (i, j, 0)),
        out_shape=jax.ShapeDtypeStruct((b, t, c), BF16),
        compiler_params=_params("parallel", "arbitrary"),
        name="mx_attention",
    )(q, mkv, mkv)


def _merge_kernel(x_ref, ona_ref, osw_ref, omx_ref, gmix_ref, wg_ref, wna_ref, wsw_ref, wmx_ref, wout_ref,
                  gffn_ref, wrh_ref, wrl_ref, z_ref, aff_ref, h_scr):
    j = pl.program_id(1)
    nstrip = z_ref.shape[1] // 3
    strips = range(nstrip)

    @pl.when(j == 0)
    def _():
        x = x_ref[...]
        h_scr[...] = _rms(x, gmix_ref[...]).astype(BF16)
        for c in strips:
            z_ref[:, c] = _to_row_tiles(x[:, c * LANES:(c + 1) * LANES])

    gates = jax.nn.sigmoid(jnp.dot(h_scr[...], wg_ref[...], preferred_element_type=F32))
    tn = wna_ref.shape[1]
    merged = None
    for o_ref, w_ref, b in ((ona_ref, wna_ref, 0), (osw_ref, wsw_ref, 1), (omx_ref, wmx_ref, 2)):
        term = gates[:, b * tn:(b + 1) * tn] * jnp.dot(o_ref[...], w_ref[...], preferred_element_type=F32)
        merged = term if merged is None else merged + term
    part = jnp.dot(merged.astype(BF16), wout_ref[...], preferred_element_type=F32)
    for c in strips:
        z_ref[:, c] += _to_row_tiles(part[:, c * LANES:(c + 1) * LANES])

    @pl.when(j == pl.num_programs(1) - 1)
    def _():
        x2 = jnp.concatenate([_from_row_tiles(z_ref[:, c]) for c in strips], axis=1)
        h2 = _rms(x2, gffn_ref[...])
        for c in strips:
            z_ref[:, nstrip + c] = _to_row_tiles(h2[:, c * LANES:(c + 1) * LANES])
            z_ref[:, 2 * nstrip + c] = jnp.zeros((z_ref.shape[0], SUBLANES, LANES), F32)
        hi = h2.astype(BF16)
        lo = (h2 - hi.astype(F32)).astype(BF16)
        wh = wrh_ref[...]
        logits = (lax.dot_general(wh, hi, NT_DIMS, preferred_element_type=F32)
                  + lax.dot_general(wh, lo, NT_DIMS, preferred_element_type=F32)
                  + lax.dot_general(wrl_ref[...], hi, NT_DIMS, preferred_element_type=F32))
        m = jnp.max(logits, axis=0, keepdims=True)
        e = jnp.exp(logits - m)
        aff = e / jnp.sum(e, axis=0, keepdims=True)
        for c in range(aff_ref.shape[0]):
            aff_ref[c] = aff[:, c * LANES:(c + 1) * LANES]


def merge(x, o_na, o_sw, o_mx, g_mix, w_gate, w_na_o, w_sw_o, w_mx_o, w_out, g_ffn, wr_hi, wr_lo,
          tm=512, tn=MERGE_TN):
    n, d = x.shape
    tm = min(tm, n)
    assert n % tm == 0 and tm % LANES == 0 and d % LANES == 0
    ne = wr_hi.shape[0]
    row = lambda i, j: (i, 0)
    return pl.pallas_call(
        _merge_kernel,
        grid=(n // tm, d // tn),
        in_specs=[pl.BlockSpec((tm, d), row),
                  pl.BlockSpec((tm, o_na.shape[1]), row),
                  pl.BlockSpec((tm, o_sw.shape[1]), row),
                  pl.BlockSpec((tm, o_mx.shape[1]), row),
                  pl.BlockSpec((1, d), lambda i, j: (0, 0)),
                  pl.BlockSpec((d, N_BRANCHES * tn), lambda i, j: (0, j)),
                  pl.BlockSpec((w_na_o.shape[0], tn), lambda i, j: (0, j)),
                  pl.BlockSpec((w_sw_o.shape[0], tn), lambda i, j: (0, j)),
                  pl.BlockSpec((w_mx_o.shape[0], tn), lambda i, j: (0, j)),
                  pl.BlockSpec((tn, d), lambda i, j: (j, 0)),
                  pl.BlockSpec((1, d), lambda i, j: (0, 0)),
                  pl.BlockSpec((ne, d), lambda i, j: (0, 0)),
                  pl.BlockSpec((ne, d), lambda i, j: (0, 0))],
        out_specs=[pl.BlockSpec(_row_tile_shape(tm, 3 * d), lambda i, j: (i, 0, 0, 0)),
                   pl.BlockSpec((tm // LANES, ne, LANES), lambda i, j: (i, 0, 0))],
        out_shape=[jax.ShapeDtypeStruct(_row_tile_shape(n, 3 * d), F32),
                   jax.ShapeDtypeStruct((n // LANES, ne, LANES), F32)],
        scratch_shapes=[pltpu.VMEM((tm, d), BF16)],
        compiler_params=_params("parallel", "arbitrary"),
        name="merge",
    )(x, o_na, o_sw, o_mx, g_mix, w_gate, w_na_o, w_sw_o, w_mx_o, w_out, g_ffn, wr_hi, wr_lo)


NA_WIDTH = NA_HEADS * NA_HEAD_DIM
SW_WIDTH = SW_HEADS * SW_HEAD_DIM
SW_KV_WIDTH = SW_KV_HEADS * SW_HEAD_DIM
QKV_WIDTHS = (NA_WIDTH, NA_WIDTH, NA_WIDTH, SW_WIDTH, SW_KV_WIDTH, SW_KV_WIDTH)


def prep_weights(g_mix, g_mem, w_in, w_mem_kv, na_rpb, t5_table, sw_sink, w_na_o, w_sw_o, w_mx_o, w_out,
                 g_ffn, w_router):
    d = w_in.shape[0]
    mx_width = w_mx_o.shape[0]
    widths = QKV_WIDTHS + (mx_width,)
    offs = np.concatenate([[0], np.cumsum(widths)])
    cols = [w_in[:, offs[i]:offs[i + 1]] for i in range(len(widths))]
    cols[3] = sw_pair_heads(cols[3], axis=1)
    w_gate = (w_in[:, offs[-1]:].reshape(d, N_BRANCHES, d // MERGE_TN, MERGE_TN).transpose(0, 2, 1, 3)
              .reshape(d, N_BRANCHES * d))
    wr = w_router.T.astype(F32)
    wr_hi = wr.astype(BF16)
    return dict(
        g_mix=g_mix.reshape(1, d).astype(F32), g_mem=g_mem.reshape(1, d).astype(F32),
        g_ffn=g_ffn.reshape(1, d).astype(F32),
        w_qkv=jnp.concatenate(cols, axis=1).astype(BF16), qkv_widths=widths,
        qkv_scales=(NA_HEAD_DIM ** -0.5, 1.0, 1.0, SW_HEAD_DIM ** -0.5, 1.0, 1.0, 1.0),
        w_mem_kv=w_mem_kv.astype(BF16),
        na_bias=na_bias_table(na_rpb), sw_bias=sw_bias_table(t5_table), sw_sink=sw_sink.astype(F32),
        w_gate=w_gate.astype(BF16), w_na_o=w_na_o.astype(BF16), w_sw_o=sw_pair_heads(w_sw_o, axis=0).astype(BF16),
        w_mx_o=w_mx_o.astype(BF16), w_out=w_out.astype(BF16),
        wr_hi=wr_hi, wr_lo=(wr - wr_hi.astype(F32)).astype(BF16))


def layer_front(x, mem, w):
    b, t, d = x.shape
    m = mem.shape[1]
    xf = x.reshape(b * t, d)
    na_q, na_k, na_v, sw_q, sw_k, sw_v, mx_q = norm_proj(xf, w["g_mix"], w["w_qkv"], w["qkv_widths"],
                                                         w["qkv_scales"], tm=512)
    (mkv,) = norm_proj(mem.reshape(b * m, d), w["g_mem"], w["w_mem_kv"], (w["w_mem_kv"].shape[1],), (1.0,),
                       tm=512)
    r3 = lambda a: a.reshape(b, t, a.shape[1])
    o_na = na_attention(r3(na_q), r3(na_k), r3(na_v), w["na_bias"])
    o_sw = sw_attention(r3(sw_q), r3(sw_k), r3(sw_v), w["sw_bias"], w["sw_sink"])
    o_mx = mx_attention(r3(mx_q), mkv.reshape(b, m, mkv.shape[1]))
    f2 = lambda a: a.reshape(b * t, a.shape[2])
    return merge(xf, f2(o_na), f2(o_sw), f2(o_mx), w["g_mix"], w["w_gate"], w["w_na_o"], w["w_sw_o"],
                 w["w_mx_o"], w["w_out"], w["g_ffn"], w["wr_hi"], w["wr_lo"])


def _select_kernel(aff_ref, sel_ref, pos_ref, off_ref, *, cap):
    nc, ne, _ = aff_ref.shape
    bits = lax.bitcast_convert_type(aff_ref[...], I32)
    tok = (lax.broadcasted_iota(I32, bits.shape, 0) * LANES + lax.broadcasted_iota(I32, bits.shape, 2))

    def count(flags):
        return jnp.sum(flags, axis=(0, 2), keepdims=True)

    def value_step(i, prefix):
        cand = prefix | lax.shift_left(jnp.int32(1), 30 - i)
        cnt = count(jnp.where(bits >= cand, 1.0, 0.0))
        return jnp.where(cnt >= cap, cand, prefix)

    tau = lax.fori_loop(0, 31, value_step, jnp.zeros((1, ne, 1), I32))
    gt = bits > tau
    eq = bits == tau
    need = cap - count(jnp.where(gt, 1.0, 0.0))

    def index_step(i, last):
        cand = last | lax.shift_left(jnp.int32(1), 15 - i)
        cnt = count(jnp.where(eq, jnp.where(tok < cand, 1.0, 0.0), 0.0))
        return jnp.where(cnt < need, cand, last)

    last = lax.fori_loop(0, 16, index_step, jnp.zeros((1, ne, 1), I32))
    sel_ref[...] = jnp.where(gt, 1.0, jnp.where(eq, jnp.where(tok <= last, 1.0, 0.0), 0.0))

    upper = (lax.broadcasted_iota(I32, (LANES, LANES), 0) <= lax.broadcasted_iota(I32, (LANES, LANES), 1))
    upper = jnp.where(upper, 1.0, 0.0).astype(BF16)

    def chunk_step(c, off):
        s = sel_ref[c]
        cum = jnp.dot(s.astype(BF16), upper, preferred_element_type=F32)
        pos_ref[c] = off + cum - s
        off_ref[c] = jnp.broadcast_to(off, s.shape)
        return off + cum[:, LANES - 1:LANES]

    lax.fori_loop(0, nc, chunk_step, jnp.zeros((ne, 1), F32), unroll=CHUNK_UNROLL)


def select(aff, cap):
    nc, ne, _ = aff.shape
    assert nc * LANES <= 65536
    shape = jax.ShapeDtypeStruct(aff.shape, F32)
    return pl.pallas_call(
        functools.partial(_select_kernel, cap=float(cap)),
        out_shape=[shape, shape, shape],
        compiler_params=pltpu.CompilerParams(vmem_limit_bytes=VMEM_LIMIT),
        name="select",
    )(aff)


TOK_COL, CHUNK_COL, HI_COL = 0, 1, 2


def _compact_kernel(choff_ref, sel_ref, pos_ref, aff_ref, o_ref, tv_scr):
    e = pl.program_id(0)
    nc, ne, _ = aff_ref.shape
    win = 2 * LANES
    col = lax.broadcasted_iota(I32, (LANES, LANES), 1)
    row = lax.broadcasted_iota(I32, (LANES, LANES), 0)

    @pl.when(e == 0)
    def _():
        ecol = lax.broadcasted_iota(I32, (ne, LANES), 1) - lax.broadcasted_iota(I32, (ne, LANES), 0)
        place = [jnp.where(ecol == HI_COL + k * ne, 1.0, 0.0).astype(BF16) for k in range(3)]

        def build(c, carry):
            a = aff_ref[c]
            hi = a.astype(BF16)
            r1 = a - hi.astype(F32)
            mid = r1.astype(BF16)
            lo = (r1 - mid.astype(F32)).astype(BF16)
            rec = (lax.dot_general(hi, place[0], TN_DIMS, preferred_element_type=F32)
                   + lax.dot_general(mid, place[1], TN_DIMS, preferred_element_type=F32)
                   + lax.dot_general(lo, place[2], TN_DIMS, preferred_element_type=F32))
            rec = rec + jnp.where(col == TOK_COL, row, jnp.where(col == CHUNK_COL, c, 0)).astype(F32)
            tv_scr[c] = rec.astype(BF16)
            return carry

        lax.fori_loop(0, nc, build, 0, unroll=CHUNK_UNROLL)

    o_ref[...] = jnp.zeros_like(o_ref)
    slot0 = lax.broadcasted_iota(I32, (win, LANES), 0).astype(F32)

    def body(c, carry):
        off = choff_ref[c * ne + e]
        base = pl.multiple_of((off // LANES) * LANES, LANES)
        s = sel_ref[c, pl.ds(e, 1), :]
        p = pos_ref[c, pl.ds(e, 1), :] - base.astype(F32)
        onehot = jnp.where(s > 0.0, jnp.where(slot0 == p, 1.0, 0.0), 0.0).astype(BF16)
        o_ref[0, pl.ds(base, win), :] += jnp.dot(onehot, tv_scr[c], preferred_element_type=F32)
        return carry

    lax.fori_loop(0, nc, body, 0, unroll=CHUNK_UNROLL)


def compact(sel, pos, choff, aff, cap):
    nc, ne, _ = aff.shape
    assert nc <= 256 and HI_COL + 3 * ne <= LANES and cap % LANES == 0
    rows = cap + 2 * LANES
    full = pl.BlockSpec(aff.shape, lambda e, s: (0, 0, 0))
    rec = pl.pallas_call(
        _compact_kernel,
        grid_spec=pltpu.PrefetchScalarGridSpec(
            num_scalar_prefetch=1, grid=(ne,),
            in_specs=[full, full, full],
            out_specs=pl.BlockSpec((1, rows, LANES), lambda e, s: (e, 0, 0)),
            scratch_shapes=[pltpu.VMEM((nc, LANES, LANES), BF16)]),
        out_shape=jax.ShapeDtypeStruct((ne, rows, LANES), F32),
        compiler_params=_params("arbitrary"),
        name="compact",
    )(choff, sel, pos, aff)
    idx = (rec[:, :cap, CHUNK_COL] * LANES + rec[:, :cap, TOK_COL]).astype(I32)
    return idx, rec


def _expert_kernel(idx_prev, idx_cur, idx_next, rec_ref, zin_ref, wg_ref, wu_ref, wd_ref, z_ref,
                   buf, xs, acc, gsem, ssem, *, ts, nf, s_tiles, ne):
    del zin_ref
    sub = z_ref.shape[1] // 3
    t = pl.program_id(0)
    f = pl.program_id(1)
    last_t = pl.num_programs(0) - 1
    rps = ts // nf
    slot, nslot, pslot = t % 3, (t + 1) % 3, (t + 2) % 3
    odd = (t // s_tiles) % 2
    nodd = (jnp.minimum(t + 1, last_t) // s_tiles) % 2
    podd = (jnp.maximum(t - 1, 0) // s_tiles) % 2

    def rows(ref, row, first, count):
        return ref.at[row >> 3, pl.ds(first, count), pl.ds(row & (SUBLANES - 1), 1), :]

    def gather(token, ra, rs, to_slot, is_odd):
        return pltpu.make_async_copy(rows(z_ref, token, is_odd * sub, 2 * sub),
                                     buf.at[to_slot, ra, :, pl.ds(rs, 1), :], gsem.at[to_slot])

    def scatter(token, ra, rs, from_slot, is_odd):
        return pltpu.make_async_copy(buf.at[from_slot, ra, pl.ds(is_odd * sub, sub), pl.ds(rs, 1), :],
                                     rows(z_ref, token, is_odd * 2 * sub, sub), ssem.at[from_slot])

    def wait_gather(s):
        pltpu.make_async_copy(z_ref.at[pl.ds(0, ts // SUBLANES), pl.ds(0, 2 * sub)], buf.at[s], gsem.at[s]).wait()

    def wait_scatter(s):
        pltpu.make_async_copy(buf.at[s, :, pl.ds(0, sub)], z_ref.at[pl.ds(0, ts // SUBLANES), pl.ds(0, sub)],
                              ssem.at[s]).wait()

    def for_rows(fn):
        def body(r, carry):
            fn(idx_cur[0, 0, r], r >> 3, r & (SUBLANES - 1))
            return carry
        lax.fori_loop(0, ts, body, 0)

    @pl.when(f == 0)
    def _():
        @pl.when(t == 0)
        def _():
            def start(token, ra, rs):
                gather(token, ra, rs, 0, 0).start()
                gather(token, ra, rs, 2, 0).start()
            for_rows(start)
            wait_gather(2)

        acc[...] = jnp.zeros(acc.shape, F32)
        wait_gather(slot)
        h2_first = (1 - odd) * sub
        for c in range(sub):
            xs[:, c * LANES:(c + 1) * LANES] = _from_row_tiles(buf[slot, :, h2_first + c]).astype(BF16)

    for k in range(rps):
        r = f * rps + k
        ra, rs = f * (rps // SUBLANES) + k // SUBLANES, k % SUBLANES
        gather(idx_next[0, 0, r], ra, rs, nslot, nodd).start()
        scatter(idx_prev[0, 0, r], ra, rs, pslot, podd).start()

    x = xs[...]
    a = jnp.dot(x, wg_ref[0], preferred_element_type=F32)
    b = jnp.dot(x, wu_ref[0], preferred_element_type=F32)
    hm = (jax.nn.silu(a) * b).astype(BF16)
    acc[...] += jnp.dot(hm, wd_ref[0].astype(BF16), preferred_element_type=F32)

    @pl.when(f == nf - 1)
    def _():
        lane = lax.broadcasted_iota(I32, (1, LANES), 1) - (HI_COL + t // s_tiles)
        own = (lane == 0) | (lane == ne) | (lane == 2 * ne)
        contrib = acc[...] * jnp.sum(jnp.where(own, rec_ref[0], 0.0), axis=1, keepdims=True)
        y_first = odd * sub
        for c in range(sub):
            buf[slot, :, y_first + c] += _to_row_tiles(contrib[:, c * LANES:(c + 1) * LANES])
        wait_scatter(pslot)

        @pl.when(t == last_t)
        def _():
            wait_gather(nslot)
            for_rows(lambda token, ra, rs: scatter(token, ra, rs, slot, odd).start())
            wait_scatter(slot)


def expert_ffn(idx, rec, z, w_gate, w_up, w_down, tf=512):
    ne, cap = idx.shape
    d = z.shape[1] // 3 * LANES
    ff = w_gate.shape[2]
    ts = EXPERT_TS
    tf = min(tf, ff)
    nf = ff // tf
    s_tiles = cap // ts
    assert cap % ts == 0 and s_tiles >= 2 and ts % nf == 0
    nt = ne * s_tiles
    idx3 = idx.reshape(nt, 1, ts)
    smem_tile = lambda shift: pl.BlockSpec(
        (1, 1, ts), lambda t, f: (jnp.clip(t + shift, 0, nt - 1), 0, 0), memory_space=pltpu.SMEM)
    return pl.pallas_call(
        functools.partial(_expert_kernel, ts=ts, nf=nf, s_tiles=s_tiles, ne=ne),
        grid=(nt, nf),
        in_specs=[smem_tile(-1), smem_tile(0), smem_tile(1),
                  pl.BlockSpec((1, ts, LANES), lambda t, f: (t // s_tiles, t % s_tiles, 0)),
                  pl.BlockSpec(memory_space=pl.ANY),
                  pl.BlockSpec((1, d, tf), lambda t, f: (t // s_tiles, 0, f)),
                  pl.BlockSpec((1, d, tf), lambda t, f: (t // s_tiles, 0, f)),
                  pl.BlockSpec((1, tf, d), lambda t, f: (t // s_tiles, f, 0))],
        out_specs=pl.BlockSpec(memory_space=pl.ANY),
        out_shape=jax.ShapeDtypeStruct(z.shape, F32),
        scratch_shapes=[pltpu.VMEM((3,) + _row_tile_shape(ts, 2 * d), F32), pltpu.VMEM((ts, d), BF16),
                        pltpu.VMEM((ts, d), F32), pltpu.SemaphoreType.DMA((3,)), pltpu.SemaphoreType.DMA((3,))],
        input_output_aliases={4: 0},
        compiler_params=_params("arbitrary", "arbitrary"),
        name="expert_ffn",
    )(idx3, idx3, idx3, rec, z, w_gate, w_up, w_down)


def _final_norm_kernel(y0_ref, y1_ref, g_ref, o_ref):
    x = jnp.concatenate([_from_row_tiles(y0_ref[:, c] + y1_ref[:, c]) for c in range(y0_ref.shape[1])], axis=1)
    o_ref[...] = _rms(x, g_ref[...])


def final_norm(z, g, tm=512):
    n, d = z.shape[0] * SUBLANES, z.shape[1] // 3 * LANES
    tm = min(tm, n)
    assert n % tm == 0
    return pl.pallas_call(
        _final_norm_kernel,
        grid=(n // tm,),
        in_specs=[pl.BlockSpec(_row_tile_shape(tm, d), lambda i: (i, 0, 0, 0)),
                  pl.BlockSpec(_row_tile_shape(tm, d), lambda i: (i, 2, 0, 0)),
                  pl.BlockSpec((1, d), lambda i: (0, 0))],
        out_specs=pl.BlockSpec((tm, d), lambda i: (i, 0)),
        out_shape=jax.ShapeDtypeStruct((n, d), F32),
        compiler_params=_params("parallel"),
        name="final_norm",
    )(z, z, g)


def moe_and_norm(z, aff, w_e_gate, w_e_up, w_e_down, g_final):
    n = z.shape[0] * SUBLANES
    cap = EC_CAPACITY * n // N_EXPERTS
    sel, pos, off = select(aff, cap)
    choff = off[:, :, 0].astype(I32).reshape(-1)
    idx, rec = compact(sel, pos, choff, aff, cap)
    z = expert_ffn(idx, rec, z, w_e_gate, w_e_up, w_e_down)
    return final_norm(z, g_final)


def encoder_group(x, mem, w, experts, g_final):
    b, t, d = x.shape
    z, aff = layer_front(x, mem, w)
    return moe_and_norm(z, aff, *experts, g_final).reshape(b, t, d)


def kernel(x_prompt, x_sample, mem_prompt, mem_sample, g_mix, g_mem, w_in, w_mem_kv, na_rpb, t5_table, sw_sink,
           w_na_o, w_sw_o, w_mx_o, w_out, g_ffn, w_router, w_e_gate, w_e_up, w_e_down, g_final):
    assert g_mix.shape[0] == 1, "single-layer trunk"
    w = prep_weights(g_mix[0], g_mem[0], w_in[0], w_mem_kv[0], na_rpb[0], t5_table, sw_sink[0], w_na_o[0],
                     w_sw_o[0], w_mx_o[0], w_out[0], g_ffn[0], w_router[0])
    experts = (w_e_gate[0].astype(BF16), w_e_up[0].astype(BF16), w_e_down[0])
    gf = g_final.reshape(1, -1).astype(F32)
    y_prompt = encoder_group(x_prompt, mem_prompt, w, experts, gf)
    y_sample = encoder_group(x_sample, mem_sample, w, experts, gf)
    return (y_prompt, y_sample)
```

```python
import functools
import math

import numpy as np
import jax
import jax.numpy as jnp
from jax import lax
from jax.experimental import pallas as pl
from jax.experimental.pallas import tpu as pltpu

F32 = jnp.float32
BF16 = jnp.bfloat16
I32 = jnp.int32

RMS_EPS = 1e-6
NEG_INF = -1e30

GRID_W = 64
NA_HEADS = 8
NA_HEAD_DIM = 64
NA_KR = 8
NA_KC = 16
SW_HEADS = 16
SW_KV_HEADS = 4
SW_HEAD_DIM = 64
SW_WINDOW = 128
SW_BLOCK = 128
MX_HEADS = 4
T5_BUCKETS = 32
T5_MAX_DIST = 128
N_BRANCHES = 3
N_EXPERTS = 16
EC_CAPACITY = 2

LANES = 128
SUBLANES = 8
V7X_VMEM_BYTES = 64 * 1024 * 1024
VMEM_LIMIT = V7X_VMEM_BYTES * 7 // 8

CHUNK_UNROLL = 8
EXPERT_TS = 512
MERGE_TN = 256

NT_DIMS = (((1,), (1,)), ((), ()))
TN_DIMS = (((0,), (0,)), ((), ()))


def _params(*sem):
    return pltpu.CompilerParams(dimension_semantics=sem, vmem_limit_bytes=VMEM_LIMIT)


def _rms(x, g):
    return x * lax.rsqrt(jnp.mean(x * x, axis=-1, keepdims=True) + RMS_EPS) * g


def _to_row_tiles(strip):
    return strip.reshape(strip.shape[0] // SUBLANES, SUBLANES, LANES)


def _from_row_tiles(tiles):
    return tiles.reshape(tiles.shape[0] * SUBLANES, LANES)


def _row_tile_shape(rows, d):
    return (rows // SUBLANES, d // LANES, SUBLANES, LANES)


def _norm_proj_kernel(x_ref, g_ref, w_ref, *o_refs, scales):
    h = _rms(x_ref[...], g_ref[...]).astype(BF16)
    off = 0
    for o_ref, sc in zip(o_refs, scales):
        width = o_ref.shape[1]
        for c0 in range(0, width, 512):
            cw = min(512, width - c0)
            r = jnp.dot(h, w_ref[:, off + c0:off + c0 + cw], preferred_element_type=F32)
            if sc != 1.0:
                r = r * sc
            o_ref[:, c0:c0 + cw] = r.astype(o_ref.dtype)
        off += width


def norm_proj(x, g, w, widths, scales, tm):
    n, d = x.shape
    tm = min(tm, n)
    assert n % tm == 0
    return pl.pallas_call(
        functools.partial(_norm_proj_kernel, scales=tuple(scales)),
        grid=(n // tm,),
        in_specs=[pl.BlockSpec((tm, d), lambda i: (i, 0)),
                  pl.BlockSpec((1, d), lambda i: (0, 0)),
                  pl.BlockSpec(w.shape, lambda i: (0, 0))],
        out_specs=[pl.BlockSpec((tm, c), lambda i: (i, 0)) for c in widths],
        out_shape=[jax.ShapeDtypeStruct((n, c), BF16) for c in widths],
        compiler_params=_params("parallel"),
        name="norm_proj",
    )(x, g, w)


def _softmax_pv(s, v):
    m = jnp.max(s, axis=-1, keepdims=True)
    e = jnp.exp(s - m)
    den = jnp.sum(e, axis=-1, keepdims=True)
    o = jnp.dot(e.astype(BF16), v, preferred_element_type=F32)
    return o / den


def _softmax_pv_half(s, v, extra_logit, use_lo):
    rows, keys = s.shape
    mb = jnp.maximum(jnp.broadcast_to(jnp.max(s, axis=-1, keepdims=True), (rows, LANES)), extra_logit)
    e = jnp.concatenate([jnp.exp(s[:, t * LANES:(t + 1) * LANES] - mb) for t in range(keys // LANES)],
                        axis=1).astype(BF16)
    lo = lax.broadcasted_iota(I32, v.shape, 1) < LANES // 2
    keep = lo if use_lo else jnp.logical_not(lo)
    o = jnp.dot(e, jnp.where(keep, v, jnp.ones_like(v)), preferred_element_type=F32)
    den = pltpu.roll(o + jnp.exp(extra_logit - mb), LANES // 2, 1)
    return o / den


def _na_kernel(q_ref, k_ref, v_ref, bias_ref, o_ref, *, rows, rb):
    j = pl.program_id(1)
    lo = lax.broadcasted_iota(I32, (GRID_W, LANES), 1) < NA_HEAD_DIM
    nkeys = NA_KR * GRID_W

    def body(i, carry):
        r = j * rb + i
        rs = jnp.clip(r - NA_KR // 2, 0, rows - NA_KR)
        off = r - rs
        q = q_ref[0, pl.ds(pl.multiple_of(i * GRID_W, GRID_W), GRID_W), :]
        kk = k_ref[0, pl.ds(pl.multiple_of(rs * GRID_W, GRID_W), nkeys), :]
        vv = v_ref[0, pl.ds(pl.multiple_of(rs * GRID_W, GRID_W), nkeys), :]
        scores = []
        for p in range(NA_HEADS // 2):
            qp = q[:, p * LANES:(p + 1) * LANES]
            kp = kk[:, p * LANES:(p + 1) * LANES]
            zero = jnp.zeros_like(qp)
            q2 = jnp.concatenate([jnp.where(lo, qp, zero), jnp.where(lo, zero, qp)], axis=0)
            s = lax.dot_general(q2, kp, NT_DIMS, preferred_element_type=F32)
            scores.append(s + bias_ref[off, 2 * p:2 * p + 2].reshape(2 * GRID_W, nkeys))
        outs = []
        for p in range(NA_HEADS // 2):
            o2 = _softmax_pv(scores[p], vv[:, p * LANES:(p + 1) * LANES])
            outs.append(jnp.where(lo, o2[:GRID_W], o2[GRID_W:]))
        o_ref[0, pl.ds(pl.multiple_of(i * GRID_W, GRID_W), GRID_W), :] = (
            jnp.concatenate(outs, axis=1).astype(o_ref.dtype))
        return carry

    lax.fori_loop(0, rb, body, 0, unroll=2)


def na_attention(q, k, v, bias, rb=8):
    b, t, c = q.shape
    rows = t // GRID_W
    assert rows >= NA_KR and rows % rb == 0
    return pl.pallas_call(
        functools.partial(_na_kernel, rows=rows, rb=rb),
        grid=(b, rows // rb),
        in_specs=[pl.BlockSpec((1, rb * GRID_W, c), lambda i, j: (i, j, 0)),
                  pl.BlockSpec((1, t, c), lambda i, j: (i, 0, 0)),
                  pl.BlockSpec((1, t, c), lambda i, j: (i, 0, 0)),
                  pl.BlockSpec(bias.shape, lambda i, j: (0, 0, 0, 0))],
        out_specs=pl.BlockSpec((1, rb * GRID_W, c), lambda i, j: (i, j, 0)),
        out_shape=jax.ShapeDtypeStruct((b, t, c), BF16),
        compiler_params=_params("parallel", "arbitrary"),
        name="na_attention",
    )(q, k, v, bias)


def na_bias_table(rpb):
    col = np.arange(GRID_W)
    col_start = np.clip(col - NA_KC // 2, 0, GRID_W - NA_KC)
    in_win = (col[None, :] >= col_start[:, None]) & (col[None, :] < col_start[:, None] + NA_KC)
    dc = np.clip(col[None, :] - col[:, None] + NA_KC - 1, 0, 2 * NA_KC - 2)
    heads = rpb.shape[0]
    by_col = jnp.take(rpb.astype(F32), jnp.asarray(dc.reshape(-1)), axis=2)
    by_col = by_col.reshape(heads, 2 * NA_KR - 1, GRID_W, GRID_W)
    by_col = jnp.where(jnp.asarray(in_win)[None, None], by_col, NEG_INF)
    per_off = [by_col[:, NA_KR - 1 - off:2 * NA_KR - 1 - off].transpose(0, 2, 1, 3) for off in range(NA_KR)]
    return jnp.stack(per_off).reshape(NA_KR, heads, GRID_W, NA_KR * GRID_W)


def _sw_kernel(sink_ref, q_ref, k_ref, v_ref, bias_ref, o_ref, s_scr, *, nb):
    n = pl.program_id(1)
    blk = SW_BLOCK
    group = SW_HEADS // SW_KV_HEADS

    def rows_of(ref, c):
        return ref[0, pl.ds(pl.multiple_of(c * blk, blk), blk), :]

    cl = jnp.maximum(n - 1, 0)
    cr = jnp.minimum(n + 1, nb - 1)
    k3 = jnp.concatenate([rows_of(k_ref, cl), rows_of(k_ref, n), rows_of(k_ref, cr)], axis=0)
    v3 = jnp.concatenate([rows_of(v_ref, cl), rows_of(v_ref, n), rows_of(v_ref, cr)], axis=0)
    pen_l = jnp.where(n > 0, 0.0, NEG_INF).astype(F32)
    pen_r = jnp.where(n < nb - 1, 0.0, NEG_INF).astype(F32)
    key = lax.broadcasted_iota(I32, (1, 3 * blk), 1)
    pen = jnp.where(key < blk, pen_l, jnp.where(key >= 2 * blk, pen_r, 0.0))
    lo = lax.broadcasted_iota(I32, (blk, LANES), 1) < SW_HEAD_DIM

    for c in range(SW_KV_HEADS):
        pair, half = divmod(c, 2)
        kp = k3[:, pair * LANES:(pair + 1) * LANES]
        keep = lo if half == 0 else jnp.logical_not(lo)
        qs = []
        for g in range(group):
            t = pair * group + g
            qt = q_ref[0, :, t * LANES:(t + 1) * LANES]
            qs.append(jnp.where(keep, qt, jnp.zeros_like(qt)))
        qq = jnp.concatenate(qs, axis=0)
        s = lax.dot_general(qq, kp, NT_DIMS, preferred_element_type=F32)
        s_scr[c] = s + bias_ref[c * group:(c + 1) * group].reshape(group * blk, 3 * blk) + pen

    for pair in range(SW_KV_HEADS // 2):
        vp = v3[:, pair * LANES:(pair + 1) * LANES]
        per_half = []
        for half in range(2):
            c = 2 * pair + half
            sink = jnp.concatenate(
                [jnp.full((blk, LANES), sink_ref[c * group + g], F32) for g in range(group)], axis=0)
            per_half.append(_softmax_pv_half(s_scr[c], vp, sink, use_lo=(half == 0)))
        for g in range(group):
            t = pair * group + g
            o = jnp.where(lo, per_half[0][g * blk:(g + 1) * blk], per_half[1][g * blk:(g + 1) * blk])
            o_ref[0, :, t * LANES:(t + 1) * LANES] = o.astype(o_ref.dtype)


def sw_attention(q, k, v, bias, sink):
    b, t, c = q.shape
    nb = t // SW_BLOCK
    kvw = k.shape[2]
    grid_spec = pltpu.PrefetchScalarGridSpec(
        num_scalar_prefetch=1,
        grid=(b, nb),
        in_specs=[pl.BlockSpec((1, SW_BLOCK, c), lambda i, j, s: (i, j, 0)),
                  pl.BlockSpec((1, t, kvw), lambda i, j, s: (i, 0, 0)),
                  pl.BlockSpec((1, t, kvw), lambda i, j, s: (i, 0, 0)),
                  pl.BlockSpec(bias.shape, lambda i, j, s: (0, 0, 0))],
        out_specs=pl.BlockSpec((1, SW_BLOCK, c), lambda i, j, s: (i, j, 0)),
        scratch_shapes=[pltpu.VMEM((SW_KV_HEADS, (SW_HEADS // SW_KV_HEADS) * SW_BLOCK, 3 * SW_BLOCK), F32)],
    )
    return pl.pallas_call(
        functools.partial(_sw_kernel, nb=nb),
        grid_spec=grid_spec,
        out_shape=jax.ShapeDtypeStruct((b, t, c), BF16),
        compiler_params=_params("parallel", "arbitrary"),
        name="sw_attention",
    )(sink, q, k, v, bias)


def _t5_bucket(rel):
    half = T5_BUCKETS // 2
    max_exact = half // 2
    n = jnp.abs(rel)
    nf = jnp.maximum(n, 1).astype(F32)
    large = max_exact + (jnp.log(nf / max_exact) / math.log(T5_MAX_DIST / max_exact)
                         * (half - max_exact)).astype(jnp.int32)
    large = jnp.minimum(large, half - 1)
    return jnp.where(rel > 0, half, 0) + jnp.where(n < max_exact, n, large)


def sw_bias_table(t5_table):
    span = SW_BLOCK + 2 * SW_WINDOW
    rel = np.arange(span)[None, :] - SW_WINDOW - np.arange(SW_BLOCK)[:, None]
    bucket = _t5_bucket(jnp.asarray(rel, dtype=jnp.int32))[None]
    table = t5_table.astype(F32).T[:, :, None, None]
    bias = jnp.zeros((t5_table.shape[1], SW_BLOCK, span), F32)
    for b in range(T5_BUCKETS):
        bias = jnp.where(bucket == b, table[:, b], bias)
    return jnp.where(jnp.asarray(np.abs(rel) <= SW_WINDOW)[None], bias, NEG_INF)


def sw_pair_heads(w, axis):
    group = SW_HEADS // SW_KV_HEADS
    shape = w.shape
    split = shape[:axis] + (SW_KV_HEADS // 2, 2, group, SW_HEAD_DIM) + shape[axis + 1:]
    order = list(range(len(split)))
    order[axis + 1], order[axis + 2] = axis + 2, axis + 1
    return w.reshape(split).transpose(order).reshape(shape)


def _mx_kernel(q_ref, mk_ref, mv_ref, o_ref, *, scale):
    hd = q_ref.shape[2] // MX_HEADS
    heads = [slice(h * hd, (h + 1) * hd) for h in range(MX_HEADS)]
    scores = [lax.dot_general(q_ref[0, :, sl], mk_ref[0, :, sl], NT_DIMS, preferred_element_type=F32) * scale
              for sl in heads]
    for sl, s in zip(heads, scores):
        o_ref[0, :, sl] = _softmax_pv(s, mv_ref[0, :, sl]).astype(o_ref.dtype)


def mx_attention(q, mkv, tq=512):
    b, t, c = q.shape
    m = mkv.shape[1]
    tq = min(tq, t)
    return pl.pallas_call(
        functools.partial(_mx_kernel, scale=float((c // MX_HEADS) ** -0.5)),
        grid=(b, t // tq),
        in_specs=[pl.BlockSpec((1, tq, c), lambda i, j: (i, j, 0)),
                  pl.BlockSpec((1, m, c), lambda i, j: (i, 0, 0)),
                  pl.BlockSpec((1, m, c), lambda i, j: (i, 0, 1))],
        out_specs=pl.BlockSpec((1, tq, c), lambda i, j: (i, j, 0)),
        out_shape=jax.ShapeDtypeStruct((b, t, c), BF16),
        compiler_params=_params("parallel", "arbitrary"),
        name="mx_attention",
    )(q, mkv, mkv)


def _merge_kernel(x_ref, ona_ref, osw_ref, omx_ref, gmix_ref, wg_ref, wna_ref, wsw_ref, wmx_ref, wout_ref,
                  gffn_ref, wrh_ref, wrl_ref, z_ref, aff_ref, h_scr):
    j = pl.program_id(1)
    nstrip = z_ref.shape[1] // 3
    strips = range(nstrip)

    @pl.when(j == 0)
    def _():
        x = x_ref[...]
        h_scr[...] = _rms(x, gmix_ref[...]).astype(BF16)
        for c in strips:
            z_ref[:, c] = _to_row_tiles(x[:, c * LANES:(c + 1) * LANES])

    gates = jax.nn.sigmoid(jnp.dot(h_scr[...], wg_ref[...], preferred_element_type=F32))
    tn = wna_ref.shape[1]
    merged = None
    for o_ref, w_ref, b in ((ona_ref, wna_ref, 0), (osw_ref, wsw_ref, 1), (omx_ref, wmx_ref, 2)):
        term = gates[:, b * tn:(b + 1) * tn] * jnp.dot(o_ref[...], w_ref[...], preferred_element_type=F32)
        merged = term if merged is None else merged + term
    part = jnp.dot(merged.astype(BF16), wout_ref[...], preferred_element_type=F32)
    for c in strips:
        z_ref[:, c] += _to_row_tiles(part[:, c * LANES:(c + 1) * LANES])

    @pl.when(j == pl.num_programs(1) - 1)
    def _():
        x2 = jnp.concatenate([_from_row_tiles(z_ref[:, c]) for c in strips], axis=1)
        h2 = _rms(x2, gffn_ref[...])
        for c in strips:
            z_ref[:, nstrip + c] = _to_row_tiles(h2[:, c * LANES:(c + 1) * LANES])
            z_ref[:, 2 * nstrip + c] = jnp.zeros((z_ref.shape[0], SUBLANES, LANES), F32)
        hi = h2.astype(BF16)
        lo = (h2 - hi.astype(F32)).astype(BF16)
        wh = wrh_ref[...]
        logits = (lax.dot_general(wh, hi, NT_DIMS, preferred_element_type=F32)
                  + lax.dot_general(wh, lo, NT_DIMS, preferred_element_type=F32)
                  + lax.dot_general(wrl_ref[...], hi, NT_DIMS, preferred_element_type=F32))
        m = jnp.max(logits, axis=0, keepdims=True)
        e = jnp.exp(logits - m)
        aff = e / jnp.sum(e, axis=0, keepdims=True)
        for c in range(aff_ref.shape[0]):
            aff_ref[c] = aff[:, c * LANES:(c + 1) * LANES]


def merge(x, o_na, o_sw, o_mx, g_mix, w_gate, w_na_o, w_sw_o, w_mx_o, w_out, g_ffn, wr_hi, wr_lo,
          tm=512, tn=MERGE_TN):
    n, d = x.shape
    tm = min(tm, n)
    assert n % tm == 0 and tm % LANES == 0 and d % LANES == 0
    ne = wr_hi.shape[0]
    row = lambda i, j: (i, 0)
    return pl.pallas_call(
        _merge_kernel,
        grid=(n // tm, d // tn),
        in_specs=[pl.BlockSpec((tm, d), row),
                  pl.BlockSpec((tm, o_na.shape[1]), row),
                  pl.BlockSpec((tm, o_sw.shape[1]), row),
                  pl.BlockSpec((tm, o_mx.shape[1]), row),
                  pl.BlockSpec((1, d), lambda i, j: (0, 0)),
                  pl.BlockSpec((d, N_BRANCHES * tn), lambda i, j: (0, j)),
                  pl.BlockSpec((w_na_o.shape[0], tn), lambda i, j: (0, j)),
                  pl.BlockSpec((w_sw_o.shape[0], tn), lambda i, j: (0, j)),
                  pl.BlockSpec((w_mx_o.shape[0], tn), lambda i, j: (0, j)),
                  pl.BlockSpec((tn, d), lambda i, j: (j, 0)),
                  pl.BlockSpec((1, d), lambda i, j: (0, 0)),
                  pl.BlockSpec((ne, d), lambda i, j: (0, 0)),
                  pl.BlockSpec((ne, d), lambda i, j: (0, 0))],
        out_specs=[pl.BlockSpec(_row_tile_shape(tm, 3 * d), lambda i, j: (i, 0, 0, 0)),
                   pl.BlockSpec((tm // LANES, ne, LANES), lambda i, j: (i, 0, 0))],
        out_shape=[jax.ShapeDtypeStruct(_row_tile_shape(n, 3 * d), F32),
                   jax.ShapeDtypeStruct((n // LANES, ne, LANES), F32)],
        scratch_shapes=[pltpu.VMEM((tm, d), BF16)],
        compiler_params=_params("parallel", "arbitrary"),
        name="merge",
    )(x, o_na, o_sw, o_mx, g_mix, w_gate, w_na_o, w_sw_o, w_mx_o, w_out, g_ffn, wr_hi, wr_lo)


NA_WIDTH = NA_HEADS * NA_HEAD_DIM
SW_WIDTH = SW_HEADS * SW_HEAD_DIM
SW_KV_WIDTH = SW_KV_HEADS * SW_HEAD_DIM
QKV_WIDTHS = (NA_WIDTH, NA_WIDTH, NA_WIDTH, SW_WIDTH, SW_KV_WIDTH, SW_KV_WIDTH)


def prep_weights(g_mix, g_mem, w_in, w_mem_kv, na_rpb, t5_table, sw_sink, w_na_o, w_sw_o, w_mx_o, w_out,
                 g_ffn, w_router):
    d = w_in.shape[0]
    mx_width = w_mx_o.shape[0]
    widths = QKV_WIDTHS + (mx_width,)
    offs = np.concatenate([[0], np.cumsum(widths)])
    cols = [w_in[:, offs[i]:offs[i + 1]] for i in range(len(widths))]
    cols[3] = sw_pair_heads(cols[3], axis=1)
    w_gate = (w_in[:, offs[-1]:].reshape(d, N_BRANCHES, d // MERGE_TN, MERGE_TN).transpose(0, 2, 1, 3)
              .reshape(d, N_BRANCHES * d))
    wr = w_router.T.astype(F32)
    wr_hi = wr.astype(BF16)
    return dict(
        g_mix=g_mix.reshape(1, d).astype(F32), g_mem=g_mem.reshape(1, d).astype(F32),
        g_ffn=g_ffn.reshape(1, d).astype(F32),
        w_qkv=jnp.concatenate(cols, axis=1).astype(BF16), qkv_widths=widths,
        qkv_scales=(NA_HEAD_DIM ** -0.5, 1.0, 1.0, SW_HEAD_DIM ** -0.5, 1.0, 1.0, 1.0),
        w_mem_kv=w_mem_kv.astype(BF16),
        na_bias=na_bias_table(na_rpb), sw_bias=sw_bias_table(t5_table), sw_sink=sw_sink.astype(F32),
        w_gate=w_gate.astype(BF16), w_na_o=w_na_o.astype(BF16), w_sw_o=sw_pair_heads(w_sw_o, axis=0).astype(BF16),
        w_mx_o=w_mx_o.astype(BF16), w_out=w_out.astype(BF16),
        wr_hi=wr_hi, wr_lo=(wr - wr_hi.astype(F32)).astype(BF16))


def layer_front(x, mem, w):
    b, t, d = x.shape
    m = mem.shape[1]
    xf = x.reshape(b * t, d)
    na_q, na_k, na_v, sw_q, sw_k, sw_v, mx_q = norm_proj(xf, w["g_mix"], w["w_qkv"], w["qkv_widths"],
                                                         w["qkv_scales"], tm=512)
    (mkv,) = norm_proj(mem.reshape(b * m, d), w["g_mem"], w["w_mem_kv"], (w["w_mem_kv"].shape[1],), (1.0,),
                       tm=512)
    r3 = lambda a: a.reshape(b, t, a.shape[1])
    o_na = na_attention(r3(na_q), r3(na_k), r3(na_v), w["na_bias"])
    o_sw = sw_attention(r3(sw_q), r3(sw_k), r3(sw_v), w["sw_bias"], w["sw_sink"])
    o_mx = mx_attention(r3(mx_q), mkv.reshape(b, m, mkv.shape[1]))
    f2 = lambda a: a.reshape(b * t, a.shape[2])
    return merge(xf, f2(o_na), f2(o_sw), f2(o_mx), w["g_mix"], w["w_gate"], w["w_na_o"], w["w_sw_o"],
                 w["w_mx_o"], w["w_out"], w["g_ffn"], w["wr_hi"], w["wr_lo"])


def _select_kernel(aff_ref, sel_ref, pos_ref, off_ref, *, cap):
    nc, ne, _ = aff_ref.shape
    bits = lax.bitcast_convert_type(aff_ref[...], I32)
    tok = (lax.broadcasted_iota(I32, bits.shape, 0) * LANES + lax.broadcasted_iota(I32, bits.shape, 2))

    def count(flags):
        return jnp.sum(flags, axis=(0, 2), keepdims=True)

    def value_step(i, prefix):
        cand = prefix | lax.shift_left(jnp.int32(1), 30 - i)
        cnt = count(jnp.where(bits >= cand, 1.0, 0.0))
        return jnp.where(cnt >= cap, cand, prefix)

    tau = lax.fori_loop(0, 31, value_step, jnp.zeros((1, ne, 1), I32))
    gt = bits > tau
    eq = bits == tau
    need = cap - count(jnp.where(gt, 1.0, 0.0))

    def index_step(i, last):
        cand = last | lax.shift_left(jnp.int32(1), 15 - i)
        cnt = count(jnp.where(eq, jnp.where(tok < cand, 1.0, 0.0), 0.0))
        return jnp.where(cnt < need, cand, last)

    last = lax.fori_loop(0, 16, index_step, jnp.zeros((1, ne, 1), I32))
    sel_ref[...] = jnp.where(gt, 1.0, jnp.where(eq, jnp.where(tok <= last, 1.0, 0.0), 0.0))

    upper = (lax.broadcasted_iota(I32, (LANES, LANES), 0) <= lax.broadcasted_iota(I32, (LANES, LANES), 1))
    upper = jnp.where(upper, 1.0, 0.0).astype(BF16)

    def chunk_step(c, off):
        s = sel_ref[c]
        cum = jnp.dot(s.astype(BF16), upper, preferred_element_type=F32)
        pos_ref[c] = off + cum - s
        off_ref[c] = jnp.broadcast_to(off, s.shape)
        return off + cum[:, LANES - 1:LANES]

    lax.fori_loop(0, nc, chunk_step, jnp.zeros((ne, 1), F32), unroll=CHUNK_UNROLL)


def select(aff, cap):
    nc, ne, _ = aff.shape
    assert nc * LANES <= 65536
    shape = jax.ShapeDtypeStruct(aff.shape, F32)
    return pl.pallas_call(
        functools.partial(_select_kernel, cap=float(cap)),
        out_shape=[shape, shape, shape],
        compiler_params=pltpu.CompilerParams(vmem_limit_bytes=VMEM_LIMIT),
        name="select",
    )(aff)


TOK_COL, CHUNK_COL, HI_COL = 0, 1, 2


def _compact_kernel(choff_ref, sel_ref, pos_ref, aff_ref, o_ref, tv_scr):
    e = pl.program_id(0)
    nc, ne, _ = aff_ref.shape
    win = 2 * LANES
    col = lax.broadcasted_iota(I32, (LANES, LANES), 1)
    row = lax.broadcasted_iota(I32, (LANES, LANES), 0)

    @pl.when(e == 0)
    def _():
        ecol = lax.broadcasted_iota(I32, (ne, LANES), 1) - lax.broadcasted_iota(I32, (ne, LANES), 0)
        place = [jnp.where(ecol == HI_COL + k * ne, 1.0, 0.0).astype(BF16) for k in range(3)]

        def build(c, carry):
            a = aff_ref[c]
            hi = a.astype(BF16)
            r1 = a - hi.astype(F32)
            mid = r1.astype(BF16)
            lo = (r1 - mid.astype(F32)).astype(BF16)
            rec = (lax.dot_general(hi, place[0], TN_DIMS, preferred_element_type=F32)
                   + lax.dot_general(mid, place[1], TN_DIMS, preferred_element_type=F32)
                   + lax.dot_general(lo, place[2], TN_DIMS, preferred_element_type=F32))
            rec = rec + jnp.where(col == TOK_COL, row, jnp.where(col == CHUNK_COL, c, 0)).astype(F32)
            tv_scr[c] = rec.astype(BF16)
            return carry

        lax.fori_loop(0, nc, build, 0, unroll=CHUNK_UNROLL)

    o_ref[...] = jnp.zeros_like(o_ref)
    slot0 = lax.broadcasted_iota(I32, (win, LANES), 0).astype(F32)

    def body(c, carry):
        off = choff_ref[c * ne + e]
        base = pl.multiple_of((off // LANES) * LANES, LANES)
        s = sel_ref[c, pl.ds(e, 1), :]
        p = pos_ref[c, pl.ds(e, 1), :] - base.astype(F32)
        onehot = jnp.where(s > 0.0, jnp.where(slot0 == p, 1.0, 0.0), 0.0).astype(BF16)
        o_ref[0, pl.ds(base, win), :] += jnp.dot(onehot, tv_scr[c], preferred_element_type=F32)
        return carry

    lax.fori_loop(0, nc, body, 0, unroll=CHUNK_UNROLL)


def compact(sel, pos, choff, aff, cap):
    nc, ne, _ = aff.shape
    assert nc <= 256 and HI_COL + 3 * ne <= LANES and cap % LANES == 0
    rows = cap + 2 * LANES
    full = pl.BlockSpec(aff.shape, lambda e, s: (0, 0, 0))
    rec = pl.pallas_call(
        _compact_kernel,
        grid_spec=pltpu.PrefetchScalarGridSpec(
            num_scalar_prefetch=1, grid=(ne,),
            in_specs=[full, full, full],
            out_specs=pl.BlockSpec((1, rows, LANES), lambda e, s: (e, 0, 0)),
            scratch_shapes=[pltpu.VMEM((nc, LANES, LANES), BF16)]),
        out_shape=jax.ShapeDtypeStruct((ne, rows, LANES), F32),
        compiler_params=_params("arbitrary"),
        name="compact",
    )(choff, sel, pos, aff)
    idx = (rec[:, :cap, CHUNK_COL] * LANES + rec[:, :cap, TOK_COL]).astype(I32)
    return idx, rec


def _expert_kernel(idx_prev, idx_cur, idx_next, rec_ref, zin_ref, wg_ref, wu_ref, wd_ref, z_ref,
                   buf, xs, acc, gsem, ssem, *, ts, nf, s_tiles, ne):
    del zin_ref
    sub = z_ref.shape[1] // 3
    t = pl.program_id(0)
    f = pl.program_id(1)
    last_t = pl.num_programs(0) - 1
    rps = ts // nf
    slot, nslot, pslot = t % 3, (t + 1) % 3, (t + 2) % 3
    odd = (t // s_tiles) % 2
    nodd = (jnp.minimum(t + 1, last_t) // s_tiles) % 2
    podd = (jnp.maximum(t - 1, 0) // s_tiles) % 2

    def rows(ref, row, first, count):
        return ref.at[row >> 3, pl.ds(first, count), pl.ds(row & (SUBLANES - 1), 1), :]

    def gather(token, ra, rs, to_slot, is_odd):
        return pltpu.make_async_copy(rows(z_ref, token, is_odd * sub, 2 * sub),
                                     buf.at[to_slot, ra, :, pl.ds(rs, 1), :], gsem.at[to_slot])

    def scatter(token, ra, rs, from_slot, is_odd):
        return pltpu.make_async_copy(buf.at[from_slot, ra, pl.ds(is_odd * sub, sub), pl.ds(rs, 1), :],
                                     rows(z_ref, token, is_odd * 2 * sub, sub), ssem.at[from_slot])

    def wait_gather(s):
        pltpu.make_async_copy(z_ref.at[pl.ds(0, ts // SUBLANES), pl.ds(0, 2 * sub)], buf.at[s], gsem.at[s]).wait()

    def wait_scatter(s):
        pltpu.make_async_copy(buf.at[s, :, pl.ds(0, sub)], z_ref.at[pl.ds(0, ts // SUBLANES), pl.ds(0, sub)],
                              ssem.at[s]).wait()

    def for_rows(fn):
        def body(r, carry):
            fn(idx_cur[0, 0, r], r >> 3, r & (SUBLANES - 1))
            return carry
        lax.fori_loop(0, ts, body, 0)

    @pl.when(f == 0)
    def _():
        @pl.when(t == 0)
        def _():
            def start(token, ra, rs):
                gather(token, ra, rs, 0, 0).start()
                gather(token, ra, rs, 2, 0).start()
            for_rows(start)
            wait_gather(2)

        acc[...] = jnp.zeros(acc.shape, F32)
        wait_gather(slot)
        h2_first = (1 - odd) * sub
        for c in range(sub):
            xs[:, c * LANES:(c + 1) * LANES] = _from_row_tiles(buf[slot, :, h2_first + c]).astype(BF16)

    for k in range(rps):
        r = f * rps + k
        ra, rs = f * (rps // SUBLANES) + k // SUBLANES, k % SUBLANES
        gather(idx_next[0, 0, r], ra, rs, nslot, nodd).start()
        scatter(idx_prev[0, 0, r], ra, rs, pslot, podd).start()

    x = xs[...]
    a = jnp.dot(x, wg_ref[0], preferred_element_type=F32)
    b = jnp.dot(x, wu_ref[0].astype(BF16), preferred_element_type=F32)
    hm = (jax.nn.silu(a) * b).astype(BF16)
    acc[...] += jnp.dot(hm, wd_ref[0].astype(BF16), preferred_element_type=F32)

    @pl.when(f == nf - 1)
    def _():
        lane = lax.broadcasted_iota(I32, (1, LANES), 1) - (HI_COL + t // s_tiles)
        own = (lane == 0) | (lane == ne) | (lane == 2 * ne)
        contrib = acc[...] * jnp.sum(jnp.where(own, rec_ref[0], 0.0), axis=1, keepdims=True)
        y_first = odd * sub
        for c in range(sub):
            buf[slot, :, y_first + c] += _to_row_tiles(contrib[:, c * LANES:(c + 1) * LANES])
        wait_scatter(pslot)

        @pl.when(t == last_t)
        def _():
            wait_gather(nslot)
            for_rows(lambda token, ra, rs: scatter(token, ra, rs, slot, odd).start())
            wait_scatter(slot)


def expert_ffn(idx, rec, z, w_gate, w_up, w_down, tf=512):
    ne, cap = idx.shape
    d = z.shape[1] // 3 * LANES
    ff = w_gate.shape[2]
    ts = EXPERT_TS
    tf = min(tf, ff)
    nf = ff // tf
    s_tiles = cap // ts
    assert cap % ts == 0 and s_tiles >= 2 and ts % nf == 0
    nt = ne * s_tiles
    idx3 = idx.reshape(nt, 1, ts)
    smem_tile = lambda shift: pl.BlockSpec(
        (1, 1, ts), lambda t, f: (jnp.clip(t + shift, 0, nt - 1), 0, 0), memory_space=pltpu.SMEM)
    return pl.pallas_call(
        functools.partial(_expert_kernel, ts=ts, nf=nf, s_tiles=s_tiles, ne=ne),
        grid=(nt, nf),
        in_specs=[smem_tile(-1), smem_tile(0), smem_tile(1),
                  pl.BlockSpec((1, ts, LANES), lambda t, f: (t // s_tiles, t % s_tiles, 0)),
                  pl.BlockSpec(memory_space=pl.ANY),
                  pl.BlockSpec((1, d, tf), lambda t, f: (t // s_tiles, 0, f)),
                  pl.BlockSpec((1, d, tf), lambda t, f: (t // s_tiles, 0, f)),
                  pl.BlockSpec((1, tf, d), lambda t, f: (t // s_tiles, f, 0))],
        out_specs=pl.BlockSpec(memory_space=pl.ANY),
        out_shape=jax.ShapeDtypeStruct(z.shape, F32),
        scratch_shapes=[pltpu.VMEM((3,) + _row_tile_shape(ts, 2 * d), F32), pltpu.VMEM((ts, d), BF16),
                        pltpu.VMEM((ts, d), F32), pltpu.SemaphoreType.DMA((3,)), pltpu.SemaphoreType.DMA((3,))],
        input_output_aliases={4: 0},
        compiler_params=_params("arbitrary", "arbitrary"),
        name="expert_ffn",
    )(idx3, idx3, idx3, rec, z, w_gate, w_up, w_down)


def _final_norm_kernel(y0_ref, y1_ref, g_ref, o_ref):
    x = jnp.concatenate([_from_row_tiles(y0_ref[:, c] + y1_ref[:, c]) for c in range(y0_ref.shape[1])], axis=1)
    o_ref[...] = _rms(x, g_ref[...])


def final_norm(z, g, tm=512):
    n, d = z.shape[0] * SUBLANES, z.shape[1] // 3 * LANES
    tm = min(tm, n)
    assert n % tm == 0
    return pl.pallas_call(
        _final_norm_kernel,
        grid=(n // tm,),
        in_specs=[pl.BlockSpec(_row_tile_shape(tm, d), lambda i: (i, 0, 0, 0)),
                  pl.BlockSpec(_row_tile_shape(tm, d), lambda i: (i, 2, 0, 0)),
                  pl.BlockSpec((1, d), lambda i: (0, 0))],
        out_specs=pl.BlockSpec((tm, d), lambda i: (i, 0)),
        out_shape=jax.ShapeDtypeStruct((n, d), F32),
        compiler_params=_params("parallel"),
        name="final_norm",
    )(z, z, g)


def moe_and_norm(z, aff, w_e_gate, w_e_up, w_e_down, g_final):
    n = z.shape[0] * SUBLANES
    cap = EC_CAPACITY * n // N_EXPERTS
    sel, pos, off = select(aff, cap)
    choff = off[:, :, 0].astype(I32).reshape(-1)
    idx, rec = compact(sel, pos, choff, aff, cap)
    z = expert_ffn(idx, rec, z, w_e_gate, w_e_up, w_e_down)
    return final_norm(z, g_final)


def encoder_group(x, mem, w, experts, g_final):
    b, t, d = x.shape
    z, aff = layer_front(x, mem, w)
    return moe_and_norm(z, aff, *experts, g_final).reshape(b, t, d)


def kernel(x_prompt, x_sample, mem_prompt, mem_sample, g_mix, g_mem, w_in, w_mem_kv, na_rpb, t5_table, sw_sink,
           w_na_o, w_sw_o, w_mx_o, w_out, g_ffn, w_router, w_e_gate, w_e_up, w_e_down, g_final):
    assert g_mix.shape[0] == 1, "single-layer trunk"
    w = prep_weights(g_mix[0], g_mem[0], w_in[0], w_mem_kv[0], na_rpb[0], t5_table, sw_sink[0], w_na_o[0],
                     w_sw_o[0], w_mx_o[0], w_out[0], g_ffn[0], w_router[0])
    experts = (w_e_gate[0].astype(BF16), w_e_up[0], w_e_down[0])
    gf = g_final.reshape(1, -1).astype(F32)
    y_prompt = encoder_group(x_prompt, mem_prompt, w, experts, gf)
    y_sample = encoder_group(x_sample, mem_sample, w, experts, gf)
    return (y_prompt, y_sample)
```
